```python
import math
import jax, jax.numpy as jnp
from jax import lax
import numpy as np

D_MODEL = 4096
BATCH = 4
SEQ = 4096
DEPTH = 2

N_MIXERS = 2
N_A_LAYERS = (DEPTH + 1) // 2
N_B_LAYERS = DEPTH // 2

A_HEADS = 32
A_DK = D_MODEL // A_HEADS
A_DV = D_MODEL // A_HEADS
A_QK_W = A_HEADS * A_DK
A_V_W = A_HEADS * A_DV
A_CONV_CH = 2 * A_QK_W + A_V_W
A_CONV = 4
A_CHUNK = 64
A_IN_W = A_CONV_CH + A_V_W + 2 * A_HEADS

B_HEADS = 16
B_DH = D_MODEL // (2 * B_HEADS)
B_Q_W = 2 * B_HEADS * B_DH
B_V_W = 2 * B_HEADS * B_DH
B_IN_W = 2 * B_Q_W + 2 * B_V_W
Q_BLOCK = 128

REL_BUCKETS = 32
REL_MAX_DIST = 128

DEEPNORM_ALPHA = (2.0 * DEPTH) ** 0.25
DEEPNORM_BETA = (8.0 * DEPTH) ** -0.25

LN_EPS = 1e-5
RMS_EPS = 1e-6
L2_EPS = 1e-6

kernel_name = "hybrid_deltanet_diffattn_deepnorm"


def layer_norm(x, g, b):
    xf = x.astype(jnp.float32)
    mu = jnp.mean(xf, axis=-1, keepdims=True)
    var = jnp.mean(jnp.square(xf - mu), axis=-1, keepdims=True)
    y = (xf - mu) * lax.rsqrt(var + LN_EPS)
    return (y * g.astype(jnp.float32) + b.astype(jnp.float32)).astype(x.dtype)


def rms_norm(x, w, eps):
    xf = x.astype(jnp.float32)
    y = xf * lax.rsqrt(jnp.mean(jnp.square(xf), axis=-1, keepdims=True) + eps)
    return y * w.astype(jnp.float32)


def l2norm(x):
    return x * lax.rsqrt(jnp.sum(jnp.square(x), axis=-1, keepdims=True) + L2_EPS)


def causal_short_conv(x, w):
    T = x.shape[1]
    K = w.shape[-1]
    xp = jnp.pad(x, ((0, 0), (K - 1, 0), (0, 0)))
    y = xp[:, 0:T, :] * w[:, 0]
    for j in range(1, K):
        y = y + xp[:, j:j + T, :] * w[:, j]
    return y


def gated_delta_rule_chunked(q, k, v, g, beta):
    Bn, H, T, dk = q.shape
    dv = v.shape[-1]
    C = A_CHUNK
    N = T // C
    q = q.reshape(Bn, H, N, C, dk)
    k = k.reshape(Bn, H, N, C, dk)
    v = v.reshape(Bn, H, N, C, dv)
    g = g.reshape(Bn, H, N, C)
    beta = beta.reshape(Bn, H, N, C)

    gc = jnp.cumsum(g, axis=-1)
    k_beta = k * beta[..., None]
    v_beta = v * beta[..., None]
    idx = jnp.arange(C)
    lower_incl = idx[:, None] >= idx[None, :]
    strict = idx[:, None] > idx[None, :]
    diff = gc[..., :, None] - gc[..., None, :]
    decay_mask = jnp.exp(jnp.where(lower_incl, diff, -jnp.inf))

    L = jnp.where(strict, jnp.einsum('bhncd,bhnsd->bhncs', k_beta, k) * decay_mask, 0.0)
    eye = jnp.eye(C, dtype=q.dtype)
    T_mat = lax.linalg.triangular_solve(eye + L, jnp.broadcast_to(eye, L.shape),
                                        left_side=True, lower=True, unit_diagonal=True)
    w = jnp.einsum('bhncs,bhnsd->bhncd', T_mat, k_beta * jnp.exp(gc)[..., None])
    u = jnp.einsum('bhncs,bhnse->bhnce', T_mat, v_beta)
    attn_intra = jnp.einsum('bhncd,bhnsd->bhncs', q, k) * decay_mask
    q_decay = q * jnp.exp(gc)[..., None]
    k_state = k * jnp.exp(gc[..., -1:] - gc)[..., None]
    chunk_decay = jnp.exp(gc[..., -1])

    xs = (jnp.moveaxis(w, 2, 0), jnp.moveaxis(u, 2, 0), jnp.moveaxis(attn_intra, 2, 0),
          jnp.moveaxis(q_decay, 2, 0), jnp.moveaxis(k_state, 2, 0), jnp.moveaxis(chunk_decay, 2, 0))

    def step(S, inp):
        w_c, u_c, a_c, qd_c, ks_c, cd_c = inp
        v_new = u_c - jnp.einsum('bhcd,bhde->bhce', w_c, S)
        o = jnp.einsum('bhcd,bhde->bhce', qd_c, S) + jnp.einsum('bhcs,bhse->bhce', a_c, v_new)
        S = S * cd_c[..., None, None] + jnp.einsum('bhcd,bhce->bhde', ks_c, v_new)
        return S, o

    S0 = jnp.zeros((Bn, H, dk, dv), dtype=q.dtype)
    _, outs = lax.scan(step, S0, xs)
    return jnp.moveaxis(outs, 0, 2).reshape(Bn, H, T, dv)


def gated_deltanet_branch(x, w_in, conv_w, a_log, dt_bias, norm_w, w_out):
    Bn, T, _ = x.shape
    h = x @ w_in
    qkv, z, a, b = jnp.split(h, [A_CONV_CH, A_CONV_CH + A_V_W, A_CONV_CH + A_V_W + A_HEADS], axis=-1)
    qkv = jax.nn.silu(causal_short_conv(qkv, conv_w)).astype(jnp.float32)
    q, k, v = jnp.split(qkv, [A_QK_W, 2 * A_QK_W], axis=-1)
    q = l2norm(q.reshape(Bn, T, A_HEADS, A_DK)) * (A_DK ** -0.5)
    k = l2norm(k.reshape(Bn, T, A_HEADS, A_DK))
    v = v.reshape(Bn, T, A_HEADS, A_DV)
    beta = jax.nn.sigmoid(b.astype(jnp.float32))
    g = -jnp.exp(a_log.astype(jnp.float32)) * jax.nn.softplus(
        a.astype(jnp.float32) + dt_bias.astype(jnp.float32))
    o = gated_delta_rule_chunked(q.transpose(0, 2, 1, 3), k.transpose(0, 2, 1, 3),
                                 v.transpose(0, 2, 1, 3), g.transpose(0, 2, 1),
                                 beta.transpose(0, 2, 1))
    o = o.transpose(0, 2, 1, 3)
    gate = jax.nn.silu(z.astype(jnp.float32)).reshape(Bn, T, A_HEADS, A_DV)
    o = rms_norm(o, norm_w, RMS_EPS) * gate
    return o.reshape(Bn, T, A_V_W).astype(x.dtype) @ w_out


def t5_causal_bucket(rel):
    n = jnp.maximum(rel, 0)
    max_exact = REL_BUCKETS // 2
    nf = jnp.maximum(n, 1).astype(jnp.float32)
    large = max_exact + (jnp.log(nf / max_exact) / math.log(REL_MAX_DIST / max_exact)
                         * (REL_BUCKETS - max_exact)).astype(jnp.int32)
    large = jnp.minimum(large, REL_BUCKETS - 1)
    return jnp.where(n < max_exact, n, large)


def lambda_init_fn(layer_idx):
    return 0.8 - 0.6 * math.exp(-0.3 * layer_idx)


def diff_attention_branch(x, w_in, lam_q1, lam_k1, lam_q2, lam_k2, subln_w, rel_bias, w_out, layer_idx):
    Bn, T, _ = x.shape
    lam_init = lambda_init_fn(layer_idx)
    h = x @ w_in
    q, k, v, z = jnp.split(h, [B_Q_W, 2 * B_Q_W, 2 * B_Q_W + B_V_W], axis=-1)
    q = q.reshape(Bn, T, B_HEADS, 2, B_DH).transpose(0, 2, 3, 1, 4)
    k = k.reshape(Bn, T, B_HEADS, 2, B_DH).transpose(0, 2, 3, 1, 4)
    v = v.reshape(Bn, T, B_HEADS, 2 * B_DH).transpose(0, 2, 1, 3)
    lam = (jnp.exp(jnp.sum(lam_q1.astype(jnp.float32) * lam_k1.astype(jnp.float32)))
           - jnp.exp(jnp.sum(lam_q2.astype(jnp.float32) * lam_k2.astype(jnp.float32))) + lam_init)
    scale = B_DH ** -0.5
    nb = T // Q_BLOCK
    q_blocks = jnp.moveaxis(q.reshape(Bn, B_HEADS, 2, nb, Q_BLOCK, B_DH), 3, 0)
    starts = jnp.arange(nb, dtype=jnp.int32) * Q_BLOCK
    kpos = jnp.arange(T, dtype=jnp.int32)

    def attend(args):
        qb, start = args
        s = jnp.einsum('bhpqd,bhpkd->bhpqk', qb, k).astype(jnp.float32) * scale
        qpos = start + jnp.arange(Q_BLOCK, dtype=jnp.int32)
        rel = qpos[:, None] - kpos[None, :]
        bias = rel_bias.astype(jnp.float32)[t5_causal_bucket(rel)]
        s = s + bias.transpose(2, 0, 1)[None, :, None]
        s = jnp.where((rel >= 0)[None, None, None], s, -jnp.inf)
        p = jax.nn.softmax(s, axis=-1)
        a = p[:, :, 0] - lam * p[:, :, 1]
        return jnp.einsum('bhqk,bhkd->bhqd', a.astype(v.dtype), v)

    o = lax.map(attend, (q_blocks, starts))
    o = o.transpose(1, 0, 3, 2, 4).reshape(Bn, T, B_HEADS, 2 * B_DH)
    gate = jax.nn.silu(z.astype(jnp.float32)).reshape(Bn, T, B_HEADS, 2 * B_DH)
    o = rms_norm(o, subln_w, 1e-5) * (1.0 - lam_init) * gate
    return o.reshape(Bn, T, B_V_W).astype(x.dtype) @ w_out


def setup_inputs(seed: int = 0) -> dict:
    key = jax.random.key(seed)
    ks = jax.random.split(key, 20)
    f32 = jnp.float32
    x = jax.random.normal(ks[0], (BATCH, SEQ, D_MODEL), f32)
    ln_g = 1.0 + 0.02 * jax.random.normal(ks[1], (DEPTH, D_MODEL), f32)
    ln_b = 0.02 * jax.random.normal(ks[2], (DEPTH, D_MODEL), f32)
    rel_bias = 0.5 * jax.random.normal(ks[3], (REL_BUCKETS, B_HEADS), f32)

    a_col_scale = jnp.ones((A_IN_W,), f32).at[2 * A_QK_W:A_CONV_CH].set(DEEPNORM_BETA)
    a_w_in = jax.random.normal(ks[4], (N_A_LAYERS, D_MODEL, A_IN_W), f32) * (D_MODEL ** -0.5) * a_col_scale
    a_conv_w = jax.random.normal(ks[5], (N_A_LAYERS, A_CONV_CH, A_CONV), f32) * (A_CONV ** -0.5)
    a_a_log = jnp.log(jax.random.uniform(ks[6], (N_A_LAYERS, A_HEADS), f32, 1.0, 16.0))
    dt = jnp.exp(jax.random.uniform(ks[7], (N_A_LAYERS, A_HEADS), f32, math.log(1e-3), math.log(1e-1)))
    a_dt_bias = dt + jnp.log(-jnp.expm1(-dt))
    a_norm_w = 1.0 + 0.02 * jax.random.normal(ks[8], (N_A_LAYERS, A_DV), f32)
    a_w_out = jax.random.normal(ks[9], (N_A_LAYERS, A_V_W, D_MODEL), f32) * (A_V_W ** -0.5) * DEEPNORM_BETA

    b_col_scale = jnp.ones((B_IN_W,), f32).at[2 * B_Q_W:2 * B_Q_W + B_V_W].set(DEEPNORM_BETA)
    b_w_in = jax.random.normal(ks[10], (N_B_LAYERS, D_MODEL, B_IN_W), f32) * (D_MODEL ** -0.5) * b_col_scale
    b_lam_q1 = 0.1 * jax.random.normal(ks[11], (N_B_LAYERS, B_DH), f32)
    b_lam_k1 = 0.1 * jax.random.normal(ks[12], (N_B_LAYERS, B_DH), f32)
    b_lam_q2 = 0.1 * jax.random.normal(ks[13], (N_B_LAYERS, B_DH), f32)
    b_lam_k2 = 0.1 * jax.random.normal(ks[14], (N_B_LAYERS, B_DH), f32)
    b_subln_w = 1.0 + 0.02 * jax.random.normal(ks[15], (N_B_LAYERS, 2 * B_DH), f32)
    b_w_out = jax.random.normal(ks[16], (N_B_LAYERS, B_V_W, D_MODEL), f32) * (B_V_W ** -0.5) * DEEPNORM_BETA
    return {"x": x, "ln_g": ln_g, "ln_b": ln_b, "rel_bias": rel_bias,
            "a_w_in": a_w_in, "a_conv_w": a_conv_w, "a_a_log": a_a_log, "a_dt_bias": a_dt_bias,
            "a_norm_w": a_norm_w, "a_w_out": a_w_out,
            "b_w_in": b_w_in, "b_lam_q1": b_lam_q1, "b_lam_k1": b_lam_k1, "b_lam_q2": b_lam_q2,
            "b_lam_k2": b_lam_k2, "b_subln_w": b_subln_w, "b_w_out": b_w_out}


def reference(x, ln_g, ln_b, rel_bias, a_w_in, a_conv_w, a_a_log, a_dt_bias, a_norm_w, a_w_out,
              b_w_in, b_lam_q1, b_lam_k1, b_lam_q2, b_lam_k2, b_subln_w, b_w_out):
    for i in range(DEPTH):
        j = i // N_MIXERS
        if i % N_MIXERS == 0:
            y = gated_deltanet_branch(x, a_w_in[j], a_conv_w[j], a_a_log[j], a_dt_bias[j],
                                      a_norm_w[j], a_w_out[j])
        else:
            y = diff_attention_branch(x, b_w_in[j], b_lam_q1[j], b_lam_k1[j], b_lam_q2[j], b_lam_k2[j],
                                      b_subln_w[j], rel_bias, b_w_out[j], i)
        x = layer_norm(DEEPNORM_ALPHA * x + y, ln_g[i], ln_b[i])
    return x
```

```python
import functools
import math

import numpy as np
import jax
import jax.numpy as jnp
from jax import lax
from jax.experimental import pallas as pl
from jax.experimental.pallas import tpu as pltpu

F32 = jnp.float32
BF16 = jnp.bfloat16

LANES = 128
SUBLANES = 8
VMEM_LIMIT_BYTES = 56 * 1024 * 1024

CONV_TAPS = 4
HEAD_DIM = 128
LN_EPS = 1e-5
RMS_EPS_A = 1e-6
RMS_EPS_B = 1e-5
L2_EPS = 1e-6
REL_BUCKETS = 32
REL_MAX_DIST = 128

MM_BM = 1024
MM_BN = 1024
GDN_CHUNK = 128
GDN_TB = 512
ATT_TQ = 512
LN_BM = 256


def _cparams(sem):
    return pltpu.CompilerParams(dimension_semantics=sem, vmem_limit_bytes=VMEM_LIMIT_BYTES)


def _silu(x):
    return x * (1.0 / (1.0 + jnp.exp(-x)))


def _mm_kernel(x_ref, w_ref, o_ref):
    o_ref[...] = jnp.dot(x_ref[...], w_ref[...], preferred_element_type=F32).astype(o_ref.dtype)


def _matmul(x, w, out_dtype):
    M, K = x.shape
    N = w.shape[1]
    bm, bn = min(MM_BM, M), min(MM_BN, N)
    return pl.pallas_call(
        _mm_kernel,
        grid=(M // bm, N // bn),
        in_specs=[pl.BlockSpec((bm, K), lambda i, j: (i, 0)),
                  pl.BlockSpec((K, bn), lambda i, j: (0, j))],
        out_specs=pl.BlockSpec((bm, bn), lambda i, j: (i, j)),
        out_shape=jax.ShapeDtypeStruct((M, N), out_dtype),
        compiler_params=_cparams(("parallel", "parallel")),
        name="in_proj",
    )(x, w)


def _mm_res_kernel(alpha, o_ref, w_ref, x_ref, r_ref):
    y = jnp.dot(o_ref[...], w_ref[...], preferred_element_type=F32)
    r_ref[...] = alpha * x_ref[...] + y


def _matmul_residual(o, w, x, alpha):
    M, K = o.shape
    N = w.shape[1]
    bm, bn = min(MM_BM, M), min(MM_BN, N)
    return pl.pallas_call(
        functools.partial(_mm_res_kernel, alpha),
        grid=(M // bm, N // bn),
        in_specs=[pl.BlockSpec((bm, K), lambda i, j: (i, 0)),
                  pl.BlockSpec((K, bn), lambda i, j: (0, j)),
                  pl.BlockSpec((bm, bn), lambda i, j: (i, j))],
        out_specs=pl.BlockSpec((bm, bn), lambda i, j: (i, j)),
        out_shape=jax.ShapeDtypeStruct((M, N), F32),
        compiler_params=_cparams(("parallel", "parallel")),
        name="out_proj_residual",
    )(o, w, x)


def _ln_kernel(r_ref, g_ref, b_ref, o_ref, ob_ref):
    r = r_ref[...]
    mu = jnp.mean(r, axis=-1, keepdims=True)
    d = r - mu
    var = jnp.mean(d * d, axis=-1, keepdims=True)
    y = d * lax.rsqrt(var + LN_EPS) * g_ref[...] + b_ref[...]
    o_ref[...] = y
    ob_ref[...] = y.astype(BF16)


def _layer_norm(r, g, b):
    M, D = r.shape
    bm = min(LN_BM, M)
    return pl.pallas_call(
        _ln_kernel,
        grid=(M // bm,),
        in_specs=[pl.BlockSpec((bm, D), lambda i: (i, 0)),
                  pl.BlockSpec((1, D), lambda i: (0, 0)),
                  pl.BlockSpec((1, D), lambda i: (0, 0))],
        out_specs=[pl.BlockSpec((bm, D), lambda i: (i, 0)),
                   pl.BlockSpec((bm, D), lambda i: (i, 0))],
        out_shape=[jax.ShapeDtypeStruct((M, D), F32), jax.ShapeDtypeStruct((M, D), BF16)],
        compiler_params=_cparams(("parallel",)),
        name="layer_norm",
    )(r, g.reshape(1, D), b.reshape(1, D))


def _gates_kernel(chunk, x_ref, w_ref, alog_ref, dtb_ref, gc_ref, beta_ref):
    ab = lax.dot_general(w_ref[...], x_ref[...], (((1,), (1,)), ((), ())),
                         preferred_element_type=F32)
    nh = gc_ref.shape[0]
    a = ab[:nh] + dtb_ref[...]
    b = ab[nh:]
    softplus = jnp.maximum(a, 0.0) + jnp.log1p(jnp.exp(-jnp.abs(a)))
    g = -jnp.exp(alog_ref[...]) * softplus
    pos = lax.broadcasted_iota(jnp.int32, g.shape, 1) % chunk
    s = 1
    while s < chunk:
        g = g + jnp.where(pos >= s, pltpu.roll(g, s, axis=1), 0.0)
        s *= 2
    gc_ref[...] = g
    beta_ref[...] = 1.0 / (1.0 + jnp.exp(-b))


def _gates(xb, w_ab_t, a_log, dt_bias, chunk):
    M, K = xb.shape
    nh = a_log.shape[0]
    bm = min(MM_BM, M)
    return pl.pallas_call(
        functools.partial(_gates_kernel, chunk),
        grid=(M // bm,),
        in_specs=[pl.BlockSpec((bm, K), lambda i: (i, 0)),
                  pl.BlockSpec((2 * nh, K), lambda i: (0, 0)),
                  pl.BlockSpec((nh, 1), lambda i: (0, 0)),
                  pl.BlockSpec((nh, 1), lambda i: (0, 0))],
        out_specs=[pl.BlockSpec((nh, bm), lambda i: (0, i)),
                   pl.BlockSpec((nh, bm), lambda i: (0, i))],
        out_shape=[jax.ShapeDtypeStruct((nh, M), F32), jax.ShapeDtypeStruct((nh, M), F32)],
        compiler_params=_cparams(("parallel",)),
        name="gdn_gates",
    )(xb, w_ab_t, a_log.reshape(nh, 1), dt_bias.reshape(nh, 1))


def _dot(a, b):
    return jnp.dot(a.astype(BF16), b.astype(BF16), preferred_element_type=F32)


def _dot_nt(a, b):
    return lax.dot_general(a.astype(BF16), b.astype(BF16), (((1,), (1,)), ((), ())),
                           preferred_element_type=F32)


def _gdn_kernel(chunk, q_ref, k_ref, v_ref, z_ref, wq_ref, wk_ref, wv_ref, gc_ref, beta_ref,
                nw_ref, o_ref, s_ref, hq_ref, hk_ref, hv_ref):
    tb = q_ref.shape[0]
    dk = q_ref.shape[1]
    halo = hq_ref.shape[0]

    @pl.when(pl.program_id(2) == 0)
    def _():
        s_ref[...] = jnp.zeros_like(s_ref)
        hq_ref[...] = jnp.zeros_like(hq_ref)
        hk_ref[...] = jnp.zeros_like(hk_ref)
        hv_ref[...] = jnp.zeros_like(hv_ref)

    def conv_silu(x_ref, w_ref, h_ref):
        x = x_ref[...].astype(F32)
        xs = jnp.concatenate([h_ref[...], x], axis=0)
        w = w_ref[...]
        y = x * w[CONV_TAPS - 1:CONV_TAPS]
        for j in range(CONV_TAPS - 1):
            off = halo - (CONV_TAPS - 1) + j
            y = y + xs[off:off + tb] * w[j:j + 1]
        h_ref[...] = x[tb - halo:]
        return _silu(y)

    q = conv_silu(q_ref, wq_ref, hq_ref)
    k = conv_silu(k_ref, wk_ref, hk_ref)
    v = conv_silu(v_ref, wv_ref, hv_ref)
    q = q * (lax.rsqrt(jnp.sum(q * q, axis=-1, keepdims=True) + L2_EPS) * (dk ** -0.5))
    k = k * lax.rsqrt(jnp.sum(k * k, axis=-1, keepdims=True) + L2_EPS)

    gc_row = jnp.broadcast_to(gc_ref[...], (LANES, tb))
    gc_col = gc_row.T
    beta_col = jnp.broadcast_to(beta_ref[...], (LANES, tb)).T
    eg_col = jnp.exp(gc_col)

    ri = lax.broadcasted_iota(jnp.int32, (chunk, chunk), 0)
    ci = lax.broadcasted_iota(jnp.int32, (chunk, chunk), 1)
    incl = ri >= ci
    strict = ri > ci
    eye = (ri == ci).astype(F32)
    off_masks = []
    b = 1
    while b < chunk:
        off_masks.append((ri // (2 * b) == ci // (2 * b)) & (ri // b != ci // b))
        b *= 2

    nchunks = tb // chunk
    pre = []
    for c in range(nchunks):
        sl = slice(c * chunk, (c + 1) * chunk)
        qc, kc, vc = q[sl], k[sl], v[sl]
        bc = beta_col[sl]
        gcc = gc_col[sl]
        kb = kc * bc
        diff = gcc[:, :chunk] - gc_row[:chunk, sl]
        decay = jnp.where(incl, jnp.exp(jnp.where(incl, diff, 0.0)), 0.0)
        qk = _dot_nt(jnp.concatenate([qc, kb], axis=0), kc)
        attn = qk[:chunk] * decay
        a = jnp.where(strict, qk[chunk:] * decay, 0.0)
        x = eye - jnp.where(off_masks[0], a, 0.0)
        for off_mask in off_masks[1:]:
            a_off = jnp.where(off_mask, a, 0.0)
            x = x - _dot(x, _dot(a_off, x))
        wu = _dot(x, jnp.concatenate([kb * eg_col[sl], vc * bc], axis=1))
        g_last = gcc[chunk - 1:chunk, :]
        pre.append((wu[:, :dk], wu[:, dk:], attn, qc * eg_col[sl],
                    kc * jnp.exp(g_last - gcc), jnp.exp(g_last)))

    s = s_ref[...]
    outs = []
    for c in range(nchunks):
        w_c, u_c, attn, qd, ks, cd = pre[c]
        ws = _dot(jnp.concatenate([w_c, qd], axis=0), s)
        v_new = u_c - ws[:chunk]
        outs.append(ws[chunk:] + _dot(attn, v_new))
        s = s * cd + _dot(ks.T, v_new)
    s_ref[...] = s

    o = jnp.concatenate(outs, axis=0)
    o = o * lax.rsqrt(jnp.mean(o * o, axis=-1, keepdims=True) + RMS_EPS_A) * nw_ref[...]
    o_ref[...] = (o * _silu(z_ref[...].astype(F32))).astype(o_ref.dtype)


def _gated_delta_net(h, conv_w_t, gc, beta, norm_w, batch, heads):
    M = h.shape[0]
    T = M // batch
    tb = min(GDN_TB, T)
    nt = T // tb
    dk = HEAD_DIM
    chunk = min(GDN_CHUNK, tb)

    def hspec(off):
        return pl.BlockSpec((tb, dk), lambda b, hh, t: (b * nt + t, hh + off))

    def wspec(off):
        return pl.BlockSpec((CONV_TAPS, dk), lambda b, hh, t: (0, hh + off))

    gspec = pl.BlockSpec((None, 1, tb), lambda b, hh, t: (hh, 0, b * nt + t))
    return pl.pallas_call(
        functools.partial(_gdn_kernel, chunk),
        grid=(batch, heads, nt),
        in_specs=[hspec(0), hspec(heads), hspec(2 * heads), hspec(3 * heads),
                  wspec(0), wspec(heads), wspec(2 * heads), gspec, gspec,
                  pl.BlockSpec((1, dk), lambda b, hh, t: (0, 0))],
        out_specs=pl.BlockSpec((tb, dk), lambda b, hh, t: (b * nt + t, hh)),
        out_shape=jax.ShapeDtypeStruct((M, heads * dk), BF16),
        scratch_shapes=[pltpu.VMEM((dk, dk), F32),
                        pltpu.VMEM((SUBLANES, dk), F32),
                        pltpu.VMEM((SUBLANES, dk), F32),
                        pltpu.VMEM((SUBLANES, dk), F32)],
        compiler_params=_cparams(("parallel", "parallel", "arbitrary")),
        name="gated_delta_rule",
    )(h, h, h, h, conv_w_t, conv_w_t, conv_w_t,
      gc.reshape(heads, 1, M), beta.reshape(heads, 1, M), norm_w.reshape(1, dk))


def _t5_bucket_table(n):
    rel = np.arange(n)
    max_exact = REL_BUCKETS // 2
    nf = np.maximum(rel, 1).astype(np.float32)
    large = max_exact + (np.log(nf / np.float32(max_exact)) / np.float32(math.log(REL_MAX_DIST / max_exact))
                         * np.float32(REL_BUCKETS - max_exact)).astype(np.int32)
    large = np.minimum(large, REL_BUCKETS - 1)
    return np.where(rel < max_exact, rel, large)


def _attn_kernel(lam_init, q1_ref, q2_ref, k1_ref, k2_ref, v_ref, z_ref, brow_ref, lam_ref, sw_ref,
                 o_ref, m1_ref, l1_ref, a1_ref, m2_ref, l2_ref, a2_ref, bd_ref, bs_ref):
    tq = q1_ref.shape[0]
    dh = q1_ref.shape[1]
    qi = pl.program_id(2)
    scale = dh ** -0.5

    @pl.when(qi == 0)
    def _():
        r = pltpu.roll(jnp.broadcast_to(brow_ref[...], (tq, tq)), 0, axis=1, stride=1, stride_axis=0)
        ri = lax.broadcasted_iota(jnp.int32, (tq, tq), 0)
        ci = lax.broadcasted_iota(jnp.int32, (tq, tq), 1)
        bd_ref[...] = jnp.where(ri >= ci, r, -jnp.inf)
        bs_ref[...] = jnp.where(ri < ci, r, 0.0)

    m1_ref[...] = jnp.full_like(m1_ref, -jnp.inf)
    m2_ref[...] = jnp.full_like(m2_ref, -jnp.inf)
    l1_ref[...] = jnp.zeros_like(l1_ref)
    l2_ref[...] = jnp.zeros_like(l2_ref)
    a1_ref[...] = jnp.zeros_like(a1_ref)
    a2_ref[...] = jnp.zeros_like(a2_ref)

    q1 = q1_ref[...]
    q2 = q2_ref[...]

    def update(q, k_ref, m_ref, l_ref, a_ref, kj, vblk, bias):
        start = pl.multiple_of(kj * tq, tq)
        kblk = k_ref[pl.ds(start, tq), :]
        s = lax.dot_general(q, kblk, (((1,), (1,)), ((), ())), preferred_element_type=F32) * scale
        if bias is not None:
            s = s + bias
        m_old = m_ref[...]
        m_new = jnp.maximum(m_old, jnp.max(s, axis=-1, keepdims=True))
        p = jnp.exp(s - m_new)
        alpha = jnp.exp(m_old - m_new)
        l_ref[...] = alpha * l_ref[...] + jnp.sum(p, axis=-1, keepdims=True)
        a_ref[...] = alpha * a_ref[...] + jnp.dot(p.astype(BF16), vblk, preferred_element_type=F32)
        m_ref[...] = m_new

    def step(kj, bias):
        start = pl.multiple_of(kj * tq, tq)
        vblk = v_ref[pl.ds(start, tq), :]
        update(q1, k1_ref, m1_ref, l1_ref, a1_ref, kj, vblk, bias)
        update(q2, k2_ref, m2_ref, l2_ref, a2_ref, kj, vblk, bias)

    def far_body(kj, carry):
        step(kj, None)
        return carry

    lax.fori_loop(0, jnp.maximum(qi - 1, 0), far_body, 0)

    @pl.when(qi >= 1)
    def _():
        step(qi - 1, bs_ref[...])

    step(qi, bd_ref[...])

    lv = lam_ref[...]
    lam = (jnp.exp(jnp.sum(lv[0:1] * lv[1:2], axis=-1, keepdims=True))
           - jnp.exp(jnp.sum(lv[2:3] * lv[3:4], axis=-1, keepdims=True)) + lam_init)
    o = a1_ref[...] / l1_ref[...] - lam * (a2_ref[...] / l2_ref[...])
    o = o * lax.rsqrt(jnp.mean(o * o, axis=-1, keepdims=True) + RMS_EPS_B) * sw_ref[...]
    o = o * (1.0 - lam_init) * _silu(z_ref[...].astype(F32))
    o_ref[...] = o.astype(o_ref.dtype)


def _diff_attention(h, brow, lam_vecs, subln_w, batch, heads, lam_init):
    M = h.shape[0]
    T = M // batch
    tq = min(ATT_TQ, T)
    nq = T // tq
    dh = HEAD_DIM
    dv = 2 * dh
    return pl.pallas_call(
        functools.partial(_attn_kernel, lam_init),
        grid=(batch, heads, nq),
        in_specs=[pl.BlockSpec((tq, dh), lambda b, hh, i: (b * nq + i, 2 * hh)),
                  pl.BlockSpec((tq, dh), lambda b, hh, i: (b * nq + i, 2 * hh + 1)),
                  pl.BlockSpec((T, dh), lambda b, hh, i: (b, 2 * heads + 2 * hh)),
                  pl.BlockSpec((T, dh), lambda b, hh, i: (b, 2 * heads + 2 * hh + 1)),
                  pl.BlockSpec((T, dv), lambda b, hh, i: (b, 2 * heads + hh)),
                  pl.BlockSpec((tq, dv), lambda b, hh, i: (b * nq + i, 3 * heads + hh)),
                  pl.BlockSpec((None, 1, tq), lambda b, hh, i: (hh, 0, 0)),
                  pl.BlockSpec((4, dh), lambda b, hh, i: (0, 0)),
                  pl.BlockSpec((1, dv), lambda b, hh, i: (0, 0))],
        out_specs=pl.BlockSpec((tq, dv), lambda b, hh, i: (b * nq + i, hh)),
        out_shape=jax.ShapeDtypeStruct((M, heads * dv), BF16),
        scratch_shapes=[pltpu.VMEM((tq, 1), F32), pltpu.VMEM((tq, 1), F32), pltpu.VMEM((tq, dv), F32),
                        pltpu.VMEM((tq, 1), F32), pltpu.VMEM((tq, 1), F32), pltpu.VMEM((tq, dv), F32),
                        pltpu.VMEM((tq, tq), F32), pltpu.VMEM((tq, tq), F32)],
        compiler_params=_cparams(("parallel", "parallel", "arbitrary")),
        name="diff_attention",
    )(h, h, h, h, h, h, brow, lam_vecs, subln_w.reshape(1, dv))


def kernel(x, ln_g, ln_b, rel_bias, a_w_in, a_conv_w, a_a_log, a_dt_bias, a_norm_w, a_w_out,
           b_w_in, b_lam_q1, b_lam_k1, b_lam_q2, b_lam_k2, b_subln_w, b_w_out):
    batch, T, D = x.shape
    M = batch * T
    depth = ln_g.shape[0]
    a_heads = a_a_log.shape[1]
    b_heads = rel_bias.shape[1]
    alpha = (2.0 * depth) ** 0.25
    assert T % min(GDN_TB, T) == 0 and T % min(ATT_TQ, T) == 0 and min(ATT_TQ, T) >= LANES

    xf = x.reshape(M, D)
    xb = xf.astype(BF16)
    for i in range(depth):
        j = i // 2
        if i % 2 == 0:
            qkvz_w = 4 * a_heads * HEAD_DIM
            w_in = a_w_in[j]
            h = _matmul(xb, w_in[:, :qkvz_w].astype(BF16), BF16)
            chunk = min(GDN_CHUNK, GDN_TB, T)
            gc, beta = _gates(xb, w_in[:, qkvz_w:].T.astype(BF16), a_a_log[j], a_dt_bias[j], chunk)
            o = _gated_delta_net(h, a_conv_w[j].T, gc, beta, a_norm_w[j], batch, a_heads)
            w_out = a_w_out[j]
        else:
            tq = min(ATT_TQ, T)
            h = _matmul(xb, b_w_in[j].astype(BF16), BF16)
            table = rel_bias[_t5_bucket_table(tq)] - rel_bias[REL_BUCKETS - 1][None, :]
            brow = jnp.roll(table[::-1], 1, axis=0).T.reshape(b_heads, 1, tq)
            lam_vecs = jnp.stack([b_lam_q1[j], b_lam_k1[j], b_lam_q2[j], b_lam_k2[j]])
            lam_init = 0.8 - 0.6 * math.exp(-0.3 * i)
            o = _diff_attention(h, brow, lam_vecs, b_subln_w[j], batch, b_heads, lam_init)
            w_out = b_w_out[j]
        r = _matmul_residual(o, w_out.astype(BF16), xf, alpha)
        xf, xb = _layer_norm(r, ln_g[i], ln_b[i])
    return xf.reshape(batch, T, D)
```

```python
import functools
import math

import numpy as np
import jax
import jax.numpy as jnp
from jax import lax
from jax.experimental import pallas as pl
from jax.experimental.pallas import tpu as pltpu

F32 = jnp.float32
BF16 = jnp.bfloat16

LANES = 128
SUBLANES = 8
VMEM_LIMIT_BYTES = 56 * 1024 * 1024

CONV_TAPS = 4
HEAD_DIM = 128
LN_EPS = 1e-5
RMS_EPS_A = 1e-6
RMS_EPS_B = 1e-5
L2_EPS = 1e-6
REL_BUCKETS = 32
REL_MAX_DIST = 128

MM_BM = 1024
MM_BN = 1024
GDN_CHUNK = 128
GDN_TB = 512
GDN_HEADS_PER_STEP = 4
ATT_TQ = 512
LN_BM = 256


def _cparams(sem):
    return pltpu.CompilerParams(dimension_semantics=sem, vmem_limit_bytes=VMEM_LIMIT_BYTES)


def _silu(x):
    return x * (1.0 / (1.0 + jnp.exp(-x)))


def _mm_kernel(x_ref, w_ref, o_ref):
    o_ref[...] = jnp.dot(x_ref[...], w_ref[...], preferred_element_type=F32).astype(o_ref.dtype)


def _matmul(x, w, out_dtype):
    M, K = x.shape
    N = w.shape[1]
    bm, bn = min(MM_BM, M), min(MM_BN, N)
    return pl.pallas_call(
        _mm_kernel,
        grid=(M // bm, N // bn),
        in_specs=[pl.BlockSpec((bm, K), lambda i, j: (i, 0)),
                  pl.BlockSpec((K, bn), lambda i, j: (0, j))],
        out_specs=pl.BlockSpec((bm, bn), lambda i, j: (i, j)),
        out_shape=jax.ShapeDtypeStruct((M, N), out_dtype),
        compiler_params=_cparams(("parallel", "parallel")),
        name="in_proj",
    )(x, w)


def _mm_res_kernel(alpha, o_ref, w_ref, x_ref, r_ref):
    y = jnp.dot(o_ref[...], w_ref[...], preferred_element_type=F32)
    r_ref[...] = alpha * x_ref[...] + y


def _matmul_residual(o, w, x, alpha):
    M, K = o.shape
    N = w.shape[1]
    bm, bn = min(MM_BM, M), min(MM_BN, N)
    return pl.pallas_call(
        functools.partial(_mm_res_kernel, alpha),
        grid=(M // bm, N // bn),
        in_specs=[pl.BlockSpec((bm, K), lambda i, j: (i, 0)),
                  pl.BlockSpec((K, bn), lambda i, j: (0, j)),
                  pl.BlockSpec((bm, bn), lambda i, j: (i, j))],
        out_specs=pl.BlockSpec((bm, bn), lambda i, j: (i, j)),
        out_shape=jax.ShapeDtypeStruct((M, N), F32),
        compiler_params=_cparams(("parallel", "parallel")),
        name="out_proj_residual",
    )(o, w, x)


def _ln_kernel(r_ref, g_ref, b_ref, o_ref, ob_ref):
    r = r_ref[...]
    mu = jnp.mean(r, axis=-1, keepdims=True)
    d = r - mu
    var = jnp.mean(d * d, axis=-1, keepdims=True)
    y = d * lax.rsqrt(var + LN_EPS) * g_ref[...] + b_ref[...]
    o_ref[...] = y
    ob_ref[...] = y.astype(BF16)


def _layer_norm(r, g, b):
    M, D = r.shape
    bm = min(LN_BM, M)
    return pl.pallas_call(
        _ln_kernel,
        grid=(M // bm,),
        in_specs=[pl.BlockSpec((bm, D), lambda i: (i, 0)),
                  pl.BlockSpec((1, D), lambda i: (0, 0)),
                  pl.BlockSpec((1, D), lambda i: (0, 0))],
        out_specs=[pl.BlockSpec((bm, D), lambda i: (i, 0)),
                   pl.BlockSpec((bm, D), lambda i: (i, 0))],
        out_shape=[jax.ShapeDtypeStruct((M, D), F32), jax.ShapeDtypeStruct((M, D), BF16)],
        compiler_params=_cparams(("parallel",)),
        name="layer_norm",
    )(r, g.reshape(1, D), b.reshape(1, D))


def _gates_kernel(chunk, x_ref, w_ref, alog_ref, dtb_ref, gc_ref, beta_ref):
    ab = lax.dot_general(w_ref[...], x_ref[...], (((1,), (1,)), ((), ())),
                         preferred_element_type=F32)
    nh = gc_ref.shape[0]
    a = ab[:nh] + dtb_ref[...]
    b = ab[nh:]
    softplus = jnp.maximum(a, 0.0) + jnp.log1p(jnp.exp(-jnp.abs(a)))
    g = -jnp.exp(alog_ref[...]) * softplus
    pos = lax.broadcasted_iota(jnp.int32, g.shape, 1) % chunk
    s = 1
    while s < chunk:
        g = g + jnp.where(pos >= s, pltpu.roll(g, s, axis=1), 0.0)
        s *= 2
    gc_ref[...] = g
    beta_ref[...] = 1.0 / (1.0 + jnp.exp(-b))


def _gates(xb, w_ab_t, a_log, dt_bias, chunk):
    M, K = xb.shape
    nh = a_log.shape[0]
    bm = min(MM_BM, M)
    return pl.pallas_call(
        functools.partial(_gates_kernel, chunk),
        grid=(M // bm,),
        in_specs=[pl.BlockSpec((bm, K), lambda i: (i, 0)),
                  pl.BlockSpec((2 * nh, K), lambda i: (0, 0)),
                  pl.BlockSpec((nh, 1), lambda i: (0, 0)),
                  pl.BlockSpec((nh, 1), lambda i: (0, 0))],
        out_specs=[pl.BlockSpec((nh, bm), lambda i: (0, i)),
                   pl.BlockSpec((nh, bm), lambda i: (0, i))],
        out_shape=[jax.ShapeDtypeStruct((nh, M), F32), jax.ShapeDtypeStruct((nh, M), F32)],
        compiler_params=_cparams(("parallel",)),
        name="gdn_gates",
    )(xb, w_ab_t, a_log.reshape(nh, 1), dt_bias.reshape(nh, 1))


def _dot(a, b):
    return jnp.dot(a.astype(BF16), b.astype(BF16), preferred_element_type=F32)


def _dot_nt(a, b):
    return lax.dot_general(a.astype(BF16), b.astype(BF16), (((1,), (1,)), ((), ())),
                           preferred_element_type=F32)


def _gdn_kernel(chunk, q_ref, k_ref, v_ref, z_ref, wq_ref, wk_ref, wv_ref, gc_ref, beta_ref,
                nw_ref, o_ref, s_ref, hq_ref, hk_ref, hv_ref):
    tb = q_ref.shape[0]
    dk = HEAD_DIM
    nheads = q_ref.shape[1] // dk
    halo = hq_ref.shape[0]
    nchunks = tb // chunk

    @pl.when(pl.program_id(2) == 0)
    def _():
        s_ref[...] = jnp.zeros_like(s_ref)
        hq_ref[...] = jnp.zeros_like(hq_ref)
        hk_ref[...] = jnp.zeros_like(hk_ref)
        hv_ref[...] = jnp.zeros_like(hv_ref)

    def conv_silu(x_ref, w_ref, h_ref):
        x = x_ref[...].astype(F32)
        xs = jnp.concatenate([h_ref[...], x], axis=0)
        w = w_ref[...]
        y = x * w[CONV_TAPS - 1:CONV_TAPS]
        for j in range(CONV_TAPS - 1):
            off = halo - (CONV_TAPS - 1) + j
            y = y + xs[off:off + tb] * w[j:j + 1]
        h_ref[...] = x[tb - halo:]
        return _silu(y)

    q_all = conv_silu(q_ref, wq_ref, hq_ref)
    k_all = conv_silu(k_ref, wk_ref, hk_ref)
    v_all = conv_silu(v_ref, wv_ref, hv_ref)

    ri = lax.broadcasted_iota(jnp.int32, (chunk, chunk), 0)
    ci = lax.broadcasted_iota(jnp.int32, (chunk, chunk), 1)
    incl = ri >= ci
    strict = ri > ci
    eye = (ri == ci).astype(F32)
    off_masks = []
    b = 1
    while b < chunk:
        off_masks.append((ri // (2 * b) == ci // (2 * b)) & (ri // b != ci // b))
        b *= 2

    chains = [(g, c) for g in range(nheads) for c in range(nchunks)]
    a_mat, attn, rhs, lhs_top, qd, cdec = {}, {}, {}, {}, {}, {}
    for g in range(nheads):
        hs = slice(g * dk, (g + 1) * dk)
        q = q_all[:, hs]
        k = k_all[:, hs]
        v = v_all[:, hs]
        q = q * (lax.rsqrt(jnp.sum(q * q, axis=-1, keepdims=True) + L2_EPS) * (dk ** -0.5))
        k = k * lax.rsqrt(jnp.sum(k * k, axis=-1, keepdims=True) + L2_EPS)
        gc_row = jnp.broadcast_to(gc_ref[g], (LANES, tb))
        gc_col = gc_row.T
        beta_col = jnp.broadcast_to(beta_ref[g], (LANES, tb)).T
        eg_col = jnp.exp(gc_col)
        for c in range(nchunks):
            sl = slice(c * chunk, (c + 1) * chunk)
            qc, kc, vc = q[sl], k[sl], v[sl]
            bc = beta_col[sl]
            gcc = gc_col[sl]
            kb = kc * bc
            diff = gcc[:, :chunk] - gc_row[:chunk, sl]
            decay = jnp.where(incl, jnp.exp(jnp.where(incl, diff, 0.0)), 0.0)
            qk = _dot_nt(jnp.concatenate([qc, kb], axis=0), kc)
            attn[g, c] = qk[:chunk] * decay
            a_mat[g, c] = jnp.where(strict, qk[chunk:] * decay, 0.0)
            rhs[g, c] = jnp.concatenate([kb * eg_col[sl], vc * bc], axis=1)
            g_last = gcc[chunk - 1:chunk, :]
            lhs_top[g, c] = (kc * jnp.exp(g_last - gcc)).T
            qd[g, c] = qc * eg_col[sl]
            cdec[g, c] = jnp.exp(g_last)

    x = {ch: eye - jnp.where(off_masks[0], a_mat[ch], 0.0) for ch in chains}
    for off_mask in off_masks[1:]:
        y = {ch: _dot(jnp.where(off_mask, a_mat[ch], 0.0), x[ch]) for ch in chains}
        x = {ch: x[ch] - _dot(x[ch], y[ch]) for ch in chains}
    wu = {ch: _dot(x[ch], rhs[ch]) for ch in chains}
    st = {ch: _dot(jnp.concatenate([lhs_top[ch], attn[ch]], axis=0), wu[ch]) for ch in chains}

    s = [s_ref[g] for g in range(nheads)]
    outs = {}
    for c in range(nchunks):
        for g in range(nheads):
            t = st[g, c]
            lhs = jnp.concatenate([-t[:dk, :dk], qd[g, c] - t[dk:, :dk]], axis=0)
            r = _dot(lhs, s[g])
            outs[g, c] = r[dk:] + t[dk:, dk:]
            s[g] = s[g] * cdec[g, c] + r[:dk] + t[:dk, dk:]
    for g in range(nheads):
        s_ref[g] = s[g]

    nw = nw_ref[...]
    cols = []
    for g in range(nheads):
        og = jnp.concatenate([outs[g, c] for c in range(nchunks)], axis=0)
        cols.append(og * lax.rsqrt(jnp.mean(og * og, axis=-1, keepdims=True) + RMS_EPS_A) * nw)
    o = jnp.concatenate(cols, axis=1)
    o_ref[...] = (o * _silu(z_ref[...].astype(F32))).astype(o_ref.dtype)


def _gated_delta_net(h, conv_w_t, gc, beta, norm_w, batch, heads):
    M = h.shape[0]
    T = M // batch
    tb = min(GDN_TB, T)
    nt = T // tb
    dk = HEAD_DIM
    chunk = min(GDN_CHUNK, tb)
    hps = min(GDN_HEADS_PER_STEP, heads)
    ng = heads // hps
    width = hps * dk

    def hspec(off):
        return pl.BlockSpec((tb, width), lambda b, hh, t: (b * nt + t, hh + off))

    def wspec(off):
        return pl.BlockSpec((CONV_TAPS, width), lambda b, hh, t: (0, hh + off))

    gspec = pl.BlockSpec((hps, 1, tb), lambda b, hh, t: (hh, 0, b * nt + t))
    return pl.pallas_call(
        functools.partial(_gdn_kernel, chunk),
        grid=(batch, ng, nt),
        in_specs=[hspec(0), hspec(ng), hspec(2 * ng), hspec(3 * ng),
                  wspec(0), wspec(ng), wspec(2 * ng), gspec, gspec,
                  pl.BlockSpec((1, dk), lambda b, hh, t: (0, 0))],
        out_specs=pl.BlockSpec((tb, width), lambda b, hh, t: (b * nt + t, hh)),
        out_shape=jax.ShapeDtypeStruct((M, heads * dk), BF16),
        scratch_shapes=[pltpu.VMEM((hps, dk, dk), F32),
                        pltpu.VMEM((SUBLANES, width), F32),
                        pltpu.VMEM((SUBLANES, width), F32),
                        pltpu.VMEM((SUBLANES, width), F32)],
        compiler_params=_cparams(("parallel", "parallel", "arbitrary")),
        name="gated_delta_rule",
    )(h, h, h, h, conv_w_t, conv_w_t, conv_w_t,
      gc.reshape(heads, 1, M), beta.reshape(heads, 1, M), norm_w.reshape(1, dk))


def _t5_bucket_table(n):
    rel = np.arange(n)
    max_exact = REL_BUCKETS // 2
    nf = np.maximum(rel, 1).astype(np.float32)
    large = max_exact + (np.log(nf / np.float32(max_exact)) / np.float32(math.log(REL_MAX_DIST / max_exact))
                         * np.float32(REL_BUCKETS - max_exact)).astype(np.int32)
    large = np.minimum(large, REL_BUCKETS - 1)
    return np.where(rel < max_exact, rel, large)


def _attn_kernel(lam_init, q1_ref, q2_ref, k1_ref, k2_ref, v_ref, z_ref, brow_ref, lam_ref, sw_ref,
                 o_ref, m1_ref, l1_ref, a1_ref, m2_ref, l2_ref, a2_ref, bd_ref, bs_ref):
    tq = q1_ref.shape[0]
    dh = q1_ref.shape[1]
    qi = pl.program_id(2)
    scale = dh ** -0.5

    @pl.when(qi == 0)
    def _():
        r = pltpu.roll(jnp.broadcast_to(brow_ref[...], (tq, tq)), 0, axis=1, stride=1, stride_axis=0)
        ri = lax.broadcasted_iota(jnp.int32, (tq, tq), 0)
        ci = lax.broadcasted_iota(jnp.int32, (tq, tq), 1)
        bd_ref[...] = jnp.where(ri >= ci, r, -jnp.inf)
        bs_ref[...] = jnp.where(ri < ci, r, 0.0)

    m1_ref[...] = jnp.full_like(m1_ref, -jnp.inf)
    m2_ref[...] = jnp.full_like(m2_ref, -jnp.inf)
    l1_ref[...] = jnp.zeros_like(l1_ref)
    l2_ref[...] = jnp.zeros_like(l2_ref)
    a1_ref[...] = jnp.zeros_like(a1_ref)
    a2_ref[...] = jnp.zeros_like(a2_ref)

    q1 = q1_ref[...]
    q2 = q2_ref[...]

    def update(q, k_ref, m_ref, l_ref, a_ref, kj, vblk, bias):
        start = pl.multiple_of(kj * tq, tq)
        kblk = k_ref[pl.ds(start, tq), :]
        s = lax.dot_general(q, kblk, (((1,), (1,)), ((), ())), preferred_element_type=F32) * scale
        if bias is not None:
            s = s + bias
        m_old = m_ref[...]
        m_new = jnp.maximum(m_old, jnp.max(s, axis=-1, keepdims=True))
        p = jnp.exp(s - m_new)
        alpha = jnp.exp(m_old - m_new)
        l_ref[...] = alpha * l_ref[...] + jnp.sum(p, axis=-1, keepdims=True)
        a_ref[...] = alpha * a_ref[...] + jnp.dot(p.astype(BF16), vblk, preferred_element_type=F32)
        m_ref[...] = m_new

    def step(kj, bias):
        start = pl.multiple_of(kj * tq, tq)
        vblk = v_ref[pl.ds(start, tq), :]
        update(q1, k1_ref, m1_ref, l1_ref, a1_ref, kj, vblk, bias)
        update(q2, k2_ref, m2_ref, l2_ref, a2_ref, kj, vblk, bias)

    def far_body(kj, carry):
        step(kj, None)
        return carry

    lax.fori_loop(0, jnp.maximum(qi - 1, 0), far_body, 0)

    @pl.when(qi >= 1)
    def _():
        step(qi - 1, bs_ref[...])

    step(qi, bd_ref[...])

    lv = lam_ref[...]
    lam = (jnp.exp(jnp.sum(lv[0:1] * lv[1:2], axis=-1, keepdims=True))
           - jnp.exp(jnp.sum(lv[2:3] * lv[3:4], axis=-1, keepdims=True)) + lam_init)
    o = a1_ref[...] / l1_ref[...] - lam * (a2_ref[...] / l2_ref[...])
    o = o * lax.rsqrt(jnp.mean(o * o, axis=-1, keepdims=True) + RMS_EPS_B) * sw_ref[...]
    o = o * (1.0 - lam_init) * _silu(z_ref[...].astype(F32))
    o_ref[...] = o.astype(o_ref.dtype)


def _diff_attention(h, brow, lam_vecs, subln_w, batch, heads, lam_init):
    M = h.shape[0]
    T = M // batch
    tq = min(ATT_TQ, T)
    nq = T // tq
    dh = HEAD_DIM
    dv = 2 * dh
    return pl.pallas_call(
        functools.partial(_attn_kernel, lam_init),
        grid=(batch, heads, nq),
        in_specs=[pl.BlockSpec((tq, dh), lambda b, hh, i: (b * nq + i, 2 * hh)),
                  pl.BlockSpec((tq, dh), lambda b, hh, i: (b * nq + i, 2 * hh + 1)),
                  pl.BlockSpec((T, dh), lambda b, hh, i: (b, 2 * heads + 2 * hh)),
                  pl.BlockSpec((T, dh), lambda b, hh, i: (b, 2 * heads + 2 * hh + 1)),
                  pl.BlockSpec((T, dv), lambda b, hh, i: (b, 2 * heads + hh)),
                  pl.BlockSpec((tq, dv), lambda b, hh, i: (b * nq + i, 3 * heads + hh)),
                  pl.BlockSpec((None, 1, tq), lambda b, hh, i: (hh, 0, 0)),
                  pl.BlockSpec((4, dh), lambda b, hh, i: (0, 0)),
                  pl.BlockSpec((1, dv), lambda b, hh, i: (0, 0))],
        out_specs=pl.BlockSpec((tq, dv), lambda b, hh, i: (b * nq + i, hh)),
        out_shape=jax.ShapeDtypeStruct((M, heads * dv), BF16),
        scratch_shapes=[pltpu.VMEM((tq, 1), F32), pltpu.VMEM((tq, 1), F32), pltpu.VMEM((tq, dv), F32),
                        pltpu.VMEM((tq, 1), F32), pltpu.VMEM((tq, 1), F32), pltpu.VMEM((tq, dv), F32),
                        pltpu.VMEM((tq, tq), F32), pltpu.VMEM((tq, tq), F32)],
        compiler_params=_cparams(("parallel", "parallel", "arbitrary")),
        name="diff_attention",
    )(h, h, h, h, h, h, brow, lam_vecs, subln_w.reshape(1, dv))


def kernel(x, ln_g, ln_b, rel_bias, a_w_in, a_conv_w, a_a_log, a_dt_bias, a_norm_w, a_w_out,
           b_w_in, b_lam_q1, b_lam_k1, b_lam_q2, b_lam_k2, b_subln_w, b_w_out):
    batch, T, D = x.shape
    M = batch * T
    depth = ln_g.shape[0]
    a_heads = a_a_log.shape[1]
    b_heads = rel_bias.shape[1]
    alpha = (2.0 * depth) ** 0.25
    assert T % min(GDN_TB, T) == 0 and T % min(ATT_TQ, T) == 0 and min(ATT_TQ, T) >= LANES

    xf = x.reshape(M, D)
    xb = xf.astype(BF16)
    for i in range(depth):
        j = i // 2
        if i % 2 == 0:
            qkvz_w = 4 * a_heads * HEAD_DIM
            w_in = a_w_in[j]
            h = _matmul(xb, w_in[:, :qkvz_w].astype(BF16), BF16)
            chunk = min(GDN_CHUNK, GDN_TB, T)
            gc, beta = _gates(xb, w_in[:, qkvz_w:].T.astype(BF16), a_a_log[j], a_dt_bias[j], chunk)
            o = _gated_delta_net(h, a_conv_w[j].T, gc, beta, a_norm_w[j], batch, a_heads)
            w_out = a_w_out[j]
        else:
            tq = min(ATT_TQ, T)
            h = _matmul(xb, b_w_in[j].astype(BF16), BF16)
            table = rel_bias[_t5_bucket_table(tq)] - rel_bias[REL_BUCKETS - 1][None, :]
            brow = jnp.roll(table[::-1], 1, axis=0).T.reshape(b_heads, 1, tq)
            lam_vecs = jnp.stack([b_lam_q1[j], b_lam_k1[j], b_lam_q2[j], b_lam_k2[j]])
            lam_init = 0.8 - 0.6 * math.exp(-0.3 * i)
            o = _diff_attention(h, brow, lam_vecs, b_subln_w[j], batch, b_heads, lam_init)
            w_out = b_w_out[j]
        r = _matmul_residual(o, w_out.astype(BF16), xf, alpha)
        xf, xb = _layer_norm(r, ln_g[i], ln_b[i])
    return xf.reshape(batch, T, D)
```

```python
import functools
import math

import numpy as np
import jax
import jax.numpy as jnp
from jax import lax
from jax.experimental import pallas as pl
from jax.experimental.pallas import tpu as pltpu

F32 = jnp.float32
BF16 = jnp.bfloat16

LANES = 128
SUBLANES = 8
VMEM_LIMIT_BYTES = 56 * 1024 * 1024

CONV_TAPS = 4
HEAD_DIM = 128
LN_EPS = 1e-5
RMS_EPS_A = 1e-6
RMS_EPS_B = 1e-5
L2_EPS = 1e-6
REL_BUCKETS = 32
REL_MAX_DIST = 128
LOG2_E = math.log2(math.e)

MM_BM = 1024
MM_BN = 1024
GDN_CHUNK = 128
GDN_TB = 512
GDN_HEADS_PER_STEP = 4
ATT_TQ = 512
LN_BM = 256


def _cparams(sem):
    return pltpu.CompilerParams(dimension_semantics=sem, vmem_limit_bytes=VMEM_LIMIT_BYTES)


def _silu(x):
    return x * (1.0 / (1.0 + jnp.exp(-x)))


def _mm_kernel(x_ref, w_ref, o_ref):
    o_ref[...] = jnp.dot(x_ref[...], w_ref[...], preferred_element_type=F32).astype(o_ref.dtype)


def _matmul(x, w, out_dtype):
    M, K = x.shape
    N = w.shape[1]
    bm, bn = min(MM_BM, M), min(MM_BN, N)
    return pl.pallas_call(
        _mm_kernel,
        grid=(M // bm, N // bn),
        in_specs=[pl.BlockSpec((bm, K), lambda i, j: (i, 0)),
                  pl.BlockSpec((K, bn), lambda i, j: (0, j))],
        out_specs=pl.BlockSpec((bm, bn), lambda i, j: (i, j)),
        out_shape=jax.ShapeDtypeStruct((M, N), out_dtype),
        compiler_params=_cparams(("parallel", "parallel")),
        name="in_proj",
    )(x, w)


def _mm_res_kernel(alpha, o_ref, w_ref, x_ref, r_ref):
    y = jnp.dot(o_ref[...], w_ref[...], preferred_element_type=F32)
    r_ref[...] = alpha * x_ref[...] + y


def _matmul_residual(o, w, x, alpha):
    M, K = o.shape
    N = w.shape[1]
    bm, bn = min(MM_BM, M), min(MM_BN, N)
    return pl.pallas_call(
        functools.partial(_mm_res_kernel, alpha),
        grid=(M // bm, N // bn),
        in_specs=[pl.BlockSpec((bm, K), lambda i, j: (i, 0)),
                  pl.BlockSpec((K, bn), lambda i, j: (0, j)),
                  pl.BlockSpec((bm, bn), lambda i, j: (i, j))],
        out_specs=pl.BlockSpec((bm, bn), lambda i, j: (i, j)),
        out_shape=jax.ShapeDtypeStruct((M, N), F32),
        compiler_params=_cparams(("parallel", "parallel")),
        name="out_proj_residual",
    )(o, w, x)


def _ln_kernel(r_ref, g_ref, b_ref, o_ref, ob_ref):
    r = r_ref[...]
    mu = jnp.mean(r, axis=-1, keepdims=True)
    d = r - mu
    var = jnp.mean(d * d, axis=-1, keepdims=True)
    y = d * lax.rsqrt(var + LN_EPS) * g_ref[...] + b_ref[...]
    o_ref[...] = y
    ob_ref[...] = y.astype(BF16)


def _layer_norm(r, g, b):
    M, D = r.shape
    bm = min(LN_BM, M)
    return pl.pallas_call(
        _ln_kernel,
        grid=(M // bm,),
        in_specs=[pl.BlockSpec((bm, D), lambda i: (i, 0)),
                  pl.BlockSpec((1, D), lambda i: (0, 0)),
                  pl.BlockSpec((1, D), lambda i: (0, 0))],
        out_specs=[pl.BlockSpec((bm, D), lambda i: (i, 0)),
                   pl.BlockSpec((bm, D), lambda i: (i, 0))],
        out_shape=[jax.ShapeDtypeStruct((M, D), F32), jax.ShapeDtypeStruct((M, D), BF16)],
        compiler_params=_cparams(("parallel",)),
        name="layer_norm",
    )(r, g.reshape(1, D), b.reshape(1, D))


def _gates_kernel(chunk, x_ref, w_ref, alog_ref, dtb_ref, gc_ref, beta_ref):
    ab = lax.dot_general(w_ref[...], x_ref[...], (((1,), (1,)), ((), ())),
                         preferred_element_type=F32)
    nh = gc_ref.shape[0]
    a = ab[:nh] + dtb_ref[...]
    b = ab[nh:]
    softplus = jnp.maximum(a, 0.0) + jnp.log1p(jnp.exp(-jnp.abs(a)))
    g = -jnp.exp(alog_ref[...]) * softplus
    pos = lax.broadcasted_iota(jnp.int32, g.shape, 1) % chunk
    s = 1
    while s < chunk:
        g = g + jnp.where(pos >= s, pltpu.roll(g, s, axis=1), 0.0)
        s *= 2
    gc_ref[...] = g
    beta_ref[...] = 1.0 / (1.0 + jnp.exp(-b))


def _gates(xb, w_ab_t, a_log, dt_bias, chunk):
    M, K = xb.shape
    nh = a_log.shape[0]
    bm = min(MM_BM, M)
    return pl.pallas_call(
        functools.partial(_gates_kernel, chunk),
        grid=(M // bm,),
        in_specs=[pl.BlockSpec((bm, K), lambda i: (i, 0)),
                  pl.BlockSpec((2 * nh, K), lambda i: (0, 0)),
                  pl.BlockSpec((nh, 1), lambda i: (0, 0)),
                  pl.BlockSpec((nh, 1), lambda i: (0, 0))],
        out_specs=[pl.BlockSpec((nh, bm), lambda i: (0, i)),
                   pl.BlockSpec((nh, bm), lambda i: (0, i))],
        out_shape=[jax.ShapeDtypeStruct((nh, M), F32), jax.ShapeDtypeStruct((nh, M), F32)],
        compiler_params=_cparams(("parallel",)),
        name="gdn_gates",
    )(xb, w_ab_t, a_log.reshape(nh, 1), dt_bias.reshape(nh, 1))


def _dot(a, b):
    return jnp.dot(a.astype(BF16), b.astype(BF16), preferred_element_type=F32)


def _dot_nt(a, b):
    return lax.dot_general(a.astype(BF16), b.astype(BF16), (((1,), (1,)), ((), ())),
                           preferred_element_type=F32)


def _gdn_kernel(chunk, q_ref, k_ref, v_ref, z_ref, wq_ref, wk_ref, wv_ref, gc_ref, beta_ref,
                nw_ref, o_ref, s_ref, hq_ref, hk_ref, hv_ref):
    tb = q_ref.shape[0]
    dk = HEAD_DIM
    nheads = q_ref.shape[1] // dk
    halo = hq_ref.shape[0]
    nchunks = tb // chunk

    @pl.when(pl.program_id(2) == 0)
    def _():
        s_ref[...] = jnp.zeros_like(s_ref)
        hq_ref[...] = jnp.zeros_like(hq_ref)
        hk_ref[...] = jnp.zeros_like(hk_ref)
        hv_ref[...] = jnp.zeros_like(hv_ref)

    def conv_silu(x_ref, w_ref, h_ref):
        x = x_ref[...].astype(F32)
        xs = jnp.concatenate([h_ref[...], x], axis=0)
        w = w_ref[...]
        y = x * w[CONV_TAPS - 1:CONV_TAPS]
        for j in range(CONV_TAPS - 1):
            off = halo - (CONV_TAPS - 1) + j
            y = y + xs[off:off + tb] * w[j:j + 1]
        h_ref[...] = x[tb - halo:]
        return _silu(y)

    q_all = conv_silu(q_ref, wq_ref, hq_ref)
    k_all = conv_silu(k_ref, wk_ref, hk_ref)
    v_all = conv_silu(v_ref, wv_ref, hv_ref)

    ri = lax.broadcasted_iota(jnp.int32, (chunk, chunk), 0)
    ci = lax.broadcasted_iota(jnp.int32, (chunk, chunk), 1)
    incl = ri >= ci
    strict = ri > ci
    eye = (ri == ci).astype(F32)
    off_masks = []
    b = 1
    while b < chunk:
        off_masks.append((ri // (2 * b) == ci // (2 * b)) & (ri // b != ci // b))
        b *= 2

    chains = [(g, c) for g in range(nheads) for c in range(nchunks)]
    a_mat, attn, rhs, lhs_top, qd, cdec = {}, {}, {}, {}, {}, {}
    for g in range(nheads):
        hs = slice(g * dk, (g + 1) * dk)
        q = q_all[:, hs]
        k = k_all[:, hs]
        v = v_all[:, hs]
        q = q * (lax.rsqrt(jnp.sum(q * q, axis=-1, keepdims=True) + L2_EPS) * (dk ** -0.5))
        k = k * lax.rsqrt(jnp.sum(k * k, axis=-1, keepdims=True) + L2_EPS)
        gc_row = jnp.broadcast_to(gc_ref[g], (LANES, tb))
        gc_col = gc_row.T
        beta_col = jnp.broadcast_to(beta_ref[g], (LANES, tb)).T
        eg_col = jnp.exp(gc_col)
        for c in range(nchunks):
            sl = slice(c * chunk, (c + 1) * chunk)
            qc, kc, vc = q[sl], k[sl], v[sl]
            bc = beta_col[sl]
            gcc = gc_col[sl]
            kb = kc * bc
            diff = gcc[:, :chunk] - gc_row[:chunk, sl]
            decay = jnp.where(incl, jnp.exp(jnp.where(incl, diff, 0.0)), 0.0)
            qk = _dot_nt(jnp.concatenate([qc, kb], axis=0), kc)
            attn[g, c] = qk[:chunk] * decay
            a_mat[g, c] = jnp.where(strict, qk[chunk:] * decay, 0.0)
            rhs[g, c] = jnp.concatenate([kb * eg_col[sl], vc * bc], axis=1)
            g_last = gcc[chunk - 1:chunk, :]
            lhs_top[g, c] = (kc * jnp.exp(g_last - gcc)).T
            qd[g, c] = qc * eg_col[sl]
            cdec[g, c] = jnp.exp(g_last)

    x = {ch: eye - jnp.where(off_masks[0], a_mat[ch], 0.0) for ch in chains}
    for off_mask in off_masks[1:]:
        y = {ch: _dot(jnp.where(off_mask, a_mat[ch], 0.0), x[ch]) for ch in chains}
        x = {ch: x[ch] - _dot(x[ch], y[ch]) for ch in chains}
    wu = {ch: _dot(x[ch], rhs[ch]) for ch in chains}
    st = {ch: _dot(jnp.concatenate([lhs_top[ch], attn[ch]], axis=0), wu[ch]) for ch in chains}

    s = [s_ref[g] for g in range(nheads)]
    outs = {}
    for c in range(nchunks):
        for g in range(nheads):
            t = st[g, c]
            lhs = jnp.concatenate([-t[:dk, :dk], qd[g, c] - t[dk:, :dk]], axis=0)
            r = _dot(lhs, s[g])
            outs[g, c] = r[dk:] + t[dk:, dk:]
            s[g] = s[g] * cdec[g, c] + r[:dk] + t[:dk, dk:]
    for g in range(nheads):
        s_ref[g] = s[g]

    nw = nw_ref[...]
    cols = []
    for g in range(nheads):
        og = jnp.concatenate([outs[g, c] for c in range(nchunks)], axis=0)
        cols.append(og * lax.rsqrt(jnp.mean(og * og, axis=-1, keepdims=True) + RMS_EPS_A) * nw)
    o = jnp.concatenate(cols, axis=1)
    o_ref[...] = (o * _silu(z_ref[...].astype(F32))).astype(o_ref.dtype)


def _gated_delta_net(h, conv_w_t, gc, beta, norm_w, batch, heads):
    M = h.shape[0]
    T = M // batch
    tb = min(GDN_TB, T)
    nt = T // tb
    dk = HEAD_DIM
    chunk = min(GDN_CHUNK, tb)
    hps = min(GDN_HEADS_PER_STEP, heads)
    ng = heads // hps
    width = hps * dk

    def hspec(off):
        return pl.BlockSpec((tb, width), lambda b, hh, t: (b * nt + t, hh + off))

    def wspec(off):
        return pl.BlockSpec((CONV_TAPS, width), lambda b, hh, t: (0, hh + off))

    gspec = pl.BlockSpec((hps, 1, tb), lambda b, hh, t: (hh, 0, b * nt + t))
    return pl.pallas_call(
        functools.partial(_gdn_kernel, chunk),
        grid=(batch, ng, nt),
        in_specs=[hspec(0), hspec(ng), hspec(2 * ng), hspec(3 * ng),
                  wspec(0), wspec(ng), wspec(2 * ng), gspec, gspec,
                  pl.BlockSpec((1, dk), lambda b, hh, t: (0, 0))],
        out_specs=pl.BlockSpec((tb, width), lambda b, hh, t: (b * nt + t, hh)),
        out_shape=jax.ShapeDtypeStruct((M, heads * dk), BF16),
        scratch_shapes=[pltpu.VMEM((hps, dk, dk), F32),
                        pltpu.VMEM((SUBLANES, width), F32),
                        pltpu.VMEM((SUBLANES, width), F32),
                        pltpu.VMEM((SUBLANES, width), F32)],
        compiler_params=_cparams(("parallel", "parallel", "arbitrary")),
        name="gated_delta_rule",
    )(h, h, h, h, conv_w_t, conv_w_t, conv_w_t,
      gc.reshape(heads, 1, M), beta.reshape(heads, 1, M), norm_w.reshape(1, dk))


def _t5_bucket_table(n):
    rel = np.arange(n)
    max_exact = REL_BUCKETS // 2
    nf = np.maximum(rel, 1).astype(np.float32)
    large = max_exact + (np.log(nf / np.float32(max_exact)) / np.float32(math.log(REL_MAX_DIST / max_exact))
                         * np.float32(REL_BUCKETS - max_exact)).astype(np.int32)
    large = np.minimum(large, REL_BUCKETS - 1)
    return np.where(rel < max_exact, rel, large)


def _attn_kernel(lam_init, q1_ref, q2_ref, k1_ref, k2_ref, v_ref, z_ref, brow_ref, lam_ref, sw_ref,
                 o_ref, m1_ref, l1_ref, a1_ref, m2_ref, l2_ref, a2_ref, bd_ref, bs_ref):
    tq = q1_ref.shape[0]
    dv = v_ref.shape[1]
    qi = pl.program_id(2)

    @pl.when(qi == 0)
    def _():
        r = pltpu.roll(jnp.broadcast_to(brow_ref[...], (tq, tq)), 0, axis=1, stride=1, stride_axis=0)
        r = r * LOG2_E
        ri = lax.broadcasted_iota(jnp.int32, (tq, tq), 0)
        ci = lax.broadcasted_iota(jnp.int32, (tq, tq), 1)
        bd_ref[...] = jnp.where(ri >= ci, r, -jnp.inf)
        bs_ref[...] = jnp.where(ri < ci, r, 0.0)

    m1_ref[...] = jnp.full_like(m1_ref, -jnp.inf)
    m2_ref[...] = jnp.full_like(m2_ref, -jnp.inf)
    l1_ref[...] = jnp.zeros_like(l1_ref)
    l2_ref[...] = jnp.zeros_like(l2_ref)
    a1_ref[...] = jnp.zeros_like(a1_ref)
    a2_ref[...] = jnp.zeros_like(a2_ref)

    q1 = q1_ref[...]
    q2 = q2_ref[...]

    def lanes(x, width):
        return jnp.concatenate([x] * (width // LANES), axis=1)

    def softmax_update(s, m_ref, l_ref):
        m_old = m_ref[...]
        m_new = jnp.maximum(m_old, jnp.max(s, axis=-1, keepdims=True))
        p = jnp.exp2(s - lanes(m_new, s.shape[1]))
        alpha = jnp.exp2(m_old - m_new)
        l_ref[...] = alpha * l_ref[...] + jnp.sum(p, axis=-1, keepdims=True)
        m_ref[...] = m_new
        return p.astype(BF16), alpha

    def step(kj, bias):
        start = pl.multiple_of(kj * tq, tq)
        s1 = lax.dot_general(q1, k1_ref[pl.ds(start, tq), :], (((1,), (1,)), ((), ())),
                             preferred_element_type=F32)
        s2 = lax.dot_general(q2, k2_ref[pl.ds(start, tq), :], (((1,), (1,)), ((), ())),
                             preferred_element_type=F32)
        if bias is not None:
            s1 = s1 + bias
            s2 = s2 + bias
        vblk = v_ref[pl.ds(start, tq), :]
        p1, alpha1 = softmax_update(s1, m1_ref, l1_ref)
        p2, alpha2 = softmax_update(s2, m2_ref, l2_ref)
        a1_ref[...] = lanes(alpha1, dv) * a1_ref[...] + jnp.dot(p1, vblk, preferred_element_type=F32)
        a2_ref[...] = lanes(alpha2, dv) * a2_ref[...] + jnp.dot(p2, vblk, preferred_element_type=F32)

    def far_body(kj, carry):
        step(kj, None)
        return carry

    lax.fori_loop(0, jnp.maximum(qi - 1, 0), far_body, 0)

    @pl.when(qi >= 1)
    def _():
        step(qi - 1, bs_ref[...])

    step(qi, bd_ref[...])

    lv = lam_ref[...]
    lam = (jnp.exp(jnp.sum(lv[0:1] * lv[1:2], axis=-1, keepdims=True))
           - jnp.exp(jnp.sum(lv[2:3] * lv[3:4], axis=-1, keepdims=True)) + lam_init)
    o = a1_ref[...] / lanes(l1_ref[...], dv) - lam * (a2_ref[...] / lanes(l2_ref[...], dv))
    o = o * lax.rsqrt(jnp.mean(o * o, axis=-1, keepdims=True) + RMS_EPS_B) * sw_ref[...]
    o = o * (1.0 - lam_init) * _silu(z_ref[...].astype(F32))
    o_ref[...] = o.astype(o_ref.dtype)


def _diff_attention(h, brow, lam_vecs, subln_w, batch, heads, lam_init):
    M = h.shape[0]
    T = M // batch
    tq = min(ATT_TQ, T)
    nq = T // tq
    dh = HEAD_DIM
    dv = 2 * dh
    return pl.pallas_call(
        functools.partial(_attn_kernel, lam_init),
        grid=(batch, heads, nq),
        in_specs=[pl.BlockSpec((tq, dh), lambda b, hh, i: (b * nq + i, 2 * hh)),
                  pl.BlockSpec((tq, dh), lambda b, hh, i: (b * nq + i, 2 * hh + 1)),
                  pl.BlockSpec((T, dh), lambda b, hh, i: (b, 2 * heads + 2 * hh)),
                  pl.BlockSpec((T, dh), lambda b, hh, i: (b, 2 * heads + 2 * hh + 1)),
                  pl.BlockSpec((T, dv), lambda b, hh, i: (b, 2 * heads + hh)),
                  pl.BlockSpec((tq, dv), lambda b, hh, i: (b * nq + i, 3 * heads + hh)),
                  pl.BlockSpec((None, 1, tq), lambda b, hh, i: (hh, 0, 0)),
                  pl.BlockSpec((4, dh), lambda b, hh, i: (0, 0)),
                  pl.BlockSpec((1, dv), lambda b, hh, i: (0, 0))],
        out_specs=pl.BlockSpec((tq, dv), lambda b, hh, i: (b * nq + i, hh)),
        out_shape=jax.ShapeDtypeStruct((M, heads * dv), BF16),
        scratch_shapes=[pltpu.VMEM((tq, LANES), F32), pltpu.VMEM((tq, LANES), F32), pltpu.VMEM((tq, dv), F32),
                        pltpu.VMEM((tq, LANES), F32), pltpu.VMEM((tq, LANES), F32), pltpu.VMEM((tq, dv), F32),
                        pltpu.VMEM((tq, tq), F32), pltpu.VMEM((tq, tq), F32)],
        compiler_params=_cparams(("parallel", "parallel", "arbitrary")),
        name="diff_attention",
    )(h, h, h, h, h, h, brow, lam_vecs, subln_w.reshape(1, dv))


def kernel(x, ln_g, ln_b, rel_bias, a_w_in, a_conv_w, a_a_log, a_dt_bias, a_norm_w, a_w_out,
           b_w_in, b_lam_q1, b_lam_k1, b_lam_q2, b_lam_k2, b_subln_w, b_w_out):
    batch, T, D = x.shape
    M = batch * T
    depth = ln_g.shape[0]
    a_heads = a_a_log.shape[1]
    b_heads = rel_bias.shape[1]
    alpha = (2.0 * depth) ** 0.25
    assert T % min(GDN_TB, T) == 0 and T % min(ATT_TQ, T) == 0 and min(ATT_TQ, T) >= LANES

    xf = x.reshape(M, D)
    xb = xf.astype(BF16)
    for i in range(depth):
        j = i // 2
        if i % 2 == 0:
            qkvz_w = 4 * a_heads * HEAD_DIM
            w_in = a_w_in[j]
            h = _matmul(xb, w_in[:, :qkvz_w].astype(BF16), BF16)
            chunk = min(GDN_CHUNK, GDN_TB, T)
            gc, beta = _gates(xb, w_in[:, qkvz_w:].T.astype(BF16), a_a_log[j], a_dt_bias[j], chunk)
            o = _gated_delta_net(h, a_conv_w[j].T, gc, beta, a_norm_w[j], batch, a_heads)
            w_out = a_w_out[j]
        else:
            tq = min(ATT_TQ, T)
            q_w = 2 * b_heads * HEAD_DIM
            col_scale = jnp.where(jnp.arange(b_w_in.shape[2]) < q_w, HEAD_DIM ** -0.5 * LOG2_E, 1.0)
            h = _matmul(xb, (b_w_in[j] * col_scale[None, :].astype(F32)).astype(BF16), BF16)
            table = rel_bias[_t5_bucket_table(tq)] - rel_bias[REL_BUCKETS - 1][None, :]
            brow = jnp.roll(table[::-1], 1, axis=0).T.reshape(b_heads, 1, tq)
            lam_vecs = jnp.stack([b_lam_q1[j], b_lam_k1[j], b_lam_q2[j], b_lam_k2[j]])
            lam_init = 0.8 - 0.6 * math.exp(-0.3 * i)
            o = _diff_attention(h, brow, lam_vecs, b_subln_w[j], batch, b_heads, lam_init)
            w_out = b_w_out[j]
        r = _matmul_residual(o, w_out.astype(BF16), xf, alpha)
        xf, xb = _layer_norm(r, ln_g[i], ln_b[i])
    return xf.reshape(batch, T, D)
```

```python
import functools
import math

import numpy as np
import jax
import jax.numpy as jnp
from jax import lax
from jax.experimental import pallas as pl
from jax.experimental.pallas import tpu as pltpu

F32 = jnp.float32
BF16 = jnp.bfloat16

LANES = 128
SUBLANES = 8
VMEM_LIMIT_BYTES = 56 * 1024 * 1024

CONV_TAPS = 4
HEAD_DIM = 128
LN_EPS = 1e-5
RMS_EPS_A = 1e-6
RMS_EPS_B = 1e-5
L2_EPS = 1e-6
REL_BUCKETS = 32
REL_MAX_DIST = 128
LOG2_E = math.log2(math.e)

MM_BM = 1024
MM_BN = 1024
GDN_CHUNK = 128
GDN_TB = 512
GDN_HEADS_PER_STEP = 4
ATT_TQ = 512
LN_BM = 256
CAST_BK = 512
CAST_BN = 2048


def _cparams(sem):
    return pltpu.CompilerParams(dimension_semantics=sem, vmem_limit_bytes=VMEM_LIMIT_BYTES)


def _silu(x):
    return x * (1.0 / (1.0 + jnp.exp(-x)))


def _cast_kernel(w_ref, scale_ref, o_ref):
    o_ref[...] = (w_ref[...] * scale_ref[...]).astype(o_ref.dtype)


def _cast_weight(w3, layer, ncols, col_scale):
    K = w3.shape[1]
    bk, bn = min(CAST_BK, K), min(CAST_BN, ncols)
    return pl.pallas_call(
        _cast_kernel,
        grid=(K // bk, ncols // bn),
        in_specs=[pl.BlockSpec((None, bk, bn), lambda i, j: (layer, i, j)),
                  pl.BlockSpec((1, bn), lambda i, j: (0, j))],
        out_specs=pl.BlockSpec((bk, bn), lambda i, j: (i, j)),
        out_shape=jax.ShapeDtypeStruct((K, ncols), BF16),
        compiler_params=_cparams(("parallel", "parallel")),
        name="cast_weight",
    )(w3, col_scale.reshape(1, ncols).astype(F32))


def _mm_kernel(x_ref, w_ref, o_ref):
    o_ref[...] = jnp.dot(x_ref[...], w_ref[...], preferred_element_type=F32).astype(o_ref.dtype)


def _matmul(x, w, out_dtype):
    M, K = x.shape
    N = w.shape[1]
    bm, bn = min(MM_BM, M), min(MM_BN, N)
    return pl.pallas_call(
        _mm_kernel,
        grid=(M // bm, N // bn),
        in_specs=[pl.BlockSpec((bm, K), lambda i, j: (i, 0)),
                  pl.BlockSpec((K, bn), lambda i, j: (0, j))],
        out_specs=pl.BlockSpec((bm, bn), lambda i, j: (i, j)),
        out_shape=jax.ShapeDtypeStruct((M, N), out_dtype),
        compiler_params=_cparams(("parallel", "parallel")),
        name="in_proj",
    )(x, w)


def _mm_res_kernel(alpha, o_ref, w_ref, x_ref, r_ref):
    y = jnp.dot(o_ref[...], w_ref[...], preferred_element_type=F32)
    r_ref[...] = alpha * x_ref[...] + y


def _matmul_residual(o, w, x, alpha):
    M, K = o.shape
    N = w.shape[1]
    bm, bn = min(MM_BM, M), min(MM_BN, N)
    return pl.pallas_call(
        functools.partial(_mm_res_kernel, alpha),
        grid=(M // bm, N // bn),
        in_specs=[pl.BlockSpec((bm, K), lambda i, j: (i, 0)),
                  pl.BlockSpec((K, bn), lambda i, j: (0, j)),
                  pl.BlockSpec((bm, bn), lambda i, j: (i, j))],
        out_specs=pl.BlockSpec((bm, bn), lambda i, j: (i, j)),
        out_shape=jax.ShapeDtypeStruct((M, N), F32),
        compiler_params=_cparams(("parallel", "parallel")),
        name="out_proj_residual",
    )(o, w, x)


def _ln_kernel(r_ref, g_ref, b_ref, o_ref, ob_ref):
    r = r_ref[...]
    mu = jnp.mean(r, axis=-1, keepdims=True)
    d = r - mu
    var = jnp.mean(d * d, axis=-1, keepdims=True)
    y = d * lax.rsqrt(var + LN_EPS) * g_ref[...] + b_ref[...]
    o_ref[...] = y
    ob_ref[...] = y.astype(BF16)


def _layer_norm(r, g, b):
    M, D = r.shape
    bm = min(LN_BM, M)
    return pl.pallas_call(
        _ln_kernel,
        grid=(M // bm,),
        in_specs=[pl.BlockSpec((bm, D), lambda i: (i, 0)),
                  pl.BlockSpec((1, D), lambda i: (0, 0)),
                  pl.BlockSpec((1, D), lambda i: (0, 0))],
        out_specs=[pl.BlockSpec((bm, D), lambda i: (i, 0)),
                   pl.BlockSpec((bm, D), lambda i: (i, 0))],
        out_shape=[jax.ShapeDtypeStruct((M, D), F32), jax.ShapeDtypeStruct((M, D), BF16)],
        compiler_params=_cparams(("parallel",)),
        name="layer_norm",
    )(r, g.reshape(1, D), b.reshape(1, D))


def _gates_kernel(chunk, x_ref, w_ref, alog_ref, dtb_ref, gc_ref, beta_ref):
    ab = lax.dot_general(w_ref[...], x_ref[...], (((1,), (1,)), ((), ())),
                         preferred_element_type=F32)
    nh = gc_ref.shape[0]
    a = ab[:nh] + dtb_ref[...]
    b = ab[nh:]
    softplus = jnp.maximum(a, 0.0) + jnp.log1p(jnp.exp(-jnp.abs(a)))
    g = -jnp.exp(alog_ref[...]) * softplus
    pos = lax.broadcasted_iota(jnp.int32, g.shape, 1) % chunk
    s = 1
    while s < chunk:
        g = g + jnp.where(pos >= s, pltpu.roll(g, s, axis=1), 0.0)
        s *= 2
    gc_ref[...] = g
    beta_ref[...] = 1.0 / (1.0 + jnp.exp(-b))


def _gates(xb, w_ab_t, a_log, dt_bias, chunk):
    M, K = xb.shape
    nh = a_log.shape[0]
    bm = min(MM_BM, M)
    return pl.pallas_call(
        functools.partial(_gates_kernel, chunk),
        grid=(M // bm,),
        in_specs=[pl.BlockSpec((bm, K), lambda i: (i, 0)),
                  pl.BlockSpec((2 * nh, K), lambda i: (0, 0)),
                  pl.BlockSpec((nh, 1), lambda i: (0, 0)),
                  pl.BlockSpec((nh, 1), lambda i: (0, 0))],
        out_specs=[pl.BlockSpec((nh, bm), lambda i: (0, i)),
                   pl.BlockSpec((nh, bm), lambda i: (0, i))],
        out_shape=[jax.ShapeDtypeStruct((nh, M), F32), jax.ShapeDtypeStruct((nh, M), F32)],
        compiler_params=_cparams(("parallel",)),
        name="gdn_gates",
    )(xb, w_ab_t, a_log.reshape(nh, 1), dt_bias.reshape(nh, 1))


def _dot(a, b):
    return jnp.dot(a.astype(BF16), b.astype(BF16), preferred_element_type=F32)


def _dot_nt(a, b):
    return lax.dot_general(a.astype(BF16), b.astype(BF16), (((1,), (1,)), ((), ())),
                           preferred_element_type=F32)


def _gdn_kernel(chunk, q_ref, k_ref, v_ref, z_ref, wq_ref, wk_ref, wv_ref, gc_ref, beta_ref,
                nw_ref, o_ref, s_ref, hq_ref, hk_ref, hv_ref):
    tb = q_ref.shape[0]
    dk = HEAD_DIM
    nheads = q_ref.shape[1] // dk
    halo = hq_ref.shape[0]
    nchunks = tb // chunk

    @pl.when(pl.program_id(2) == 0)
    def _():
        s_ref[...] = jnp.zeros_like(s_ref)
        hq_ref[...] = jnp.zeros_like(hq_ref)
        hk_ref[...] = jnp.zeros_like(hk_ref)
        hv_ref[...] = jnp.zeros_like(hv_ref)

    def conv_silu(x_ref, w_ref, h_ref):
        x = x_ref[...].astype(F32)
        xs = jnp.concatenate([h_ref[...], x], axis=0)
        w = w_ref[...]
        y = x * w[CONV_TAPS - 1:CONV_TAPS]
        for j in range(CONV_TAPS - 1):
            off = halo - (CONV_TAPS - 1) + j
            y = y + xs[off:off + tb] * w[j:j + 1]
        h_ref[...] = x[tb - halo:]
        return _silu(y)

    q_all = conv_silu(q_ref, wq_ref, hq_ref)
    k_all = conv_silu(k_ref, wk_ref, hk_ref)
    v_all = conv_silu(v_ref, wv_ref, hv_ref)

    ri = lax.broadcasted_iota(jnp.int32, (chunk, chunk), 0)
    ci = lax.broadcasted_iota(jnp.int32, (chunk, chunk), 1)
    incl = ri >= ci
    strict = ri > ci
    eye = (ri == ci).astype(F32)
    off_masks = []
    b = 1
    while b < chunk:
        off_masks.append((ri // (2 * b) == ci // (2 * b)) & (ri // b != ci // b))
        b *= 2

    chains = [(g, c) for g in range(nheads) for c in range(nchunks)]
    a_mat, attn, rhs, lhs_top, qd, cdec = {}, {}, {}, {}, {}, {}
    for g in range(nheads):
        hs = slice(g * dk, (g + 1) * dk)
        q = q_all[:, hs]
        k = k_all[:, hs]
        v = v_all[:, hs]
        q = q * (lax.rsqrt(jnp.sum(q * q, axis=-1, keepdims=True) + L2_EPS) * (dk ** -0.5))
        k = k * lax.rsqrt(jnp.sum(k * k, axis=-1, keepdims=True) + L2_EPS)
        gc_row = jnp.broadcast_to(gc_ref[g], (LANES, tb))
        gc_col = gc_row.T
        beta_col = jnp.broadcast_to(beta_ref[g], (LANES, tb)).T
        eg_col = jnp.exp(gc_col)
        for c in range(nchunks):
            sl = slice(c * chunk, (c + 1) * chunk)
            qc, kc, vc = q[sl], k[sl], v[sl]
            bc = beta_col[sl]
            gcc = gc_col[sl]
            kb = kc * bc
            diff = gcc[:, :chunk] - gc_row[:chunk, sl]
            decay = jnp.where(incl, jnp.exp(jnp.where(incl, diff, 0.0)), 0.0)
            qk = _dot_nt(jnp.concatenate([qc, kb], axis=0), kc)
            attn[g, c] = qk[:chunk] * decay
            a_mat[g, c] = jnp.where(strict, qk[chunk:] * decay, 0.0)
            rhs[g, c] = jnp.concatenate([kb * eg_col[sl], vc * bc], axis=1)
            g_last = gcc[chunk - 1:chunk, :]
            lhs_top[g, c] = (kc * jnp.exp(g_last - gcc)).T
            qd[g, c] = qc * eg_col[sl]
            cdec[g, c] = jnp.exp(g_last)

    x = {ch: eye - jnp.where(off_masks[0], a_mat[ch], 0.0) for ch in chains}
    for off_mask in off_masks[1:]:
        y = {ch: _dot(jnp.where(off_mask, a_mat[ch], 0.0), x[ch]) for ch in chains}
        x = {ch: x[ch] - _dot(x[ch], y[ch]) for ch in chains}
    wu = {ch: _dot(x[ch], rhs[ch]) for ch in chains}
    st = {ch: _dot(jnp.concatenate([lhs_top[ch], attn[ch]], axis=0), wu[ch]) for ch in chains}

    s = [s_ref[g] for g in range(nheads)]
    outs = {}
    for c in range(nchunks):
        for g in range(nheads):
            t = st[g, c]
            lhs = jnp.concatenate([-t[:dk, :dk], qd[g, c] - t[dk:, :dk]], axis=0)
            r = _dot(lhs, s[g])
            outs[g, c] = r[dk:] + t[dk:, dk:]
            s[g] = s[g] * cdec[g, c] + r[:dk] + t[:dk, dk:]
    for g in range(nheads):
        s_ref[g] = s[g]

    nw = nw_ref[...]
    cols = []
    for g in range(nheads):
        og = jnp.concatenate([outs[g, c] for c in range(nchunks)], axis=0)
        cols.append(og * lax.rsqrt(jnp.mean(og * og, axis=-1, keepdims=True) + RMS_EPS_A) * nw)
    o = jnp.concatenate(cols, axis=1)
    o_ref[...] = (o * _silu(z_ref[...].astype(F32))).astype(o_ref.dtype)


def _gated_delta_net(h, conv_w_t, gc, beta, norm_w, batch, heads):
    M = h.shape[0]
    T = M // batch
    tb = min(GDN_TB, T)
    nt = T // tb
    dk = HEAD_DIM
    chunk = min(GDN_CHUNK, tb)
    hps = min(GDN_HEADS_PER_STEP, heads)
    ng = heads // hps
    width = hps * dk

    def hspec(off):
        return pl.BlockSpec((tb, width), lambda b, hh, t: (b * nt + t, hh + off))

    def wspec(off):
        return pl.BlockSpec((CONV_TAPS, width), lambda b, hh, t: (0, hh + off))

    gspec = pl.BlockSpec((hps, 1, tb), lambda b, hh, t: (hh, 0, b * nt + t))
    return pl.pallas_call(
        functools.partial(_gdn_kernel, chunk),
        grid=(batch, ng, nt),
        in_specs=[hspec(0), hspec(ng), hspec(2 * ng), hspec(3 * ng),
                  wspec(0), wspec(ng), wspec(2 * ng), gspec, gspec,
                  pl.BlockSpec((1, dk), lambda b, hh, t: (0, 0))],
        out_specs=pl.BlockSpec((tb, width), lambda b, hh, t: (b * nt + t, hh)),
        out_shape=jax.ShapeDtypeStruct((M, heads * dk), BF16),
        scratch_shapes=[pltpu.VMEM((hps, dk, dk), F32),
                        pltpu.VMEM((SUBLANES, width), F32),
                        pltpu.VMEM((SUBLANES, width), F32),
                        pltpu.VMEM((SUBLANES, width), F32)],
        compiler_params=_cparams(("parallel", "parallel", "arbitrary")),
        name="gated_delta_rule",
    )(h, h, h, h, conv_w_t, conv_w_t, conv_w_t,
      gc.reshape(heads, 1, M), beta.reshape(heads, 1, M), norm_w.reshape(1, dk))


def _t5_bucket_table(n):
    rel = np.arange(n)
    max_exact = REL_BUCKETS // 2
    nf = np.maximum(rel, 1).astype(np.float32)
    large = max_exact + (np.log(nf / np.float32(max_exact)) / np.float32(math.log(REL_MAX_DIST / max_exact))
                         * np.float32(REL_BUCKETS - max_exact)).astype(np.int32)
    large = np.minimum(large, REL_BUCKETS - 1)
    return np.where(rel < max_exact, rel, large)


def _attn_kernel(lam_init, q1_ref, q2_ref, k1_ref, k2_ref, v_ref, z_ref, brow_ref, lam_ref, sw_ref,
                 o_ref, m1_ref, l1_ref, a1_ref, m2_ref, l2_ref, a2_ref, bd_ref, bs_ref, s_ref):
    tq = q1_ref.shape[0]
    dv = v_ref.shape[1]
    qi = pl.program_id(2)

    @pl.when(qi == 0)
    def _():
        r = pltpu.roll(jnp.broadcast_to(brow_ref[...], (tq, tq)), 0, axis=1, stride=1, stride_axis=0)
        r = r * LOG2_E
        ri = lax.broadcasted_iota(jnp.int32, (tq, tq), 0)
        ci = lax.broadcasted_iota(jnp.int32, (tq, tq), 1)
        bd_ref[...] = jnp.where(ri >= ci, r, -jnp.inf)
        bs_ref[...] = jnp.where(ri < ci, r, 0.0)[:LANES, tq - LANES:]

    m1_ref[...] = jnp.full_like(m1_ref, -jnp.inf)
    m2_ref[...] = jnp.full_like(m2_ref, -jnp.inf)
    l1_ref[...] = jnp.zeros_like(l1_ref)
    l2_ref[...] = jnp.zeros_like(l2_ref)
    a1_ref[...] = jnp.zeros_like(a1_ref)
    a2_ref[...] = jnp.zeros_like(a2_ref)

    q1 = q1_ref[...]
    q2 = q2_ref[...]

    def lanes(x, width):
        return jnp.concatenate([x] * (width // LANES), axis=1)

    def softmax_update(s, m_ref, l_ref):
        m_old = m_ref[...]
        m_new = jnp.maximum(m_old, jnp.max(s, axis=-1, keepdims=True))
        p = jnp.exp2(s - lanes(m_new, s.shape[1]))
        alpha = jnp.exp2(m_old - m_new)
        l_ref[...] = alpha * l_ref[...] + jnp.sum(p, axis=-1, keepdims=True)
        m_ref[...] = m_new
        return p.astype(BF16), alpha

    def scores(kj, slot):
        start = pl.multiple_of(kj * tq, tq)
        s_ref[slot, 0] = lax.dot_general(q1, k1_ref[pl.ds(start, tq), :], (((1,), (1,)), ((), ())),
                                         preferred_element_type=F32)
        s_ref[slot, 1] = lax.dot_general(q2, k2_ref[pl.ds(start, tq), :], (((1,), (1,)), ((), ())),
                                         preferred_element_type=F32)

    def consume(kj, slot, bias_ref):
        start = pl.multiple_of(kj * tq, tq)
        vblk = v_ref[pl.ds(start, tq), :]
        for stream, (m_ref, l_ref, a_ref) in enumerate(((m1_ref, l1_ref, a1_ref), (m2_ref, l2_ref, a2_ref))):
            s = s_ref[slot, stream]
            if bias_ref is not None:
                s = s + bias_ref[...]
            p, alpha = softmax_update(s, m_ref, l_ref)
            a_ref[...] = lanes(alpha, dv) * a_ref[...] + jnp.dot(p, vblk, preferred_element_type=F32)

    def add_previous_block_bias(kj, slot):
        flag = jnp.where(kj == qi - 1, 1.0, 0.0)
        corner = flag * bs_ref[...]
        for stream in range(2):
            s_ref[slot, stream, :LANES, tq - LANES:] = s_ref[slot, stream, :LANES, tq - LANES:] + corner

    odd = qi % 2

    @pl.when(odd == 0)
    def _():
        scores(0, 0)

    @pl.when(odd == 1)
    def _():
        scores(0, 1)
        add_previous_block_bias(0, 1)
        scores(1, 0)
        consume(0, 1, None)

    def pair_body(t, carry):
        kj = odd + 2 * t
        scores(kj + 1, 1)
        consume(kj, 0, None)
        add_previous_block_bias(kj + 1, 1)
        scores(kj + 2, 0)
        consume(kj + 1, 1, None)
        return carry

    lax.fori_loop(0, qi // 2, pair_body, 0)
    consume(qi, 0, bd_ref)

    lv = lam_ref[...]
    lam = (jnp.exp(jnp.sum(lv[0:1] * lv[1:2], axis=-1, keepdims=True))
           - jnp.exp(jnp.sum(lv[2:3] * lv[3:4], axis=-1, keepdims=True)) + lam_init)
    o = a1_ref[...] / lanes(l1_ref[...], dv) - lam * (a2_ref[...] / lanes(l2_ref[...], dv))
    o = o * lax.rsqrt(jnp.mean(o * o, axis=-1, keepdims=True) + RMS_EPS_B) * sw_ref[...]
    o = o * (1.0 - lam_init) * _silu(z_ref[...].astype(F32))
    o_ref[...] = o.astype(o_ref.dtype)


def _diff_attention(h, brow, lam_vecs, subln_w, batch, heads, lam_init):
    M = h.shape[0]
    T = M // batch
    tq = min(ATT_TQ, T)
    nq = T // tq
    dh = HEAD_DIM
    dv = 2 * dh
    return pl.pallas_call(
        functools.partial(_attn_kernel, lam_init),
        grid=(batch, heads, nq),
        in_specs=[pl.BlockSpec((tq, dh), lambda b, hh, i: (b * nq + i, 2 * hh)),
                  pl.BlockSpec((tq, dh), lambda b, hh, i: (b * nq + i, 2 * hh + 1)),
                  pl.BlockSpec((T, dh), lambda b, hh, i: (b, 2 * heads + 2 * hh)),
                  pl.BlockSpec((T, dh), lambda b, hh, i: (b, 2 * heads + 2 * hh + 1)),
                  pl.BlockSpec((T, dv), lambda b, hh, i: (b, 2 * heads + hh)),
                  pl.BlockSpec((tq, dv), lambda b, hh, i: (b * nq + i, 3 * heads + hh)),
                  pl.BlockSpec((None, 1, tq), lambda b, hh, i: (hh, 0, 0)),
                  pl.BlockSpec((4, dh), lambda b, hh, i: (0, 0)),
                  pl.BlockSpec((1, dv), lambda b, hh, i: (0, 0))],
        out_specs=pl.BlockSpec((tq, dv), lambda b, hh, i: (b * nq + i, hh)),
        out_shape=jax.ShapeDtypeStruct((M, heads * dv), BF16),
        scratch_shapes=[pltpu.VMEM((tq, LANES), F32), pltpu.VMEM((tq, LANES), F32), pltpu.VMEM((tq, dv), F32),
                        pltpu.VMEM((tq, LANES), F32), pltpu.VMEM((tq, LANES), F32), pltpu.VMEM((tq, dv), F32),
                        pltpu.VMEM((tq, tq), F32), pltpu.VMEM((LANES, LANES), F32),
                        pltpu.VMEM((2, 2, tq, tq), F32)],
        compiler_params=_cparams(("parallel", "parallel", "arbitrary")),
        name="diff_attention",
    )(h, h, h, h, h, h, brow, lam_vecs, subln_w.reshape(1, dv))


def kernel(x, ln_g, ln_b, rel_bias, a_w_in, a_conv_w, a_a_log, a_dt_bias, a_norm_w, a_w_out,
           b_w_in, b_lam_q1, b_lam_k1, b_lam_q2, b_lam_k2, b_subln_w, b_w_out):
    batch, T, D = x.shape
    M = batch * T
    depth = ln_g.shape[0]
    a_heads = a_a_log.shape[1]
    b_heads = rel_bias.shape[1]
    alpha = (2.0 * depth) ** 0.25
    assert T % min(GDN_TB, T) == 0 and T % min(ATT_TQ, T) == 0 and min(ATT_TQ, T) >= LANES

    xf = x.reshape(M, D)
    xb = xf.astype(BF16)
    for i in range(depth):
        j = i // 2
        if i % 2 == 0:
            qkvz_w = 4 * a_heads * HEAD_DIM
            h = _matmul(xb, _cast_weight(a_w_in, j, qkvz_w, jnp.ones((qkvz_w,), F32)), BF16)
            chunk = min(GDN_CHUNK, GDN_TB, T)
            gc, beta = _gates(xb, a_w_in[j, :, qkvz_w:].T.astype(BF16), a_a_log[j], a_dt_bias[j], chunk)
            o = _gated_delta_net(h, a_conv_w[j].T, gc, beta, a_norm_w[j], batch, a_heads)
            w_out = a_w_out
        else:
            tq = min(ATT_TQ, T)
            q_w = 2 * b_heads * HEAD_DIM
            n_in = b_w_in.shape[2]
            col_scale = jnp.where(jnp.arange(n_in) < q_w, HEAD_DIM ** -0.5 * LOG2_E, 1.0)
            h = _matmul(xb, _cast_weight(b_w_in, j, n_in, col_scale), BF16)
            table = rel_bias[_t5_bucket_table(tq)] - rel_bias[REL_BUCKETS - 1][None, :]
            brow = jnp.roll(table[::-1], 1, axis=0).T.reshape(b_heads, 1, tq)
            lam_vecs = jnp.stack([b_lam_q1[j], b_lam_k1[j], b_lam_q2[j], b_lam_k2[j]])
            lam_init = 0.8 - 0.6 * math.exp(-0.3 * i)
            o = _diff_attention(h, brow, lam_vecs, b_subln_w[j], batch, b_heads, lam_init)
            w_out = b_w_out
        w_out_bf = _cast_weight(w_out, j, w_out.shape[2], jnp.ones((w_out.shape[2],), F32))
        r = _matmul_residual(o, w_out_bf, xf, alpha)
        xf, xb = _layer_norm(r, ln_g[i], ln_b[i])
    return xf.reshape(batch, T, D)
```

```python
import functools
import math

import numpy as np
import jax
import jax.numpy as jnp
from jax import lax
from jax.experimental import pallas as pl
from jax.experimental.pallas import tpu as pltpu

F32 = jnp.float32
BF16 = jnp.bfloat16

LANES = 128
SUBLANES = 8
VMEM_LIMIT_BYTES = 56 * 1024 * 1024

CONV_TAPS = 4
HEAD_DIM = 128
LN_EPS = 1e-5
RMS_EPS_A = 1e-6
RMS_EPS_B = 1e-5
L2_EPS = 1e-6
REL_BUCKETS = 32
REL_MAX_DIST = 128
LOG2_E = math.log2(math.e)

MM_BM = 1024
MM_BN = 1024
GDN_CHUNK = 128
GDN_TB = 512
GDN_HEADS_PER_STEP = 4
ATT_TQ = 512
LN_BM = 256
GATES_BM = 512
CAST_BK = 512
CAST_BN = 2048


def _cparams(sem):
    return pltpu.CompilerParams(dimension_semantics=sem, vmem_limit_bytes=VMEM_LIMIT_BYTES)


def _silu(x):
    h = 0.5 * x
    return h + h * jnp.tanh(h)


def _cast_kernel(w_ref, scale_ref, o_ref):
    o_ref[...] = (w_ref[...] * scale_ref[...]).astype(o_ref.dtype)


def _cast_weight(w3, layer, ncols, col_scale):
    K = w3.shape[1]
    bk, bn = min(CAST_BK, K), min(CAST_BN, ncols)
    return pl.pallas_call(
        _cast_kernel,
        grid=(K // bk, ncols // bn),
        in_specs=[pl.BlockSpec((None, bk, bn), lambda i, j: (layer, i, j)),
                  pl.BlockSpec((1, bn), lambda i, j: (0, j))],
        out_specs=pl.BlockSpec((bk, bn), lambda i, j: (i, j)),
        out_shape=jax.ShapeDtypeStruct((K, ncols), BF16),
        compiler_params=_cparams(("parallel", "parallel")),
        name="cast_weight",
    )(w3, col_scale.reshape(1, ncols).astype(F32))


def _mm_kernel(x_ref, w_ref, o_ref):
    o_ref[...] = jnp.dot(x_ref[...], w_ref[...], preferred_element_type=F32).astype(o_ref.dtype)


def _matmul(x, w, out_dtype):
    M, K = x.shape
    N = w.shape[1]
    bm, bn = min(MM_BM, M), min(MM_BN, N)
    return pl.pallas_call(
        _mm_kernel,
        grid=(M // bm, N // bn),
        in_specs=[pl.BlockSpec((bm, K), lambda i, j: (i, 0)),
                  pl.BlockSpec((K, bn), lambda i, j: (0, j))],
        out_specs=pl.BlockSpec((bm, bn), lambda i, j: (i, j)),
        out_shape=jax.ShapeDtypeStruct((M, N), out_dtype),
        compiler_params=_cparams(("parallel", "parallel")),
        name="in_proj",
    )(x, w)


def _mm_res_kernel(alpha, o_ref, w_ref, x_ref, r_ref):
    y = jnp.dot(o_ref[...], w_ref[...], preferred_element_type=F32)
    r_ref[...] = alpha * x_ref[...] + y


def _matmul_residual(o, w, x, alpha):
    M, K = o.shape
    N = w.shape[1]
    bm, bn = min(MM_BM, M), min(MM_BN, N)
    return pl.pallas_call(
        functools.partial(_mm_res_kernel, alpha),
        grid=(M // bm, N // bn),
        in_specs=[pl.BlockSpec((bm, K), lambda i, j: (i, 0)),
                  pl.BlockSpec((K, bn), lambda i, j: (0, j)),
                  pl.BlockSpec((bm, bn), lambda i, j: (i, j))],
        out_specs=pl.BlockSpec((bm, bn), lambda i, j: (i, j)),
        out_shape=jax.ShapeDtypeStruct((M, N), F32),
        compiler_params=_cparams(("parallel", "parallel")),
        name="out_proj_residual",
    )(o, w, x)


def _ln_kernel(r_ref, g_ref, b_ref, o_ref, ob_ref):
    r = r_ref[...]
    mu = jnp.mean(r, axis=-1, keepdims=True)
    d = r - mu
    var = jnp.mean(d * d, axis=-1, keepdims=True)
    y = d * lax.rsqrt(var + LN_EPS) * g_ref[...] + b_ref[...]
    o_ref[...] = y
    ob_ref[...] = y.astype(BF16)


def _layer_norm(r, g, b):
    M, D = r.shape
    bm = min(LN_BM, M)
    return pl.pallas_call(
        _ln_kernel,
        grid=(M // bm,),
        in_specs=[pl.BlockSpec((bm, D), lambda i: (i, 0)),
                  pl.BlockSpec((1, D), lambda i: (0, 0)),
                  pl.BlockSpec((1, D), lambda i: (0, 0))],
        out_specs=[pl.BlockSpec((bm, D), lambda i: (i, 0)),
                   pl.BlockSpec((bm, D), lambda i: (i, 0))],
        out_shape=[jax.ShapeDtypeStruct((M, D), F32), jax.ShapeDtypeStruct((M, D), BF16)],
        compiler_params=_cparams(("parallel",)),
        name="layer_norm",
    )(r, g.reshape(1, D), b.reshape(1, D))


def _gates_kernel(chunk, x_ref, w_ref, alog_ref, dtb_ref, gc_ref, beta_ref, xb_ref):
    nh = gc_ref.shape[0]
    xb = x_ref[...].astype(BF16)
    xb_ref[...] = xb
    lane = lax.broadcasted_iota(jnp.int32, w_ref.shape, 1)
    w = jnp.where(lane < 2 * nh, w_ref[...], 0.0).astype(BF16)
    ab = jnp.dot(xb, w, preferred_element_type=F32).T
    a = ab[:nh] + dtb_ref[...]
    b = ab[nh:2 * nh]
    softplus = jnp.maximum(a, 0.0) + jnp.log1p(jnp.exp(-jnp.abs(a)))
    g = -jnp.exp(alog_ref[...]) * softplus
    pos = lax.broadcasted_iota(jnp.int32, g.shape, 1) % chunk
    s = 1
    while s < chunk:
        g = g + jnp.where(pos >= s, pltpu.roll(g, s, axis=1), 0.0)
        s *= 2
    gc_ref[...] = g
    beta_ref[...] = 1.0 / (1.0 + jnp.exp(-b))


def _gates(x, w3, layer, col0, a_log, dt_bias, chunk):
    M, K = x.shape
    nh = a_log.shape[0]
    bm = min(GATES_BM, M)
    assert col0 % LANES == 0 and 2 * nh <= LANES
    return pl.pallas_call(
        functools.partial(_gates_kernel, chunk),
        grid=(M // bm,),
        in_specs=[pl.BlockSpec((bm, K), lambda i: (i, 0)),
                  pl.BlockSpec((None, K, LANES), lambda i: (layer, 0, col0 // LANES)),
                  pl.BlockSpec((nh, 1), lambda i: (0, 0)),
                  pl.BlockSpec((nh, 1), lambda i: (0, 0))],
        out_specs=[pl.BlockSpec((nh, bm), lambda i: (0, i)),
                   pl.BlockSpec((nh, bm), lambda i: (0, i)),
                   pl.BlockSpec((bm, K), lambda i: (i, 0))],
        out_shape=[jax.ShapeDtypeStruct((nh, M), F32), jax.ShapeDtypeStruct((nh, M), F32),
                   jax.ShapeDtypeStruct((M, K), BF16)],
        compiler_params=_cparams(("parallel",)),
        name="gdn_gates",
    )(x, w3, a_log.reshape(nh, 1), dt_bias.reshape(nh, 1))


def _dot(a, b):
    return jnp.dot(a.astype(BF16), b.astype(BF16), preferred_element_type=F32)


def _dot_nt(a, b):
    return lax.dot_general(a.astype(BF16), b.astype(BF16), (((1,), (1,)), ((), ())),
                           preferred_element_type=F32)


def _gdn_kernel(chunk, q_ref, k_ref, v_ref, z_ref, wq_ref, wk_ref, wv_ref, gc_ref, beta_ref,
                nw_ref, o_ref, s_ref, hq_ref, hk_ref, hv_ref):
    tb = q_ref.shape[0]
    dk = HEAD_DIM
    nheads = q_ref.shape[1] // dk
    halo = hq_ref.shape[0]
    nchunks = tb // chunk

    @pl.when(pl.program_id(2) == 0)
    def _():
        s_ref[...] = jnp.zeros_like(s_ref)
        hq_ref[...] = jnp.zeros_like(hq_ref)
        hk_ref[...] = jnp.zeros_like(hk_ref)
        hv_ref[...] = jnp.zeros_like(hv_ref)

    def conv_silu(x_ref, w_ref, h_ref):
        x = x_ref[...].astype(F32)
        xs = jnp.concatenate([h_ref[...], x], axis=0)
        w = w_ref[...]
        y = x * w[CONV_TAPS - 1:CONV_TAPS]
        for j in range(CONV_TAPS - 1):
            off = halo - (CONV_TAPS - 1) + j
            y = y + xs[off:off + tb] * w[j:j + 1]
        h_ref[...] = x[tb - halo:]
        return _silu(y)

    q_all = conv_silu(q_ref, wq_ref, hq_ref)
    k_all = conv_silu(k_ref, wk_ref, hk_ref)
    v_all = conv_silu(v_ref, wv_ref, hv_ref)

    ri = lax.broadcasted_iota(jnp.int32, (chunk, chunk), 0)
    ci = lax.broadcasted_iota(jnp.int32, (chunk, chunk), 1)
    incl = ri >= ci
    strict = ri > ci
    eye = (ri == ci).astype(F32)
    off_masks = []
    b = 1
    while b < chunk:
        off_masks.append((ri // (2 * b) == ci // (2 * b)) & (ri // b != ci // b))
        b *= 2

    chains = [(g, c) for g in range(nheads) for c in range(nchunks)]
    a_mat, attn, rhs, lhs_top, qd, cdec = {}, {}, {}, {}, {}, {}
    for g in range(nheads):
        hs = slice(g * dk, (g + 1) * dk)
        q = q_all[:, hs]
        k = k_all[:, hs]
        v = v_all[:, hs]
        q = q * (lax.rsqrt(jnp.sum(q * q, axis=-1, keepdims=True) + L2_EPS) * (dk ** -0.5))
        k = k * lax.rsqrt(jnp.sum(k * k, axis=-1, keepdims=True) + L2_EPS)
        gc_row = jnp.broadcast_to(gc_ref[g], (LANES, tb))
        gc_col = gc_row.T
        beta_col = jnp.broadcast_to(beta_ref[g], (LANES, tb)).T
        eg_col = jnp.exp(gc_col)
        for c in range(nchunks):
            sl = slice(c * chunk, (c + 1) * chunk)
            qc, kc, vc = q[sl], k[sl], v[sl]
            bc = beta_col[sl]
            gcc = gc_col[sl]
            kb = kc * bc
            diff = gcc[:, :chunk] - gc_row[:chunk, sl]
            decay = jnp.where(incl, jnp.exp(jnp.where(incl, diff, 0.0)), 0.0)
            qk = _dot_nt(jnp.concatenate([qc, kb], axis=0), kc)
            attn[g, c] = qk[:chunk] * decay
            a_mat[g, c] = jnp.where(strict, qk[chunk:] * decay, 0.0)
            rhs[g, c] = jnp.concatenate([kb * eg_col[sl], vc * bc], axis=1)
            g_last = gcc[chunk - 1:chunk, :]
            lhs_top[g, c] = (kc * jnp.exp(g_last - gcc)).T
            qd[g, c] = qc * eg_col[sl]
            cdec[g, c] = jnp.exp(g_last)

    x = {ch: eye - jnp.where(off_masks[0], a_mat[ch], 0.0) for ch in chains}
    for off_mask in off_masks[1:]:
        y = {ch: _dot(jnp.where(off_mask, a_mat[ch], 0.0), x[ch]) for ch in chains}
        x = {ch: x[ch] - _dot(x[ch], y[ch]) for ch in chains}
    wu = {ch: _dot(x[ch], rhs[ch]) for ch in chains}
    st = {ch: _dot(jnp.concatenate([lhs_top[ch], attn[ch]], axis=0), wu[ch]) for ch in chains}

    s = [s_ref[g] for g in range(nheads)]
    outs = {}
    for c in range(nchunks):
        for g in range(nheads):
            t = st[g, c]
            lhs = jnp.concatenate([-t[:dk, :dk], qd[g, c] - t[dk:, :dk]], axis=0)
            r = _dot(lhs, s[g])
            outs[g, c] = r[dk:] + t[dk:, dk:]
            s[g] = s[g] * cdec[g, c] + r[:dk] + t[:dk, dk:]
    for g in range(nheads):
        s_ref[g] = s[g]

    nw = nw_ref[...]
    cols = []
    for g in range(nheads):
        og = jnp.concatenate([outs[g, c] for c in range(nchunks)], axis=0)
        cols.append(og * lax.rsqrt(jnp.mean(og * og, axis=-1, keepdims=True) + RMS_EPS_A) * nw)
    o = jnp.concatenate(cols, axis=1)
    o_ref[...] = (o * _silu(z_ref[...].astype(F32))).astype(o_ref.dtype)


def _gated_delta_net(h, conv_w_t, gc, beta, norm_w, batch, heads):
    M = h.shape[0]
    T = M // batch
    tb = min(GDN_TB, T)
    nt = T // tb
    dk = HEAD_DIM
    chunk = min(GDN_CHUNK, tb)
    hps = min(GDN_HEADS_PER_STEP, heads)
    ng = heads // hps
    width = hps * dk

    def hspec(off):
        return pl.BlockSpec((tb, width), lambda b, hh, t: (b * nt + t, hh + off))

    def wspec(off):
        return pl.BlockSpec((CONV_TAPS, width), lambda b, hh, t: (0, hh + off))

    gspec = pl.BlockSpec((hps, 1, tb), lambda b, hh, t: (hh, 0, b * nt + t))
    return pl.pallas_call(
        functools.partial(_gdn_kernel, chunk),
        grid=(batch, ng, nt),
        in_specs=[hspec(0), hspec(ng), hspec(2 * ng), hspec(3 * ng),
                  wspec(0), wspec(ng), wspec(2 * ng), gspec, gspec,
                  pl.BlockSpec((1, dk), lambda b, hh, t: (0, 0))],
        out_specs=pl.BlockSpec((tb, width), lambda b, hh, t: (b * nt + t, hh)),
        out_shape=jax.ShapeDtypeStruct((M, heads * dk), BF16),
        scratch_shapes=[pltpu.VMEM((hps, dk, dk), F32),
                        pltpu.VMEM((SUBLANES, width), F32),
                        pltpu.VMEM((SUBLANES, width), F32),
                        pltpu.VMEM((SUBLANES, width), F32)],
        compiler_params=_cparams(("parallel", "parallel", "arbitrary")),
        name="gated_delta_rule",
    )(h, h, h, h, conv_w_t, conv_w_t, conv_w_t,
      gc.reshape(heads, 1, M), beta.reshape(heads, 1, M), norm_w.reshape(1, dk))


def _t5_bucket_table(n):
    rel = np.arange(n)
    max_exact = REL_BUCKETS // 2
    nf = np.maximum(rel, 1).astype(np.float32)
    large = max_exact + (np.log(nf / np.float32(max_exact)) / np.float32(math.log(REL_MAX_DIST / max_exact))
                         * np.float32(REL_BUCKETS - max_exact)).astype(np.int32)
    large = np.minimum(large, REL_BUCKETS - 1)
    return np.where(rel < max_exact, rel, large)


def _attn_kernel(lam_init, q1_ref, q2_ref, k1_ref, k2_ref, v_ref, z_ref, brow_ref, lam_ref, sw_ref,
                 o_ref, m1_ref, l1_ref, a1_ref, m2_ref, l2_ref, a2_ref, bd_ref, bs_ref, s_ref):
    tq = q1_ref.shape[0]
    dv = v_ref.shape[1]
    qi = pl.program_id(2)

    @pl.when(qi == 0)
    def _():
        r = pltpu.roll(jnp.broadcast_to(brow_ref[...], (tq, tq)), 0, axis=1, stride=1, stride_axis=0)
        r = r * LOG2_E
        ri = lax.broadcasted_iota(jnp.int32, (tq, tq), 0)
        ci = lax.broadcasted_iota(jnp.int32, (tq, tq), 1)
        bd_ref[...] = jnp.where(ri >= ci, r, -jnp.inf)
        bs_ref[...] = jnp.where(ri < ci, r, 0.0)[:LANES, tq - LANES:]

    m1_ref[...] = jnp.full_like(m1_ref, -jnp.inf)
    m2_ref[...] = jnp.full_like(m2_ref, -jnp.inf)
    l1_ref[...] = jnp.zeros_like(l1_ref)
    l2_ref[...] = jnp.zeros_like(l2_ref)
    a1_ref[...] = jnp.zeros_like(a1_ref)
    a2_ref[...] = jnp.zeros_like(a2_ref)

    q1 = q1_ref[...]
    q2 = q2_ref[...]

    def lanes(x, width):
        return jnp.concatenate([x] * (width // LANES), axis=1)

    def softmax_update(load_s, m_ref, l_ref):
        m_old = m_ref[...]
        m_new = jnp.maximum(m_old, jnp.max(load_s(), axis=-1, keepdims=True))
        p = jnp.exp2(load_s() - lanes(m_new, tq))
        alpha = jnp.exp2(m_old - m_new)
        l_ref[...] = alpha * l_ref[...] + jnp.sum(p, axis=-1, keepdims=True)
        m_ref[...] = m_new
        return p.astype(BF16), alpha

    def scores(kj, slot):
        start = pl.multiple_of(kj * tq, tq)
        s_ref[slot, 0] = lax.dot_general(q1, k1_ref[pl.ds(start, tq), :], (((1,), (1,)), ((), ())),
                                         preferred_element_type=F32)
        s_ref[slot, 1] = lax.dot_general(q2, k2_ref[pl.ds(start, tq), :], (((1,), (1,)), ((), ())),
                                         preferred_element_type=F32)

    def consume(kj, slot, bias_ref):
        start = pl.multiple_of(kj * tq, tq)
        vblk = v_ref[pl.ds(start, tq), :]
        for stream, (m_ref, l_ref, a_ref) in enumerate(((m1_ref, l1_ref, a1_ref), (m2_ref, l2_ref, a2_ref))):
            def load_s(stream=stream):
                s = s_ref[slot, stream]
                return s if bias_ref is None else s + bias_ref[...]

            p, alpha = softmax_update(load_s, m_ref, l_ref)
            a_ref[...] = lanes(alpha, dv) * a_ref[...] + jnp.dot(p, vblk, preferred_element_type=F32)

    def add_previous_block_bias(kj, slot):
        flag = jnp.where(kj == qi - 1, 1.0, 0.0)
        corner = flag * bs_ref[...]
        for stream in range(2):
            s_ref[slot, stream, :LANES, tq - LANES:] = s_ref[slot, stream, :LANES, tq - LANES:] + corner

    odd = qi % 2

    @pl.when(odd == 0)
    def _():
        scores(0, 0)

    @pl.when(odd == 1)
    def _():
        scores(0, 1)
        add_previous_block_bias(0, 1)
        scores(1, 0)
        consume(0, 1, None)

    def pair_body(t, carry):
        kj = odd + 2 * t
        scores(kj + 1, 1)
        consume(kj, 0, None)
        add_previous_block_bias(kj + 1, 1)
        scores(kj + 2, 0)
        consume(kj + 1, 1, None)
        return carry

    lax.fori_loop(0, qi // 2, pair_body, 0)
    consume(qi, 0, bd_ref)

    lv = lam_ref[...]
    lam = (jnp.exp(jnp.sum(lv[0:1] * lv[1:2], axis=-1, keepdims=True))
           - jnp.exp(jnp.sum(lv[2:3] * lv[3:4], axis=-1, keepdims=True)) + lam_init)
    o = a1_ref[...] / lanes(l1_ref[...], dv) - lam * (a2_ref[...] / lanes(l2_ref[...], dv))
    o = o * lax.rsqrt(jnp.mean(o * o, axis=-1, keepdims=True) + RMS_EPS_B) * sw_ref[...]
    o = o * (1.0 - lam_init) * _silu(z_ref[...].astype(F32))
    o_ref[...] = o.astype(o_ref.dtype)


def _diff_attention(h, brow, lam_vecs, subln_w, batch, heads, lam_init):
    M = h.shape[0]
    T = M // batch
    tq = min(ATT_TQ, T)
    nq = T // tq
    dh = HEAD_DIM
    dv = 2 * dh
    return pl.pallas_call(
        functools.partial(_attn_kernel, lam_init),
        grid=(batch, heads, nq),
        in_specs=[pl.BlockSpec((tq, dh), lambda b, hh, i: (b * nq + i, 2 * hh)),
                  pl.BlockSpec((tq, dh), lambda b, hh, i: (b * nq + i, 2 * hh + 1)),
                  pl.BlockSpec((T, dh), lambda b, hh, i: (b, 2 * heads + 2 * hh)),
                  pl.BlockSpec((T, dh), lambda b, hh, i: (b, 2 * heads + 2 * hh + 1)),
                  pl.BlockSpec((T, dv), lambda b, hh, i: (b, 2 * heads + hh)),
                  pl.BlockSpec((tq, dv), lambda b, hh, i: (b * nq + i, 3 * heads + hh)),
                  pl.BlockSpec((None, 1, tq), lambda b, hh, i: (hh, 0, 0)),
                  pl.BlockSpec((4, dh), lambda b, hh, i: (0, 0)),
                  pl.BlockSpec((1, dv), lambda b, hh, i: (0, 0))],
        out_specs=pl.BlockSpec((tq, dv), lambda b, hh, i: (b * nq + i, hh)),
        out_shape=jax.ShapeDtypeStruct((M, heads * dv), BF16),
        scratch_shapes=[pltpu.VMEM((tq, LANES), F32), pltpu.VMEM((tq, LANES), F32), pltpu.VMEM((tq, dv), F32),
                        pltpu.VMEM((tq, LANES), F32), pltpu.VMEM((tq, LANES), F32), pltpu.VMEM((tq, dv), F32),
                        pltpu.VMEM((tq, tq), F32), pltpu.VMEM((LANES, LANES), F32),
                        pltpu.VMEM((2, 2, tq, tq), F32)],
        compiler_params=_cparams(("parallel", "parallel", "arbitrary")),
        name="diff_attention",
    )(h, h, h, h, h, h, brow, lam_vecs, subln_w.reshape(1, dv))


def kernel(x, ln_g, ln_b, rel_bias, a_w_in, a_conv_w, a_a_log, a_dt_bias, a_norm_w, a_w_out,
           b_w_in, b_lam_q1, b_lam_k1, b_lam_q2, b_lam_k2, b_subln_w, b_w_out):
    batch, T, D = x.shape
    M = batch * T
    depth = ln_g.shape[0]
    a_heads = a_a_log.shape[1]
    b_heads = rel_bias.shape[1]
    alpha = (2.0 * depth) ** 0.25
    assert T % min(GDN_TB, T) == 0 and T % min(ATT_TQ, T) == 0 and min(ATT_TQ, T) >= LANES

    xf = x.reshape(M, D)
    xb = None
    for i in range(depth):
        j = i // 2
        if i % 2 == 0:
            qkvz_w = 4 * a_heads * HEAD_DIM
            chunk = min(GDN_CHUNK, GDN_TB, T)
            gc, beta, xb = _gates(xf, a_w_in, j, qkvz_w, a_a_log[j], a_dt_bias[j], chunk)
            h = _matmul(xb, _cast_weight(a_w_in, j, qkvz_w, jnp.ones((qkvz_w,), F32)), BF16)
            o = _gated_delta_net(h, a_conv_w[j].T, gc, beta, a_norm_w[j], batch, a_heads)
            w_out = a_w_out
        else:
            tq = min(ATT_TQ, T)
            q_w = 2 * b_heads * HEAD_DIM
            n_in = b_w_in.shape[2]
            col_scale = jnp.where(jnp.arange(n_in) < q_w, HEAD_DIM ** -0.5 * LOG2_E, 1.0)
            h = _matmul(xb, _cast_weight(b_w_in, j, n_in, col_scale), BF16)
            table = rel_bias[_t5_bucket_table(tq)] - rel_bias[REL_BUCKETS - 1][None, :]
            brow = jnp.roll(table[::-1], 1, axis=0).T.reshape(b_heads, 1, tq)
            lam_vecs = jnp.stack([b_lam_q1[j], b_lam_k1[j], b_lam_q2[j], b_lam_k2[j]])
            lam_init = 0.8 - 0.6 * math.exp(-0.3 * i)
            o = _diff_attention(h, brow, lam_vecs, b_subln_w[j], batch, b_heads, lam_init)
            w_out = b_w_out
        w_out_bf = _cast_weight(w_out, j, w_out.shape[2], jnp.ones((w_out.shape[2],), F32))
        r = _matmul_residual(o, w_out_bf, xf, alpha)
        xf, xb = _layer_norm(r, ln_g[i], ln_b[i])
    return xf.reshape(batch, T, D)
```

```python
import functools
import math

import numpy as np
import jax
import jax.numpy as jnp
from jax import lax
from jax.experimental import pallas as pl
from jax.experimental.pallas import tpu as pltpu

F32 = jnp.float32
BF16 = jnp.bfloat16

LANES = 128
SUBLANES = 8
VMEM_LIMIT_BYTES = 56 * 1024 * 1024

CONV_TAPS = 4
HEAD_DIM = 128
LN_EPS = 1e-5
RMS_EPS_A = 1e-6
RMS_EPS_B = 1e-5
L2_EPS = 1e-6
REL_BUCKETS = 32
REL_MAX_DIST = 128
LOG2_E = math.log2(math.e)

MM_BM = 1024
MM_BN = 1024
GDN_CHUNK = 128
GDN_TB = 512
GDN_HEADS_PER_STEP = 4
ATT_TQ = 512
LN_BM = 256
GATES_BM = 512
GATES_W_ROWS = 64
CAST_T_BN = 512
CAST_T_BK = 1024
CAST_BK = 512
CAST_BN = 2048


def _cparams(sem):
    return pltpu.CompilerParams(dimension_semantics=sem, vmem_limit_bytes=VMEM_LIMIT_BYTES)


def _silu(x):
    h = 0.5 * x
    return h + h * jnp.tanh(h)


def _cast_kernel(w_ref, scale_ref, o_ref):
    o_ref[...] = (w_ref[...] * scale_ref[...]).astype(o_ref.dtype)


def _cast_weight(w3, layer, ncols, col_scale):
    K = w3.shape[1]
    bk, bn = min(CAST_BK, K), min(CAST_BN, ncols)
    return pl.pallas_call(
        _cast_kernel,
        grid=(K // bk, ncols // bn),
        in_specs=[pl.BlockSpec((None, bk, bn), lambda i, j: (layer, i, j)),
                  pl.BlockSpec((1, bn), lambda i, j: (0, j))],
        out_specs=pl.BlockSpec((bk, bn), lambda i, j: (i, j)),
        out_shape=jax.ShapeDtypeStruct((K, ncols), BF16),
        compiler_params=_cparams(("parallel", "parallel")),
        name="cast_weight",
    )(w3, col_scale.reshape(1, ncols).astype(F32))


def _cast_t_kernel(w_ref, o_ref):
    o_ref[...] = w_ref[...].T.astype(o_ref.dtype)


def _cast_weight_transposed(wt3, layer, nrows):
    K = wt3.shape[2]
    bn, bk = min(CAST_T_BN, nrows), min(CAST_T_BK, K)
    return pl.pallas_call(
        _cast_t_kernel,
        grid=(nrows // bn, K // bk),
        in_specs=[pl.BlockSpec((None, bn, bk), lambda n, k: (layer, n, k))],
        out_specs=pl.BlockSpec((bk, bn), lambda n, k: (k, n)),
        out_shape=jax.ShapeDtypeStruct((K, nrows), BF16),
        compiler_params=_cparams(("parallel", "parallel")),
        name="cast_weight_t",
    )(wt3)


def _mm_kernel(x_ref, w_ref, o_ref):
    o_ref[...] = jnp.dot(x_ref[...], w_ref[...], preferred_element_type=F32).astype(o_ref.dtype)


def _matmul(x, w, out_dtype):
    M, K = x.shape
    N = w.shape[1]
    bm, bn = min(MM_BM, M), min(MM_BN, N)
    return pl.pallas_call(
        _mm_kernel,
        grid=(M // bm, N // bn),
        in_specs=[pl.BlockSpec((bm, K), lambda i, j: (i, 0)),
                  pl.BlockSpec((K, bn), lambda i, j: (0, j))],
        out_specs=pl.BlockSpec((bm, bn), lambda i, j: (i, j)),
        out_shape=jax.ShapeDtypeStruct((M, N), out_dtype),
        compiler_params=_cparams(("parallel", "parallel")),
        name="in_proj",
    )(x, w)


def _mm_res_kernel(alpha, o_ref, w_ref, x_ref, r_ref):
    y = jnp.dot(o_ref[...], w_ref[...], preferred_element_type=F32)
    r_ref[...] = alpha * x_ref[...] + y


def _matmul_residual(o, w, x, alpha):
    M, K = o.shape
    N = w.shape[1]
    bm, bn = min(MM_BM, M), min(MM_BN, N)
    return pl.pallas_call(
        functools.partial(_mm_res_kernel, alpha),
        grid=(M // bm, N // bn),
        in_specs=[pl.BlockSpec((bm, K), lambda i, j: (i, 0)),
                  pl.BlockSpec((K, bn), lambda i, j: (0, j)),
                  pl.BlockSpec((bm, bn), lambda i, j: (i, j))],
        out_specs=pl.BlockSpec((bm, bn), lambda i, j: (i, j)),
        out_shape=jax.ShapeDtypeStruct((M, N), F32),
        compiler_params=_cparams(("parallel", "parallel")),
        name="out_proj_residual",
    )(o, w, x)


def _ln_kernel(r_ref, g_ref, b_ref, o_ref, ob_ref):
    r = r_ref[...]
    mu = jnp.mean(r, axis=-1, keepdims=True)
    d = r - mu
    var = jnp.mean(d * d, axis=-1, keepdims=True)
    y = d * lax.rsqrt(var + LN_EPS) * g_ref[...] + b_ref[...]
    o_ref[...] = y
    ob_ref[...] = y.astype(BF16)


def _layer_norm(r, g, b):
    M, D = r.shape
    bm = min(LN_BM, M)
    return pl.pallas_call(
        _ln_kernel,
        grid=(M // bm,),
        in_specs=[pl.BlockSpec((bm, D), lambda i: (i, 0)),
                  pl.BlockSpec((1, D), lambda i: (0, 0)),
                  pl.BlockSpec((1, D), lambda i: (0, 0))],
        out_specs=[pl.BlockSpec((bm, D), lambda i: (i, 0)),
                   pl.BlockSpec((bm, D), lambda i: (i, 0))],
        out_shape=[jax.ShapeDtypeStruct((M, D), F32), jax.ShapeDtypeStruct((M, D), BF16)],
        compiler_params=_cparams(("parallel",)),
        name="layer_norm",
    )(r, g.reshape(1, D), b.reshape(1, D))


def _gates_kernel(chunk, x_ref, w_ref, alog_ref, dtb_ref, gc_ref, beta_ref, xb_ref):
    nh = gc_ref.shape[0]
    xb = x_ref[...].astype(BF16)
    xb_ref[...] = xb
    row = lax.broadcasted_iota(jnp.int32, w_ref.shape, 0)
    w = jnp.where(row < 2 * nh, w_ref[...], 0.0).astype(BF16)
    ab = lax.dot_general(w, xb, (((1,), (1,)), ((), ())), preferred_element_type=F32)
    a = ab[:nh] + dtb_ref[...]
    b = ab[nh:2 * nh]
    softplus = jnp.maximum(a, 0.0) + jnp.log1p(jnp.exp(-jnp.abs(a)))
    g = -jnp.exp(alog_ref[...]) * softplus
    pos = lax.broadcasted_iota(jnp.int32, g.shape, 1) % chunk
    s = 1
    while s < chunk:
        g = g + jnp.where(pos >= s, pltpu.roll(g, s, axis=1), 0.0)
        s *= 2
    gc_ref[...] = g
    beta_ref[...] = 1.0 / (1.0 + jnp.exp(-b))


def _gates(x, wt3, layer, row0, a_log, dt_bias, chunk):
    M, K = x.shape
    nh = a_log.shape[0]
    bm = min(GATES_BM, M)
    assert row0 % GATES_W_ROWS == 0 and 2 * nh <= GATES_W_ROWS
    return pl.pallas_call(
        functools.partial(_gates_kernel, chunk),
        grid=(M // bm,),
        in_specs=[pl.BlockSpec((bm, K), lambda i: (i, 0)),
                  pl.BlockSpec((None, GATES_W_ROWS, K), lambda i: (layer, row0 // GATES_W_ROWS, 0)),
                  pl.BlockSpec((nh, 1), lambda i: (0, 0)),
                  pl.BlockSpec((nh, 1), lambda i: (0, 0))],
        out_specs=[pl.BlockSpec((nh, bm), lambda i: (0, i)),
                   pl.BlockSpec((nh, bm), lambda i: (0, i)),
                   pl.BlockSpec((bm, K), lambda i: (i, 0))],
        out_shape=[jax.ShapeDtypeStruct((nh, M), F32), jax.ShapeDtypeStruct((nh, M), F32),
                   jax.ShapeDtypeStruct((M, K), BF16)],
        compiler_params=_cparams(("parallel",)),
        name="gdn_gates",
    )(x, wt3, a_log.reshape(nh, 1), dt_bias.reshape(nh, 1))


def _dot(a, b):
    return jnp.dot(a.astype(BF16), b.astype(BF16), preferred_element_type=F32)


def _dot_nt(a, b):
    return lax.dot_general(a.astype(BF16), b.astype(BF16), (((1,), (1,)), ((), ())),
                           preferred_element_type=F32)


def _gdn_kernel(chunk, q_ref, k_ref, v_ref, z_ref, wq_ref, wk_ref, wv_ref, gc_ref, beta_ref,
                nw_ref, o_ref, s_ref, hq_ref, hk_ref, hv_ref):
    tb = q_ref.shape[0]
    dk = HEAD_DIM
    nheads = q_ref.shape[1] // dk
    halo = hq_ref.shape[0]
    nchunks = tb // chunk

    @pl.when(pl.program_id(2) == 0)
    def _():
        s_ref[...] = jnp.zeros_like(s_ref)
        hq_ref[...] = jnp.zeros_like(hq_ref)
        hk_ref[...] = jnp.zeros_like(hk_ref)
        hv_ref[...] = jnp.zeros_like(hv_ref)

    def conv_silu(x_ref, w_ref, h_ref):
        x = x_ref[...].astype(F32)
        xs = jnp.concatenate([h_ref[...], x], axis=0)
        w = w_ref[...]
        y = x * w[CONV_TAPS - 1:CONV_TAPS]
        for j in range(CONV_TAPS - 1):
            off = halo - (CONV_TAPS - 1) + j
            y = y + xs[off:off + tb] * w[j:j + 1]
        h_ref[...] = x[tb - halo:]
        return _silu(y)

    q_all = conv_silu(q_ref, wq_ref, hq_ref)
    k_all = conv_silu(k_ref, wk_ref, hk_ref)
    v_all = conv_silu(v_ref, wv_ref, hv_ref)

    ri = lax.broadcasted_iota(jnp.int32, (chunk, chunk), 0)
    ci = lax.broadcasted_iota(jnp.int32, (chunk, chunk), 1)
    incl = ri >= ci
    strict = ri > ci
    eye = (ri == ci).astype(F32)
    off_masks = []
    b = 1
    while b < chunk:
        off_masks.append((ri // (2 * b) == ci // (2 * b)) & (ri // b != ci // b))
        b *= 2

    chains = [(g, c) for g in range(nheads) for c in range(nchunks)]
    a_mat, attn, rhs, lhs_top, qd, cdec = {}, {}, {}, {}, {}, {}
    for g in range(nheads):
        hs = slice(g * dk, (g + 1) * dk)
        q = q_all[:, hs]
        k = k_all[:, hs]
        v = v_all[:, hs]
        q = q * (lax.rsqrt(jnp.sum(q * q, axis=-1, keepdims=True) + L2_EPS) * (dk ** -0.5))
        k = k * lax.rsqrt(jnp.sum(k * k, axis=-1, keepdims=True) + L2_EPS)
        gc_row = jnp.broadcast_to(gc_ref[g], (LANES, tb))
        gc_col = gc_row.T
        beta_col = jnp.broadcast_to(beta_ref[g], (LANES, tb)).T
        eg_col = jnp.exp(gc_col)
        for c in range(nchunks):
            sl = slice(c * chunk, (c + 1) * chunk)
            qc, kc, vc = q[sl], k[sl], v[sl]
            bc = beta_col[sl]
            gcc = gc_col[sl]
            kb = kc * bc
            diff = gcc[:, :chunk] - gc_row[:chunk, sl]
            decay = jnp.where(incl, jnp.exp(jnp.where(incl, diff, 0.0)), 0.0)
            qk = _dot_nt(jnp.concatenate([qc, kb], axis=0), kc)
            attn[g, c] = qk[:chunk] * decay
            a_mat[g, c] = jnp.where(strict, qk[chunk:] * decay, 0.0)
            rhs[g, c] = jnp.concatenate([kb * eg_col[sl], vc * bc], axis=1)
            g_last = gcc[chunk - 1:chunk, :]
            lhs_top[g, c] = (kc * jnp.exp(g_last - gcc)).T
            qd[g, c] = qc * eg_col[sl]
            cdec[g, c] = jnp.exp(g_last)

    x = {ch: eye - jnp.where(off_masks[0], a_mat[ch], 0.0) for ch in chains}
    for off_mask in off_masks[1:]:
        y = {ch: _dot(jnp.where(off_mask, a_mat[ch], 0.0), x[ch]) for ch in chains}
        x = {ch: x[ch] - _dot(x[ch], y[ch]) for ch in chains}
    wu = {ch: _dot(x[ch], rhs[ch]) for ch in chains}
    st = {ch: _dot(jnp.concatenate([lhs_top[ch], attn[ch]], axis=0), wu[ch]) for ch in chains}

    s = [s_ref[g] for g in range(nheads)]
    outs = {}
    for c in range(nchunks):
        for g in range(nheads):
            t = st[g, c]
            lhs = jnp.concatenate([-t[:dk, :dk], qd[g, c] - t[dk:, :dk]], axis=0)
            r = _dot(lhs, s[g])
            outs[g, c] = r[dk:] + t[dk:, dk:]
            s[g] = s[g] * cdec[g, c] + r[:dk] + t[:dk, dk:]
    for g in range(nheads):
        s_ref[g] = s[g]

    nw = nw_ref[...]
    cols = []
    for g in range(nheads):
        og = jnp.concatenate([outs[g, c] for c in range(nchunks)], axis=0)
        cols.append(og * lax.rsqrt(jnp.mean(og * og, axis=-1, keepdims=True) + RMS_EPS_A) * nw)
    o = jnp.concatenate(cols, axis=1)
    o_ref[...] = (o * _silu(z_ref[...].astype(F32))).astype(o_ref.dtype)


def _gated_delta_net(h, conv_w_t, gc, beta, norm_w, batch, heads):
    M = h.shape[0]
    T = M // batch
    tb = min(GDN_TB, T)
    nt = T // tb
    dk = HEAD_DIM
    chunk = min(GDN_CHUNK, tb)
    hps = min(GDN_HEADS_PER_STEP, heads)
    ng = heads // hps
    width = hps * dk

    def hspec(off):
        return pl.BlockSpec((tb, width), lambda b, hh, t: (b * nt + t, hh + off))

    def wspec(off):
        return pl.BlockSpec((CONV_TAPS, width), lambda b, hh, t: (0, hh + off))

    gspec = pl.BlockSpec((hps, 1, tb), lambda b, hh, t: (hh, 0, b * nt + t))
    return pl.pallas_call(
        functools.partial(_gdn_kernel, chunk),
        grid=(batch, ng, nt),
        in_specs=[hspec(0), hspec(ng), hspec(2 * ng), hspec(3 * ng),
                  wspec(0), wspec(ng), wspec(2 * ng), gspec, gspec,
                  pl.BlockSpec((1, dk), lambda b, hh, t: (0, 0))],
        out_specs=pl.BlockSpec((tb, width), lambda b, hh, t: (b * nt + t, hh)),
        out_shape=jax.ShapeDtypeStruct((M, heads * dk), BF16),
        scratch_shapes=[pltpu.VMEM((hps, dk, dk), F32),
                        pltpu.VMEM((SUBLANES, width), F32),
                        pltpu.VMEM((SUBLANES, width), F32),
                        pltpu.VMEM((SUBLANES, width), F32)],
        compiler_params=_cparams(("parallel", "parallel", "arbitrary")),
        name="gated_delta_rule",
    )(h, h, h, h, conv_w_t, conv_w_t, conv_w_t,
      gc.reshape(heads, 1, M), beta.reshape(heads, 1, M), norm_w.reshape(1, dk))


def _t5_bucket_table(n):
    rel = np.arange(n)
    max_exact = REL_BUCKETS // 2
    nf = np.maximum(rel, 1).astype(np.float32)
    large = max_exact + (np.log(nf / np.float32(max_exact)) / np.float32(math.log(REL_MAX_DIST / max_exact))
                         * np.float32(REL_BUCKETS - max_exact)).astype(np.int32)
    large = np.minimum(large, REL_BUCKETS - 1)
    return np.where(rel < max_exact, rel, large)


def _attn_kernel(lam_init, q1_ref, q2_ref, k1_ref, k2_ref, v_ref, z_ref, brow_ref, lam_ref, sw_ref,
                 o_ref, m1_ref, l1_ref, a1_ref, m2_ref, l2_ref, a2_ref, bd_ref, bs_ref, s_ref):
    tq = q1_ref.shape[0]
    dv = v_ref.shape[1]
    qi = pl.program_id(2)

    @pl.when(qi == 0)
    def _():
        r = pltpu.roll(jnp.broadcast_to(brow_ref[...], (tq, tq)), 0, axis=1, stride=1, stride_axis=0)
        r = r * LOG2_E
        ri = lax.broadcasted_iota(jnp.int32, (tq, tq), 0)
        ci = lax.broadcasted_iota(jnp.int32, (tq, tq), 1)
        bd_ref[...] = jnp.where(ri >= ci, r, -jnp.inf)
        bs_ref[...] = jnp.where(ri < ci, r, 0.0)[:LANES, tq - LANES:]

    m1_ref[...] = jnp.full_like(m1_ref, -jnp.inf)
    m2_ref[...] = jnp.full_like(m2_ref, -jnp.inf)
    l1_ref[...] = jnp.zeros_like(l1_ref)
    l2_ref[...] = jnp.zeros_like(l2_ref)
    a1_ref[...] = jnp.zeros_like(a1_ref)
    a2_ref[...] = jnp.zeros_like(a2_ref)

    q1 = q1_ref[...]
    q2 = q2_ref[...]

    def lanes(x, width):
        return jnp.concatenate([x] * (width // LANES), axis=1)

    def softmax_update(load_s, m_ref, l_ref):
        m_old = m_ref[...]
        m_new = jnp.maximum(m_old, jnp.max(load_s(), axis=-1, keepdims=True))
        p = jnp.exp2(load_s() - lanes(m_new, tq))
        alpha = jnp.exp2(m_old - m_new)
        l_ref[...] = alpha * l_ref[...] + jnp.sum(p, axis=-1, keepdims=True)
        m_ref[...] = m_new
        return p.astype(BF16), alpha

    def scores(kj, slot):
        start = pl.multiple_of(kj * tq, tq)
        s_ref[slot, 0] = lax.dot_general(q1, k1_ref[pl.ds(start, tq), :], (((1,), (1,)), ((), ())),
                                         preferred_element_type=F32)
        s_ref[slot, 1] = lax.dot_general(q2, k2_ref[pl.ds(start, tq), :], (((1,), (1,)), ((), ())),
                                         preferred_element_type=F32)

    def consume(kj, slot, bias_ref):
        start = pl.multiple_of(kj * tq, tq)
        vblk = v_ref[pl.ds(start, tq), :]
        for stream, (m_ref, l_ref, a_ref) in enumerate(((m1_ref, l1_ref, a1_ref), (m2_ref, l2_ref, a2_ref))):
            def load_s(stream=stream):
                s = s_ref[slot, stream]
                return s if bias_ref is None else s + bias_ref[...]

            p, alpha = softmax_update(load_s, m_ref, l_ref)
            a_ref[...] = lanes(alpha, dv) * a_ref[...] + jnp.dot(p, vblk, preferred_element_type=F32)

    def add_previous_block_bias(kj, slot):
        flag = jnp.where(kj == qi - 1, 1.0, 0.0)
        corner = flag * bs_ref[...]
        for stream in range(2):
            s_ref[slot, stream, :LANES, tq - LANES:] = s_ref[slot, stream, :LANES, tq - LANES:] + corner

    odd = qi % 2

    @pl.when(odd == 0)
    def _():
        scores(0, 0)

    @pl.when(odd == 1)
    def _():
        scores(0, 1)
        add_previous_block_bias(0, 1)
        scores(1, 0)
        consume(0, 1, None)

    def pair_body(t, carry):
        kj = odd + 2 * t
        scores(kj + 1, 1)
        consume(kj, 0, None)
        add_previous_block_bias(kj + 1, 1)
        scores(kj + 2, 0)
        consume(kj + 1, 1, None)
        return carry

    lax.fori_loop(0, qi // 2, pair_body, 0)
    consume(qi, 0, bd_ref)

    lv = lam_ref[...]
    lam = (jnp.exp(jnp.sum(lv[0:1] * lv[1:2], axis=-1, keepdims=True))
           - jnp.exp(jnp.sum(lv[2:3] * lv[3:4], axis=-1, keepdims=True)) + lam_init)
    o = a1_ref[...] / lanes(l1_ref[...], dv) - lam * (a2_ref[...] / lanes(l2_ref[...], dv))
    o = o * lax.rsqrt(jnp.mean(o * o, axis=-1, keepdims=True) + RMS_EPS_B) * sw_ref[...]
    o = o * (1.0 - lam_init) * _silu(z_ref[...].astype(F32))
    o_ref[...] = o.astype(o_ref.dtype)


def _diff_attention(h, brow, lam_vecs, subln_w, batch, heads, lam_init):
    M = h.shape[0]
    T = M // batch
    tq = min(ATT_TQ, T)
    nq = T // tq
    dh = HEAD_DIM
    dv = 2 * dh
    return pl.pallas_call(
        functools.partial(_attn_kernel, lam_init),
        grid=(batch, heads, nq),
        in_specs=[pl.BlockSpec((tq, dh), lambda b, hh, i: (b * nq + i, 2 * hh)),
                  pl.BlockSpec((tq, dh), lambda b, hh, i: (b * nq + i, 2 * hh + 1)),
                  pl.BlockSpec((T, dh), lambda b, hh, i: (b, 2 * heads + 2 * hh)),
                  pl.BlockSpec((T, dh), lambda b, hh, i: (b, 2 * heads + 2 * hh + 1)),
                  pl.BlockSpec((T, dv), lambda b, hh, i: (b, 2 * heads + hh)),
                  pl.BlockSpec((tq, dv), lambda b, hh, i: (b * nq + i, 3 * heads + hh)),
                  pl.BlockSpec((None, 1, tq), lambda b, hh, i: (hh, 0, 0)),
                  pl.BlockSpec((4, dh), lambda b, hh, i: (0, 0)),
                  pl.BlockSpec((1, dv), lambda b, hh, i: (0, 0))],
        out_specs=pl.BlockSpec((tq, dv), lambda b, hh, i: (b * nq + i, hh)),
        out_shape=jax.ShapeDtypeStruct((M, heads * dv), BF16),
        scratch_shapes=[pltpu.VMEM((tq, LANES), F32), pltpu.VMEM((tq, LANES), F32), pltpu.VMEM((tq, dv), F32),
                        pltpu.VMEM((tq, LANES), F32), pltpu.VMEM((tq, LANES), F32), pltpu.VMEM((tq, dv), F32),
                        pltpu.VMEM((tq, tq), F32), pltpu.VMEM((LANES, LANES), F32),
                        pltpu.VMEM((2, 2, tq, tq), F32)],
        compiler_params=_cparams(("parallel", "parallel", "arbitrary")),
        name="diff_attention",
    )(h, h, h, h, h, h, brow, lam_vecs, subln_w.reshape(1, dv))


def kernel(x, ln_g, ln_b, rel_bias, a_w_in, a_conv_w, a_a_log, a_dt_bias, a_norm_w, a_w_out,
           b_w_in, b_lam_q1, b_lam_k1, b_lam_q2, b_lam_k2, b_subln_w, b_w_out):
    batch, T, D = x.shape
    M = batch * T
    depth = ln_g.shape[0]
    a_heads = a_a_log.shape[1]
    b_heads = rel_bias.shape[1]
    alpha = (2.0 * depth) ** 0.25
    assert T % min(GDN_TB, T) == 0 and T % min(ATT_TQ, T) == 0 and min(ATT_TQ, T) >= LANES

    xf = x.reshape(M, D)
    xb = None
    for i in range(depth):
        j = i // 2
        if i % 2 == 0:
            qkvz_w = 4 * a_heads * HEAD_DIM
            chunk = min(GDN_CHUNK, GDN_TB, T)
            a_w_t = jnp.swapaxes(a_w_in, 1, 2)
            gc, beta, xb = _gates(xf, a_w_t, j, qkvz_w, a_a_log[j], a_dt_bias[j], chunk)
            h = _matmul(xb, _cast_weight_transposed(a_w_t, j, qkvz_w), BF16)
            o = _gated_delta_net(h, a_conv_w[j].T, gc, beta, a_norm_w[j], batch, a_heads)
            w_out = a_w_out
        else:
            tq = min(ATT_TQ, T)
            q_w = 2 * b_heads * HEAD_DIM
            n_in = b_w_in.shape[2]
            col_scale = jnp.where(jnp.arange(n_in) < q_w, HEAD_DIM ** -0.5 * LOG2_E, 1.0)
            h = _matmul(xb, _cast_weight(b_w_in, j, n_in, col_scale), BF16)
            table = rel_bias[_t5_bucket_table(tq)] - rel_bias[REL_BUCKETS - 1][None, :]
            brow = jnp.roll(table[::-1], 1, axis=0).T.reshape(b_heads, 1, tq)
            lam_vecs = jnp.stack([b_lam_q1[j], b_lam_k1[j], b_lam_q2[j], b_lam_k2[j]])
            lam_init = 0.8 - 0.6 * math.exp(-0.3 * i)
            o = _diff_attention(h, brow, lam_vecs, b_subln_w[j], batch, b_heads, lam_init)
            w_out = b_w_out
        w_out_bf = _cast_weight(w_out, j, w_out.shape[2], jnp.ones((w_out.shape[2],), F32))
        r = _matmul_residual(o, w_out_bf, xf, alpha)
        xf, xb = _layer_norm(r, ln_g[i], ln_b[i])
    return xf.reshape(batch, T, D)
```

```python
import functools
import math

import numpy as np
import jax
import jax.numpy as jnp
from jax import lax
from jax.experimental import pallas as pl
from jax.experimental.pallas import tpu as pltpu

F32 = jnp.float32
BF16 = jnp.bfloat16

LANES = 128
SUBLANES = 8
BF16_SUBLANES = 16
VMEM_LIMIT_BYTES = 56 * 1024 * 1024

CONV_TAPS = 4
HEAD_DIM = 128
LN_EPS = 1e-5
RMS_EPS_A = 1e-6
RMS_EPS_B = 1e-5
L2_EPS = 1e-6
REL_BUCKETS = 32
REL_MAX_DIST = 128
LOG2_E = math.log2(math.e)

MM_BM = 1024
MM_BN = 1024
GDN_CHUNK = 128
GDN_TB = 1024
GDN_HEADS_PER_STEP = 4
ATT_TQ = 512
LN_BM = 256
GATES_BM = 512
GATES_W_ROWS = 64
CAST_T_BN = 512
CAST_T_BK = 1024


def _cparams(sem):
    return pltpu.CompilerParams(dimension_semantics=sem, vmem_limit_bytes=VMEM_LIMIT_BYTES)


def _silu(x):
    h = 0.5 * x
    return h + h * jnp.tanh(h)


def _cast_t_kernel(w_ref, o_ref):
    o_ref[...] = w_ref[...].T.astype(o_ref.dtype)


def _cast_weight_transposed(wt3, layer, nrows):
    K = wt3.shape[2]
    bn, bk = min(CAST_T_BN, nrows), min(CAST_T_BK, K)
    return pl.pallas_call(
        _cast_t_kernel,
        grid=(nrows // bn, K // bk),
        in_specs=[pl.BlockSpec((None, bn, bk), lambda n, k: (layer, n, k))],
        out_specs=pl.BlockSpec((bk, bn), lambda n, k: (k, n)),
        out_shape=jax.ShapeDtypeStruct((K, nrows), BF16),
        compiler_params=_cparams(("parallel", "parallel")),
        name="cast_weight_t",
    )(wt3)


def _mm_kernel(n_side, x_ref, w_ref, *refs):
    side_in, o_ref, side_out = refs[:2 * n_side], refs[2 * n_side], refs[2 * n_side + 1:]
    o_ref[...] = jnp.dot(x_ref[...], w_ref[...], preferred_element_type=F32).astype(o_ref.dtype)
    for k in range(n_side):
        side_out[k][...] = (side_in[2 * k][...] * side_in[2 * k + 1][...]).astype(side_out[k].dtype)


def _matmul(x, w, out_dtype, side_casts=()):
    M, K = x.shape
    N = w.shape[1]
    bm, bn = min(MM_BM, M), min(MM_BN, N)
    ni, nj = M // bm, N // bn
    side_args, side_in_specs, side_out_specs, side_out_shapes = [], [], [], []
    for w3, layer, col_scale in side_casts:
        ks, ns = w3.shape[1], w3.shape[2]
        rows = ks // (ni * nj)
        assert rows * ni * nj == ks and rows % BF16_SUBLANES == 0
        side_args += [w3, col_scale.reshape(1, ns).astype(F32)]
        side_in_specs += [pl.BlockSpec((None, rows, ns), lambda i, j, layer=layer: (layer, i * nj + j, 0)),
                          pl.BlockSpec((1, ns), lambda i, j: (0, 0))]
        side_out_specs.append(pl.BlockSpec((rows, ns), lambda i, j: (i * nj + j, 0)))
        side_out_shapes.append(jax.ShapeDtypeStruct((ks, ns), BF16))
    out = pl.pallas_call(
        functools.partial(_mm_kernel, len(side_casts)),
        grid=(ni, nj),
        in_specs=[pl.BlockSpec((bm, K), lambda i, j: (i, 0)),
                  pl.BlockSpec((K, bn), lambda i, j: (0, j))] + side_in_specs,
        out_specs=[pl.BlockSpec((bm, bn), lambda i, j: (i, j))] + side_out_specs,
        out_shape=[jax.ShapeDtypeStruct((M, N), out_dtype)] + side_out_shapes,
        compiler_params=_cparams(("parallel", "parallel")),
        name="in_proj",
    )(x, w, *side_args)
    return out[0], out[1:]


def _mm_res_kernel(alpha, o_ref, w_ref, x_ref, r_ref):
    y = jnp.dot(o_ref[...], w_ref[...], preferred_element_type=F32)
    r_ref[...] = alpha * x_ref[...] + y


def _matmul_residual(o, w, x, alpha):
    M, K = o.shape
    N = w.shape[1]
    bm, bn = min(MM_BM, M), min(MM_BN, N)
    return pl.pallas_call(
        functools.partial(_mm_res_kernel, alpha),
        grid=(M // bm, N // bn),
        in_specs=[pl.BlockSpec((bm, K), lambda i, j: (i, 0)),
                  pl.BlockSpec((K, bn), lambda i, j: (0, j)),
                  pl.BlockSpec((bm, bn), lambda i, j: (i, j))],
        out_specs=pl.BlockSpec((bm, bn), lambda i, j: (i, j)),
        out_shape=jax.ShapeDtypeStruct((M, N), F32),
        compiler_params=_cparams(("parallel", "parallel")),
        name="out_proj_residual",
    )(o, w, x)


def _ln_kernel(r_ref, g_ref, b_ref, o_ref, ob_ref):
    r = r_ref[...]
    mu = jnp.mean(r, axis=-1, keepdims=True)
    d = r - mu
    var = jnp.mean(d * d, axis=-1, keepdims=True)
    y = d * lax.rsqrt(var + LN_EPS) * g_ref[...] + b_ref[...]
    o_ref[...] = y
    ob_ref[...] = y.astype(BF16)


def _layer_norm(r, g, b):
    M, D = r.shape
    bm = min(LN_BM, M)
    return pl.pallas_call(
        _ln_kernel,
        grid=(M // bm,),
        in_specs=[pl.BlockSpec((bm, D), lambda i: (i, 0)),
                  pl.BlockSpec((1, D), lambda i: (0, 0)),
                  pl.BlockSpec((1, D), lambda i: (0, 0))],
        out_specs=[pl.BlockSpec((bm, D), lambda i: (i, 0)),
                   pl.BlockSpec((bm, D), lambda i: (i, 0))],
        out_shape=[jax.ShapeDtypeStruct((M, D), F32), jax.ShapeDtypeStruct((M, D), BF16)],
        compiler_params=_cparams(("parallel",)),
        name="layer_norm",
    )(r, g.reshape(1, D), b.reshape(1, D))


def _gates_kernel(chunk, x_ref, w_ref, alog_ref, dtb_ref, gc_ref, beta_ref, xb_ref):
    nh = gc_ref.shape[0]
    xb = x_ref[...].astype(BF16)
    xb_ref[...] = xb
    row = lax.broadcasted_iota(jnp.int32, w_ref.shape, 0)
    w = jnp.where(row < 2 * nh, w_ref[...], 0.0).astype(BF16)
    ab = lax.dot_general(w, xb, (((1,), (1,)), ((), ())), preferred_element_type=F32)
    a = ab[:nh] + dtb_ref[...]
    b = ab[nh:2 * nh]
    softplus = jnp.maximum(a, 0.0) + jnp.log1p(jnp.exp(-jnp.abs(a)))
    g = -jnp.exp(alog_ref[...]) * softplus
    pos = lax.broadcasted_iota(jnp.int32, g.shape, 1) % chunk
    s = 1
    while s < chunk:
        g = g + jnp.where(pos >= s, pltpu.roll(g, s, axis=1), 0.0)
        s *= 2
    gc_ref[...] = g
    beta_ref[...] = 1.0 / (1.0 + jnp.exp(-b))


def _gates(x, wt3, layer, row0, a_log, dt_bias, chunk):
    M, K = x.shape
    nh = a_log.shape[0]
    bm = min(GATES_BM, M)
    assert row0 % GATES_W_ROWS == 0 and 2 * nh <= GATES_W_ROWS
    return pl.pallas_call(
        functools.partial(_gates_kernel, chunk),
        grid=(M // bm,),
        in_specs=[pl.BlockSpec((bm, K), lambda i: (i, 0)),
                  pl.BlockSpec((None, GATES_W_ROWS, K), lambda i: (layer, row0 // GATES_W_ROWS, 0)),
                  pl.BlockSpec((nh, 1), lambda i: (0, 0)),
                  pl.BlockSpec((nh, 1), lambda i: (0, 0))],
        out_specs=[pl.BlockSpec((nh, bm), lambda i: (0, i)),
                   pl.BlockSpec((nh, bm), lambda i: (0, i)),
                   pl.BlockSpec((bm, K), lambda i: (i, 0))],
        out_shape=[jax.ShapeDtypeStruct((nh, M), F32), jax.ShapeDtypeStruct((nh, M), F32),
                   jax.ShapeDtypeStruct((M, K), BF16)],
        compiler_params=_cparams(("parallel",)),
        name="gdn_gates",
    )(x, wt3, a_log.reshape(nh, 1), dt_bias.reshape(nh, 1))


def _dot(a, b):
    return jnp.dot(a.astype(BF16), b.astype(BF16), preferred_element_type=F32)


def _dot_nt(a, b):
    return lax.dot_general(a.astype(BF16), b.astype(BF16), (((1,), (1,)), ((), ())),
                           preferred_element_type=F32)


def _gdn_kernel(chunk, q_ref, k_ref, v_ref, z_ref, wq_ref, wk_ref, wv_ref, gc_ref, beta_ref,
                nw_ref, o_ref, s_ref, hq_ref, hk_ref, hv_ref):
    tb = q_ref.shape[0]
    dk = HEAD_DIM
    nheads = q_ref.shape[1] // dk
    halo = hq_ref.shape[0]
    nchunks = tb // chunk

    @pl.when(pl.program_id(2) == 0)
    def _():
        s_ref[...] = jnp.zeros_like(s_ref)
        hq_ref[...] = jnp.zeros_like(hq_ref)
        hk_ref[...] = jnp.zeros_like(hk_ref)
        hv_ref[...] = jnp.zeros_like(hv_ref)

    def conv_silu(x_ref, w_ref, h_ref):
        x = x_ref[...].astype(F32)
        xs = jnp.concatenate([h_ref[...], x], axis=0)
        w = w_ref[...]
        y = x * w[CONV_TAPS - 1:CONV_TAPS]
        for j in range(CONV_TAPS - 1):
            off = halo - (CONV_TAPS - 1) + j
            y = y + xs[off:off + tb] * w[j:j + 1]
        h_ref[...] = x[tb - halo:]
        return _silu(y)

    q_all = conv_silu(q_ref, wq_ref, hq_ref)
    k_all = conv_silu(k_ref, wk_ref, hk_ref)
    v_all = conv_silu(v_ref, wv_ref, hv_ref)

    ri = lax.broadcasted_iota(jnp.int32, (chunk, chunk), 0)
    ci = lax.broadcasted_iota(jnp.int32, (chunk, chunk), 1)
    incl = ri >= ci
    strict = ri > ci
    eye = (ri == ci).astype(F32)
    off_masks = []
    b = 1
    while b < chunk:
        off_masks.append((ri // (2 * b) == ci // (2 * b)) & (ri // b != ci // b))
        b *= 2

    chains = [(g, c) for g in range(nheads) for c in range(nchunks)]
    a_mat, attn, rhs, lhs_top, qd, cdec = {}, {}, {}, {}, {}, {}
    for g in range(nheads):
        hs = slice(g * dk, (g + 1) * dk)
        q = q_all[:, hs]
        k = k_all[:, hs]
        v = v_all[:, hs]
        q = q * (lax.rsqrt(jnp.sum(q * q, axis=-1, keepdims=True) + L2_EPS) * (dk ** -0.5))
        k = k * lax.rsqrt(jnp.sum(k * k, axis=-1, keepdims=True) + L2_EPS)
        gc_row = jnp.broadcast_to(gc_ref[g], (LANES, tb))
        gc_col = gc_row.T
        beta_col = jnp.broadcast_to(beta_ref[g], (LANES, tb)).T
        eg_col = jnp.exp(gc_col)
        for c in range(nchunks):
            sl = slice(c * chunk, (c + 1) * chunk)
            qc, kc, vc = q[sl], k[sl], v[sl]
            bc = beta_col[sl]
            gcc = gc_col[sl]
            kb = kc * bc
            diff = gcc[:, :chunk] - gc_row[:chunk, sl]
            decay = jnp.where(incl, jnp.exp(jnp.where(incl, diff, 0.0)), 0.0)
            qk = _dot_nt(jnp.concatenate([qc, kb], axis=0), kc)
            attn[g, c] = qk[:chunk] * decay
            a_mat[g, c] = jnp.where(strict, qk[chunk:] * decay, 0.0)
            rhs[g, c] = jnp.concatenate([kb * eg_col[sl], vc * bc], axis=1)
            g_last = gcc[chunk - 1:chunk, :]
            lhs_top[g, c] = (kc * jnp.exp(g_last - gcc)).T
            qd[g, c] = qc * eg_col[sl]
            cdec[g, c] = jnp.exp(g_last)

    x = {ch: eye - jnp.where(off_masks[0], a_mat[ch], 0.0) for ch in chains}
    for off_mask in off_masks[1:]:
        y = {ch: _dot(jnp.where(off_mask, a_mat[ch], 0.0), x[ch]) for ch in chains}
        x = {ch: x[ch] - _dot(x[ch], y[ch]) for ch in chains}
    wu = {ch: _dot(x[ch], rhs[ch]) for ch in chains}
    st = {ch: _dot(jnp.concatenate([lhs_top[ch], attn[ch]], axis=0), wu[ch]) for ch in chains}

    s = [s_ref[g] for g in range(nheads)]
    outs = {}
    for c in range(nchunks):
        for g in range(nheads):
            t = st[g, c]
            lhs = jnp.concatenate([-t[:dk, :dk], qd[g, c] - t[dk:, :dk]], axis=0)
            r = _dot(lhs, s[g])
            outs[g, c] = r[dk:] + t[dk:, dk:]
            s[g] = s[g] * cdec[g, c] + r[:dk] + t[:dk, dk:]
    for g in range(nheads):
        s_ref[g] = s[g]

    nw = nw_ref[...]
    cols = []
    for g in range(nheads):
        og = jnp.concatenate([outs[g, c] for c in range(nchunks)], axis=0)
        cols.append(og * lax.rsqrt(jnp.mean(og * og, axis=-1, keepdims=True) + RMS_EPS_A) * nw)
    o = jnp.concatenate(cols, axis=1)
    o_ref[...] = (o * _silu(z_ref[...].astype(F32))).astype(o_ref.dtype)


def _gated_delta_net(h, conv_w_t, gc, beta, norm_w, batch, heads):
    M = h.shape[0]
    T = M // batch
    tb = min(GDN_TB, T)
    nt = T // tb
    dk = HEAD_DIM
    chunk = min(GDN_CHUNK, tb)
    hps = min(GDN_HEADS_PER_STEP, heads)
    ng = heads // hps
    width = hps * dk

    def hspec(off):
        return pl.BlockSpec((tb, width), lambda b, hh, t: (b * nt + t, hh + off))

    def wspec(off):
        return pl.BlockSpec((CONV_TAPS, width), lambda b, hh, t: (0, hh + off))

    gspec = pl.BlockSpec((hps, 1, tb), lambda b, hh, t: (hh, 0, b * nt + t))
    return pl.pallas_call(
        functools.partial(_gdn_kernel, chunk),
        grid=(batch, ng, nt),
        in_specs=[hspec(0), hspec(ng), hspec(2 * ng), hspec(3 * ng),
                  wspec(0), wspec(ng), wspec(2 * ng), gspec, gspec,
                  pl.BlockSpec((1, dk), lambda b, hh, t: (0, 0))],
        out_specs=pl.BlockSpec((tb, width), lambda b, hh, t: (b * nt + t, hh)),
        out_shape=jax.ShapeDtypeStruct((M, heads * dk), BF16),
        scratch_shapes=[pltpu.VMEM((hps, dk, dk), F32),
                        pltpu.VMEM((SUBLANES, width), F32),
                        pltpu.VMEM((SUBLANES, width), F32),
                        pltpu.VMEM((SUBLANES, width), F32)],
        compiler_params=_cparams(("parallel", "parallel", "arbitrary")),
        name="gated_delta_rule",
    )(h, h, h, h, conv_w_t, conv_w_t, conv_w_t,
      gc.reshape(heads, 1, M), beta.reshape(heads, 1, M), norm_w.reshape(1, dk))


def _t5_bucket_table(n):
    rel = np.arange(n)
    max_exact = REL_BUCKETS // 2
    nf = np.maximum(rel, 1).astype(np.float32)
    large = max_exact + (np.log(nf / np.float32(max_exact)) / np.float32(math.log(REL_MAX_DIST / max_exact))
                         * np.float32(REL_BUCKETS - max_exact)).astype(np.int32)
    large = np.minimum(large, REL_BUCKETS - 1)
    return np.where(rel < max_exact, rel, large)


def _attn_kernel(lam_init, q1_ref, q2_ref, k1_ref, k2_ref, v_ref, z_ref, brow_ref, lam_ref, sw_ref,
                 o_ref, m1_ref, l1_ref, a1_ref, m2_ref, l2_ref, a2_ref, bd_ref, bs_ref, s_ref):
    tq = q1_ref.shape[0]
    dv = v_ref.shape[1]
    qi = pl.program_id(2)

    @pl.when(qi == 0)
    def _():
        r = pltpu.roll(jnp.broadcast_to(brow_ref[...], (tq, tq)), 0, axis=1, stride=1, stride_axis=0)
        r = r * LOG2_E
        ri = lax.broadcasted_iota(jnp.int32, (tq, tq), 0)
        ci = lax.broadcasted_iota(jnp.int32, (tq, tq), 1)
        bd_ref[...] = jnp.where(ri >= ci, r, -jnp.inf)
        bs_ref[...] = jnp.where(ri < ci, r, 0.0)[:LANES, tq - LANES:]

    m1_ref[...] = jnp.full_like(m1_ref, -jnp.inf)
    m2_ref[...] = jnp.full_like(m2_ref, -jnp.inf)
    l1_ref[...] = jnp.zeros_like(l1_ref)
    l2_ref[...] = jnp.zeros_like(l2_ref)
    a1_ref[...] = jnp.zeros_like(a1_ref)
    a2_ref[...] = jnp.zeros_like(a2_ref)

    q1 = q1_ref[...]
    q2 = q2_ref[...]

    def lanes(x, width):
        return jnp.concatenate([x] * (width // LANES), axis=1)

    def softmax_update(load_s, m_ref, l_ref):
        m_old = m_ref[...]
        m_new = jnp.maximum(m_old, jnp.max(load_s(), axis=-1, keepdims=True))
        p = jnp.exp2(load_s() - lanes(m_new, tq))
        alpha = jnp.exp2(m_old - m_new)
        l_ref[...] = alpha * l_ref[...] + jnp.sum(p, axis=-1, keepdims=True)
        m_ref[...] = m_new
        return p.astype(BF16), alpha

    def scores(kj, slot):
        start = pl.multiple_of(kj * tq, tq)
        s_ref[slot, 0] = lax.dot_general(q1, k1_ref[pl.ds(start, tq), :], (((1,), (1,)), ((), ())),
                                         preferred_element_type=F32)
        s_ref[slot, 1] = lax.dot_general(q2, k2_ref[pl.ds(start, tq), :], (((1,), (1,)), ((), ())),
                                         preferred_element_type=F32)

    def consume(kj, slot, bias_ref):
        start = pl.multiple_of(kj * tq, tq)
        vblk = v_ref[pl.ds(start, tq), :]
        for stream, (m_ref, l_ref, a_ref) in enumerate(((m1_ref, l1_ref, a1_ref), (m2_ref, l2_ref, a2_ref))):
            def load_s(stream=stream):
                s = s_ref[slot, stream]
                return s if bias_ref is None else s + bias_ref[...]

            p, alpha = softmax_update(load_s, m_ref, l_ref)
            a_ref[...] = lanes(alpha, dv) * a_ref[...] + jnp.dot(p, vblk, preferred_element_type=F32)

    def add_previous_block_bias(kj, slot):
        flag = jnp.where(kj == qi - 1, 1.0, 0.0)
        corner = flag * bs_ref[...]
        for stream in range(2):
            s_ref[slot, stream, :LANES, tq - LANES:] = s_ref[slot, stream, :LANES, tq - LANES:] + corner

    odd = qi % 2
    scores(0, odd)

    @pl.when(odd == 1)
    def _():
        add_previous_block_bias(0, 1)
        scores(1, 0)
        consume(0, 1, None)

    def pair_body(t, carry):
        kj = odd + 2 * t
        scores(kj + 1, 1)
        consume(kj, 0, None)
        add_previous_block_bias(kj + 1, 1)
        scores(kj + 2, 0)
        consume(kj + 1, 1, None)
        return carry

    lax.fori_loop(0, qi // 2, pair_body, 0)
    consume(qi, 0, bd_ref)

    lv = lam_ref[...]
    lam = (jnp.exp(jnp.sum(lv[0:1] * lv[1:2], axis=-1, keepdims=True))
           - jnp.exp(jnp.sum(lv[2:3] * lv[3:4], axis=-1, keepdims=True)) + lam_init)
    o = a1_ref[...] / lanes(l1_ref[...], dv) - lam * (a2_ref[...] / lanes(l2_ref[...], dv))
    o = o * lax.rsqrt(jnp.mean(o * o, axis=-1, keepdims=True) + RMS_EPS_B) * sw_ref[...]
    o = o * (1.0 - lam_init) * _silu(z_ref[...].astype(F32))
    o_ref[...] = o.astype(o_ref.dtype)


def _diff_attention(h, brow, lam_vecs, subln_w, batch, heads, lam_init):
    M = h.shape[0]
    T = M // batch
    tq = min(ATT_TQ, T)
    nq = T // tq
    dh = HEAD_DIM
    dv = 2 * dh
    return pl.pallas_call(
        functools.partial(_attn_kernel, lam_init),
        grid=(batch, heads, nq),
        in_specs=[pl.BlockSpec((tq, dh), lambda b, hh, i: (b * nq + i, 2 * hh)),
                  pl.BlockSpec((tq, dh), lambda b, hh, i: (b * nq + i, 2 * hh + 1)),
                  pl.BlockSpec((T, dh), lambda b, hh, i: (b, 2 * heads + 2 * hh)),
                  pl.BlockSpec((T, dh), lambda b, hh, i: (b, 2 * heads + 2 * hh + 1)),
                  pl.BlockSpec((T, dv), lambda b, hh, i: (b, 2 * heads + hh)),
                  pl.BlockSpec((tq, dv), lambda b, hh, i: (b * nq + i, 3 * heads + hh)),
                  pl.BlockSpec((None, 1, tq), lambda b, hh, i: (hh, 0, 0)),
                  pl.BlockSpec((4, dh), lambda b, hh, i: (0, 0)),
                  pl.BlockSpec((1, dv), lambda b, hh, i: (0, 0))],
        out_specs=pl.BlockSpec((tq, dv), lambda b, hh, i: (b * nq + i, hh)),
        out_shape=jax.ShapeDtypeStruct((M, heads * dv), BF16),
        scratch_shapes=[pltpu.VMEM((tq, LANES), F32), pltpu.VMEM((tq, LANES), F32), pltpu.VMEM((tq, dv), F32),
                        pltpu.VMEM((tq, LANES), F32), pltpu.VMEM((tq, LANES), F32), pltpu.VMEM((tq, dv), F32),
                        pltpu.VMEM((tq, tq), F32), pltpu.VMEM((LANES, LANES), F32),
                        pltpu.VMEM((2, 2, tq, tq), F32)],
        compiler_params=_cparams(("parallel", "parallel", "arbitrary")),
        name="diff_attention",
    )(h, h, h, h, h, h, brow, lam_vecs, subln_w.reshape(1, dv))


def kernel(x, ln_g, ln_b, rel_bias, a_w_in, a_conv_w, a_a_log, a_dt_bias, a_norm_w, a_w_out,
           b_w_in, b_lam_q1, b_lam_k1, b_lam_q2, b_lam_k2, b_subln_w, b_w_out):
    batch, T, D = x.shape
    M = batch * T
    depth = ln_g.shape[0]
    a_heads = a_a_log.shape[1]
    b_heads = rel_bias.shape[1]
    alpha = (2.0 * depth) ** 0.25
    assert T % min(GDN_TB, T) == 0 and T % min(ATT_TQ, T) == 0 and min(ATT_TQ, T) >= LANES

    q_w = 2 * b_heads * HEAD_DIM
    n_in_b = b_w_in.shape[2]
    b_col_scale = jnp.where(jnp.arange(n_in_b) < q_w, HEAD_DIM ** -0.5 * LOG2_E, 1.0)
    ones_d = jnp.ones((D,), F32)
    cast_jobs = {}
    for i in range(depth):
        if i % 2 == 1:
            cast_jobs["b_in", i // 2] = (b_w_in, i // 2, b_col_scale)
        cast_jobs["out", i] = (a_w_out if i % 2 == 0 else b_w_out, i // 2, ones_d)
    bf16_weights = {}

    xf = x.reshape(M, D)
    xb = None
    for i in range(depth):
        j = i // 2
        if i % 2 == 0:
            qkvz_w = 4 * a_heads * HEAD_DIM
            chunk = min(GDN_CHUNK, GDN_TB, T)
            a_w_t = jnp.swapaxes(a_w_in, 1, 2)
            gc, beta, xb = _gates(xf, a_w_t, j, qkvz_w, a_a_log[j], a_dt_bias[j], chunk)
            jobs = list(cast_jobs.items()) if i == 0 else []
            h, cast = _matmul(xb, _cast_weight_transposed(a_w_t, j, qkvz_w), BF16,
                              [job for _, job in jobs])
            bf16_weights.update({name: wb for (name, _), wb in zip(jobs, cast)})
            o = _gated_delta_net(h, a_conv_w[j].T, gc, beta, a_norm_w[j], batch, a_heads)
        else:
            tq = min(ATT_TQ, T)
            h, _ = _matmul(xb, bf16_weights["b_in", j], BF16)
            table = rel_bias[_t5_bucket_table(tq)] - rel_bias[REL_BUCKETS - 1][None, :]
            brow = jnp.roll(table[::-1], 1, axis=0).T.reshape(b_heads, 1, tq)
            lam_vecs = jnp.stack([b_lam_q1[j], b_lam_k1[j], b_lam_q2[j], b_lam_k2[j]])
            lam_init = 0.8 - 0.6 * math.exp(-0.3 * i)
            o = _diff_attention(h, brow, lam_vecs, b_subln_w[j], batch, b_heads, lam_init)
        r = _matmul_residual(o, bf16_weights["out", i], xf, alpha)
        xf, xb = _layer_norm(r, ln_g[i], ln_b[i])
    return xf.reshape(batch, T, D)
```

```python
import functools
import math

import numpy as np
import jax
import jax.numpy as jnp
from jax import lax
from jax.experimental import pallas as pl
from jax.experimental.pallas import tpu as pltpu

F32 = jnp.float32
BF16 = jnp.bfloat16

LANES = 128
SUBLANES = 8
BF16_SUBLANES = 16
VMEM_LIMIT_BYTES = 56 * 1024 * 1024

CONV_TAPS = 4
HEAD_DIM = 128
LN_EPS = 1e-5
RMS_EPS_A = 1e-6
RMS_EPS_B = 1e-5
L2_EPS = 1e-6
REL_BUCKETS = 32
REL_MAX_DIST = 128
LOG2_E = math.log2(math.e)

MM_BM = 1024
MM_BN = 1024
GDN_CHUNK = 128
GDN_TB = 1024
GDN_HEADS_PER_STEP = 4
ATT_TQ = 512
LN_BM = 256
GATES_BM = 512
GATES_W_ROWS = 64
CAST_T_BN = 1024
CAST_T_BK = 1024


def _cparams(sem):
    return pltpu.CompilerParams(dimension_semantics=sem, vmem_limit_bytes=VMEM_LIMIT_BYTES)


def _silu(x):
    h = 0.5 * x
    return h + h * jnp.tanh(h)


def _cast_t_kernel(w_ref, o_ref):
    o_ref[...] = w_ref[...].T.astype(o_ref.dtype)


def _cast_weight_transposed(wt3, layer, nrows):
    K = wt3.shape[2]
    bn, bk = min(CAST_T_BN, nrows), min(CAST_T_BK, K)
    return pl.pallas_call(
        _cast_t_kernel,
        grid=(nrows // bn, K // bk),
        in_specs=[pl.BlockSpec((None, bn, bk), lambda n, k: (layer, n, k))],
        out_specs=pl.BlockSpec((bk, bn), lambda n, k: (k, n)),
        out_shape=jax.ShapeDtypeStruct((K, nrows), BF16),
        compiler_params=_cparams(("parallel", "parallel")),
        name="cast_weight_t",
    )(wt3)


def _mm_kernel(n_side, x_ref, w_ref, *refs):
    side_in, o_ref, side_out = refs[:2 * n_side], refs[2 * n_side], refs[2 * n_side + 1:]
    res = jnp.dot(x_ref[...], w_ref[...], preferred_element_type=F32).astype(o_ref.dtype)
    if len(o_ref.shape) == 3:
        for c in range(o_ref.shape[0]):
            o_ref[c] = res[:, c * LANES:(c + 1) * LANES]
    else:
        o_ref[...] = res
    for k in range(n_side):
        side_out[k][...] = (side_in[2 * k][...] * side_in[2 * k + 1][...]).astype(side_out[k].dtype)


def _matmul(x, w, out_dtype, side_casts=(), slab_major=False):
    M, K = x.shape
    N = w.shape[1]
    bm, bn = min(MM_BM, M), min(MM_BN, N)
    ni, nj = M // bm, N // bn
    side_args, side_in_specs, side_out_specs, side_out_shapes = [], [], [], []
    for w3, layer, col_scale in side_casts:
        ks, ns = w3.shape[1], w3.shape[2]
        rows = ks // (ni * nj)
        assert rows * ni * nj == ks and rows % BF16_SUBLANES == 0
        side_args += [w3, col_scale.reshape(1, ns).astype(F32)]
        side_in_specs += [pl.BlockSpec((None, rows, ns), lambda i, j, layer=layer: (layer, i * nj + j, 0)),
                          pl.BlockSpec((1, ns), lambda i, j: (0, 0))]
        side_out_specs.append(pl.BlockSpec((rows, ns), lambda i, j: (i * nj + j, 0)))
        side_out_shapes.append(jax.ShapeDtypeStruct((ks, ns), BF16))
    if slab_major:
        main_spec = pl.BlockSpec((bn // LANES, bm, LANES), lambda i, j: (j, i, 0))
        main_shape = jax.ShapeDtypeStruct((N // LANES, M, LANES), out_dtype)
    else:
        main_spec = pl.BlockSpec((bm, bn), lambda i, j: (i, j))
        main_shape = jax.ShapeDtypeStruct((M, N), out_dtype)
    out = pl.pallas_call(
        functools.partial(_mm_kernel, len(side_casts)),
        grid=(ni, nj),
        in_specs=[pl.BlockSpec((bm, K), lambda i, j: (i, 0)),
                  pl.BlockSpec((K, bn), lambda i, j: (0, j))] + side_in_specs,
        out_specs=[main_spec] + side_out_specs,
        out_shape=[main_shape] + side_out_shapes,
        compiler_params=_cparams(("parallel", "parallel")),
        name="in_proj",
    )(x, w, *side_args)
    return out[0], out[1:]


def _mm_res_kernel(alpha, o_ref, w_ref, x_ref, r_ref):
    y = jnp.dot(o_ref[...], w_ref[...], preferred_element_type=F32)
    r_ref[...] = alpha * x_ref[...] + y


def _matmul_residual(o, w, x, alpha):
    M, K = o.shape
    N = w.shape[1]
    bm, bn = min(MM_BM, M), min(MM_BN, N)
    return pl.pallas_call(
        functools.partial(_mm_res_kernel, alpha),
        grid=(M // bm, N // bn),
        in_specs=[pl.BlockSpec((bm, K), lambda i, j: (i, 0)),
                  pl.BlockSpec((K, bn), lambda i, j: (0, j)),
                  pl.BlockSpec((bm, bn), lambda i, j: (i, j))],
        out_specs=pl.BlockSpec((bm, bn), lambda i, j: (i, j)),
        out_shape=jax.ShapeDtypeStruct((M, N), F32),
        compiler_params=_cparams(("parallel", "parallel")),
        name="out_proj_residual",
    )(o, w, x)


def _ln_kernel(r_ref, g_ref, b_ref, o_ref, *ob_ref):
    r = r_ref[...]
    mu = jnp.mean(r, axis=-1, keepdims=True)
    d = r - mu
    var = jnp.mean(d * d, axis=-1, keepdims=True)
    y = d * lax.rsqrt(var + LN_EPS) * g_ref[...] + b_ref[...]
    o_ref[...] = y
    for ref in ob_ref:
        ref[...] = y.astype(ref.dtype)


def _layer_norm(r, g, b, with_bf16):
    M, D = r.shape
    bm = min(LN_BM, M)
    row_spec = pl.BlockSpec((bm, D), lambda i: (i, 0))
    out = pl.pallas_call(
        _ln_kernel,
        grid=(M // bm,),
        in_specs=[row_spec,
                  pl.BlockSpec((1, D), lambda i: (0, 0)),
                  pl.BlockSpec((1, D), lambda i: (0, 0))],
        out_specs=[row_spec] + [row_spec] * with_bf16,
        out_shape=[jax.ShapeDtypeStruct((M, D), F32)] + [jax.ShapeDtypeStruct((M, D), BF16)] * with_bf16,
        compiler_params=_cparams(("parallel",)),
        name="layer_norm",
    )(r, g.reshape(1, D), b.reshape(1, D))
    return out[0], (out[1] if with_bf16 else None)


def _gates_kernel(chunk, x_ref, w_ref, alog_ref, dtb_ref, gc_ref, beta_ref, xb_ref):
    nh = gc_ref.shape[0]
    xb = x_ref[...].astype(BF16)
    xb_ref[...] = xb
    row = lax.broadcasted_iota(jnp.int32, w_ref.shape, 0)
    w = jnp.where(row < 2 * nh, w_ref[...], 0.0).astype(BF16)
    ab = lax.dot_general(w, xb, (((1,), (1,)), ((), ())), preferred_element_type=F32)
    a = ab[:nh] + dtb_ref[...]
    b = ab[nh:2 * nh]
    softplus = jnp.maximum(a, 0.0) + jnp.log1p(jnp.exp(-jnp.abs(a)))
    g = -jnp.exp(alog_ref[...]) * softplus
    pos = lax.broadcasted_iota(jnp.int32, g.shape, 1) % chunk
    s = 1
    while s < chunk:
        g = g + jnp.where(pos >= s, pltpu.roll(g, s, axis=1), 0.0)
        s *= 2
    gc_ref[...] = g
    beta_ref[...] = 1.0 / (1.0 + jnp.exp(-b))


def _gates(x, wt3, layer, row0, a_log, dt_bias, chunk):
    M, K = x.shape
    nh = a_log.shape[0]
    bm = min(GATES_BM, M)
    assert row0 % GATES_W_ROWS == 0 and 2 * nh <= GATES_W_ROWS
    return pl.pallas_call(
        functools.partial(_gates_kernel, chunk),
        grid=(M // bm,),
        in_specs=[pl.BlockSpec((bm, K), lambda i: (i, 0)),
                  pl.BlockSpec((None, GATES_W_ROWS, K), lambda i: (layer, row0 // GATES_W_ROWS, 0)),
                  pl.BlockSpec((nh, 1), lambda i: (0, 0)),
                  pl.BlockSpec((nh, 1), lambda i: (0, 0))],
        out_specs=[pl.BlockSpec((nh, bm), lambda i: (0, i)),
                   pl.BlockSpec((nh, bm), lambda i: (0, i)),
                   pl.BlockSpec((bm, K), lambda i: (i, 0))],
        out_shape=[jax.ShapeDtypeStruct((nh, M), F32), jax.ShapeDtypeStruct((nh, M), F32),
                   jax.ShapeDtypeStruct((M, K), BF16)],
        compiler_params=_cparams(("parallel",)),
        name="gdn_gates",
    )(x, wt3, a_log.reshape(nh, 1), dt_bias.reshape(nh, 1))


def _dot(a, b):
    return jnp.dot(a.astype(BF16), b.astype(BF16), preferred_element_type=F32)


def _dot_nt(a, b):
    return lax.dot_general(a.astype(BF16), b.astype(BF16), (((1,), (1,)), ((), ())),
                           preferred_element_type=F32)


def _gdn_kernel(chunk, q_ref, k_ref, v_ref, z_ref, wq_ref, wk_ref, wv_ref, gc_ref, beta_ref,
                nw_ref, o_ref, s_ref, hq_ref, hk_ref, hv_ref):
    tb = q_ref.shape[0]
    dk = HEAD_DIM
    nheads = q_ref.shape[1] // dk
    halo = hq_ref.shape[0]
    nchunks = tb // chunk

    @pl.when(pl.program_id(2) == 0)
    def _():
        s_ref[...] = jnp.zeros_like(s_ref)
        hq_ref[...] = jnp.zeros_like(hq_ref)
        hk_ref[...] = jnp.zeros_like(hk_ref)
        hv_ref[...] = jnp.zeros_like(hv_ref)

    def conv_silu(x_ref, w_ref, h_ref):
        x = x_ref[...].astype(F32)
        xs = jnp.concatenate([h_ref[...], x], axis=0)
        w = w_ref[...]
        y = x * w[CONV_TAPS - 1:CONV_TAPS]
        for j in range(CONV_TAPS - 1):
            off = halo - (CONV_TAPS - 1) + j
            y = y + xs[off:off + tb] * w[j:j + 1]
        h_ref[...] = x[tb - halo:]
        return _silu(y)

    q_all = conv_silu(q_ref, wq_ref, hq_ref)
    k_all = conv_silu(k_ref, wk_ref, hk_ref)
    v_all = conv_silu(v_ref, wv_ref, hv_ref)

    ri = lax.broadcasted_iota(jnp.int32, (chunk, chunk), 0)
    ci = lax.broadcasted_iota(jnp.int32, (chunk, chunk), 1)
    incl = ri >= ci
    strict = ri > ci
    eye = (ri == ci).astype(F32)
    off_masks = []
    b = 1
    while b < chunk:
        off_masks.append((ri // (2 * b) == ci // (2 * b)) & (ri // b != ci // b))
        b *= 2

    chains = [(g, c) for g in range(nheads) for c in range(nchunks)]
    a_mat, attn, rhs, lhs_top, qd, cdec = {}, {}, {}, {}, {}, {}
    for g in range(nheads):
        hs = slice(g * dk, (g + 1) * dk)
        q = q_all[:, hs]
        k = k_all[:, hs]
        v = v_all[:, hs]
        q = q * (lax.rsqrt(jnp.sum(q * q, axis=-1, keepdims=True) + L2_EPS) * (dk ** -0.5))
        k = k * lax.rsqrt(jnp.sum(k * k, axis=-1, keepdims=True) + L2_EPS)
        gc_row = jnp.broadcast_to(gc_ref[g], (LANES, tb))
        gc_col = gc_row.T
        beta_col = jnp.broadcast_to(beta_ref[g], (LANES, tb)).T
        eg_col = jnp.exp(gc_col)
        for c in range(nchunks):
            sl = slice(c * chunk, (c + 1) * chunk)
            qc, kc, vc = q[sl], k[sl], v[sl]
            bc = beta_col[sl]
            gcc = gc_col[sl]
            kb = kc * bc
            diff = gcc[:, :chunk] - gc_row[:chunk, sl]
            decay = jnp.where(incl, jnp.exp(jnp.where(incl, diff, 0.0)), 0.0)
            qk = _dot_nt(jnp.concatenate([qc, kb], axis=0), kc)
            attn[g, c] = qk[:chunk] * decay
            a_mat[g, c] = jnp.where(strict, qk[chunk:] * decay, 0.0)
            rhs[g, c] = jnp.concatenate([kb * eg_col[sl], vc * bc], axis=1)
            g_last = gcc[chunk - 1:chunk, :]
            lhs_top[g, c] = (kc * jnp.exp(g_last - gcc)).T
            qd[g, c] = qc * eg_col[sl]
            cdec[g, c] = jnp.exp(g_last)

    x = {ch: eye - jnp.where(off_masks[0], a_mat[ch], 0.0) for ch in chains}
    for off_mask in off_masks[1:]:
        y = {ch: _dot(jnp.where(off_mask, a_mat[ch], 0.0), x[ch]) for ch in chains}
        x = {ch: x[ch] - _dot(x[ch], y[ch]) for ch in chains}
    wu = {ch: _dot(x[ch], rhs[ch]) for ch in chains}
    st = {ch: _dot(jnp.concatenate([lhs_top[ch], attn[ch]], axis=0), wu[ch]) for ch in chains}

    s = [s_ref[g] for g in range(nheads)]
    outs = {}
    for c in range(nchunks):
        for g in range(nheads):
            t = st[g, c]
            lhs = jnp.concatenate([-t[:dk, :dk], qd[g, c] - t[dk:, :dk]], axis=0)
            r = _dot(lhs, s[g])
            outs[g, c] = r[dk:] + t[dk:, dk:]
            s[g] = s[g] * cdec[g, c] + r[:dk] + t[:dk, dk:]
    for g in range(nheads):
        s_ref[g] = s[g]

    nw = nw_ref[...]
    cols = []
    for g in range(nheads):
        og = jnp.concatenate([outs[g, c] for c in range(nchunks)], axis=0)
        cols.append(og * lax.rsqrt(jnp.mean(og * og, axis=-1, keepdims=True) + RMS_EPS_A) * nw)
    o = jnp.concatenate(cols, axis=1)
    o_ref[...] = (o * _silu(z_ref[...].astype(F32))).astype(o_ref.dtype)


def _gated_delta_net(h, conv_w_t, gc, beta, norm_w, batch, heads):
    M = h.shape[0]
    T = M // batch
    tb = min(GDN_TB, T)
    nt = T // tb
    dk = HEAD_DIM
    chunk = min(GDN_CHUNK, tb)
    hps = min(GDN_HEADS_PER_STEP, heads)
    ng = heads // hps
    width = hps * dk

    def hspec(off):
        return pl.BlockSpec((tb, width), lambda b, hh, t: (b * nt + t, hh + off))

    def wspec(off):
        return pl.BlockSpec((CONV_TAPS, width), lambda b, hh, t: (0, hh + off))

    gspec = pl.BlockSpec((hps, 1, tb), lambda b, hh, t: (hh, 0, b * nt + t))
    return pl.pallas_call(
        functools.partial(_gdn_kernel, chunk),
        grid=(batch, ng, nt),
        in_specs=[hspec(0), hspec(ng), hspec(2 * ng), hspec(3 * ng),
                  wspec(0), wspec(ng), wspec(2 * ng), gspec, gspec,
                  pl.BlockSpec((1, dk), lambda b, hh, t: (0, 0))],
        out_specs=pl.BlockSpec((tb, width), lambda b, hh, t: (b * nt + t, hh)),
        out_shape=jax.ShapeDtypeStruct((M, heads * dk), BF16),
        scratch_shapes=[pltpu.VMEM((hps, dk, dk), F32),
                        pltpu.VMEM((SUBLANES, width), F32),
                        pltpu.VMEM((SUBLANES, width), F32),
                        pltpu.VMEM((SUBLANES, width), F32)],
        compiler_params=_cparams(("parallel", "parallel", "arbitrary")),
        name="gated_delta_rule",
    )(h, h, h, h, conv_w_t, conv_w_t, conv_w_t,
      gc.reshape(heads, 1, M), beta.reshape(heads, 1, M), norm_w.reshape(1, dk))


def _t5_bucket_table(n):
    rel = np.arange(n)
    max_exact = REL_BUCKETS // 2
    nf = np.maximum(rel, 1).astype(np.float32)
    large = max_exact + (np.log(nf / np.float32(max_exact)) / np.float32(math.log(REL_MAX_DIST / max_exact))
                         * np.float32(REL_BUCKETS - max_exact)).astype(np.int32)
    large = np.minimum(large, REL_BUCKETS - 1)
    return np.where(rel < max_exact, rel, large)


def _attn_kernel(lam_init, q1_ref, q2_ref, k1_ref, k2_ref, v_ref, z_ref, brow_ref, lam_ref, sw_ref,
                 o_ref, m1_ref, l1_ref, a1_ref, m2_ref, l2_ref, a2_ref, bd_ref, bs_ref, s_ref):
    tq = q1_ref.shape[0]
    dv = v_ref.shape[0] * LANES
    qi = pl.program_id(2)

    def slabs(ref, rows):
        return jnp.concatenate([ref[c, rows, :] for c in range(ref.shape[0])], axis=1)

    @pl.when(qi == 0)
    def _():
        r = pltpu.roll(jnp.broadcast_to(brow_ref[...], (tq, tq)), 0, axis=1, stride=1, stride_axis=0)
        r = r * LOG2_E
        ri = lax.broadcasted_iota(jnp.int32, (tq, tq), 0)
        ci = lax.broadcasted_iota(jnp.int32, (tq, tq), 1)
        bd_ref[...] = jnp.where(ri >= ci, r, -jnp.inf)
        bs_ref[...] = jnp.where(ri < ci, r, 0.0)[:LANES, tq - LANES:]

    m1_ref[...] = jnp.full_like(m1_ref, -jnp.inf)
    m2_ref[...] = jnp.full_like(m2_ref, -jnp.inf)
    l1_ref[...] = jnp.zeros_like(l1_ref)
    l2_ref[...] = jnp.zeros_like(l2_ref)
    a1_ref[...] = jnp.zeros_like(a1_ref)
    a2_ref[...] = jnp.zeros_like(a2_ref)

    q1 = q1_ref[...]
    q2 = q2_ref[...]

    def lanes(x, width):
        return jnp.concatenate([x] * (width // LANES), axis=1)

    def softmax_update(load_s, m_ref, l_ref):
        m_old = m_ref[...]
        m_new = jnp.maximum(m_old, jnp.max(load_s(), axis=-1, keepdims=True))
        p = jnp.exp2(load_s() - lanes(m_new, tq))
        alpha = jnp.exp2(m_old - m_new)
        l_ref[...] = alpha * l_ref[...] + jnp.sum(p, axis=-1, keepdims=True)
        m_ref[...] = m_new
        return p.astype(BF16), alpha

    def scores(kj, slot):
        start = pl.multiple_of(kj * tq, tq)
        s_ref[slot, 0] = lax.dot_general(q1, k1_ref[pl.ds(start, tq), :], (((1,), (1,)), ((), ())),
                                         preferred_element_type=F32)
        s_ref[slot, 1] = lax.dot_general(q2, k2_ref[pl.ds(start, tq), :], (((1,), (1,)), ((), ())),
                                         preferred_element_type=F32)

    def consume(kj, slot, bias_ref):
        start = pl.multiple_of(kj * tq, tq)
        vblk = slabs(v_ref, pl.ds(start, tq))
        for stream, (m_ref, l_ref, a_ref) in enumerate(((m1_ref, l1_ref, a1_ref), (m2_ref, l2_ref, a2_ref))):
            def load_s(stream=stream):
                s = s_ref[slot, stream]
                return s if bias_ref is None else s + bias_ref[...]

            p, alpha = softmax_update(load_s, m_ref, l_ref)
            a_ref[...] = lanes(alpha, dv) * a_ref[...] + jnp.dot(p, vblk, preferred_element_type=F32)

    def add_previous_block_bias(kj, slot):
        flag = jnp.where(kj == qi - 1, 1.0, 0.0)
        corner = flag * bs_ref[...]
        for stream in range(2):
            s_ref[slot, stream, :LANES, tq - LANES:] = s_ref[slot, stream, :LANES, tq - LANES:] + corner

    odd = qi % 2
    scores(0, odd)

    @pl.when(odd == 1)
    def _():
        add_previous_block_bias(0, 1)
        scores(1, 0)
        consume(0, 1, None)

    def pair_body(t, carry):
        kj = odd + 2 * t
        scores(kj + 1, 1)
        consume(kj, 0, None)
        add_previous_block_bias(kj + 1, 1)
        scores(kj + 2, 0)
        consume(kj + 1, 1, None)
        return carry

    lax.fori_loop(0, qi // 2, pair_body, 0)
    consume(qi, 0, bd_ref)

    lv = lam_ref[...]
    lam = (jnp.exp(jnp.sum(lv[0:1] * lv[1:2], axis=-1, keepdims=True))
           - jnp.exp(jnp.sum(lv[2:3] * lv[3:4], axis=-1, keepdims=True)) + lam_init)
    o = a1_ref[...] / lanes(l1_ref[...], dv) - lam * (a2_ref[...] / lanes(l2_ref[...], dv))
    o = o * lax.rsqrt(jnp.mean(o * o, axis=-1, keepdims=True) + RMS_EPS_B) * sw_ref[...]
    o = o * (1.0 - lam_init) * _silu(slabs(z_ref, slice(None)).astype(F32))
    o_ref[...] = o.astype(o_ref.dtype)


def _diff_attention(h, brow, lam_vecs, subln_w, batch, heads, lam_init):
    M = h.shape[1]
    T = M // batch
    tq = min(ATT_TQ, T)
    nq = T // tq
    dh = HEAD_DIM
    dv = 2 * dh
    return pl.pallas_call(
        functools.partial(_attn_kernel, lam_init),
        grid=(batch, heads, nq),
        in_specs=[pl.BlockSpec((None, tq, dh), lambda b, hh, i: (2 * hh, b * nq + i, 0)),
                  pl.BlockSpec((None, tq, dh), lambda b, hh, i: (2 * hh + 1, b * nq + i, 0)),
                  pl.BlockSpec((None, T, dh), lambda b, hh, i: (2 * heads + 2 * hh, b, 0)),
                  pl.BlockSpec((None, T, dh), lambda b, hh, i: (2 * heads + 2 * hh + 1, b, 0)),
                  pl.BlockSpec((2, T, dh), lambda b, hh, i: (2 * heads + hh, b, 0)),
                  pl.BlockSpec((2, tq, dh), lambda b, hh, i: (3 * heads + hh, b * nq + i, 0)),
                  pl.BlockSpec((None, 1, tq), lambda b, hh, i: (hh, 0, 0)),
                  pl.BlockSpec((4, dh), lambda b, hh, i: (0, 0)),
                  pl.BlockSpec((1, dv), lambda b, hh, i: (0, 0))],
        out_specs=pl.BlockSpec((tq, dv), lambda b, hh, i: (b * nq + i, hh)),
        out_shape=jax.ShapeDtypeStruct((M, heads * dv), BF16),
        scratch_shapes=[pltpu.VMEM((tq, LANES), F32), pltpu.VMEM((tq, LANES), F32), pltpu.VMEM((tq, dv), F32),
                        pltpu.VMEM((tq, LANES), F32), pltpu.VMEM((tq, LANES), F32), pltpu.VMEM((tq, dv), F32),
                        pltpu.VMEM((tq, tq), F32), pltpu.VMEM((LANES, LANES), F32),
                        pltpu.VMEM((2, 2, tq, tq), F32)],
        compiler_params=_cparams(("parallel", "parallel", "arbitrary")),
        name="diff_attention",
    )(h, h, h, h, h, h, brow, lam_vecs, subln_w.reshape(1, dv))


def kernel(x, ln_g, ln_b, rel_bias, a_w_in, a_conv_w, a_a_log, a_dt_bias, a_norm_w, a_w_out,
           b_w_in, b_lam_q1, b_lam_k1, b_lam_q2, b_lam_k2, b_subln_w, b_w_out):
    batch, T, D = x.shape
    M = batch * T
    depth = ln_g.shape[0]
    a_heads = a_a_log.shape[1]
    b_heads = rel_bias.shape[1]
    alpha = (2.0 * depth) ** 0.25
    assert T % min(GDN_TB, T) == 0 and T % min(ATT_TQ, T) == 0 and min(ATT_TQ, T) >= LANES

    q_w = 2 * b_heads * HEAD_DIM
    n_in_b = b_w_in.shape[2]
    b_col_scale = jnp.where(jnp.arange(n_in_b) < q_w, HEAD_DIM ** -0.5 * LOG2_E, 1.0)
    ones_d = jnp.ones((D,), F32)
    cast_jobs = {}
    for i in range(depth):
        if i % 2 == 1:
            cast_jobs["b_in", i // 2] = (b_w_in, i // 2, b_col_scale)
        cast_jobs["out", i] = (a_w_out if i % 2 == 0 else b_w_out, i // 2, ones_d)
    bf16_weights = {}

    xf = x.reshape(M, D)
    xb = None
    for i in range(depth):
        j = i // 2
        if i % 2 == 0:
            qkvz_w = 4 * a_heads * HEAD_DIM
            chunk = min(GDN_CHUNK, GDN_TB, T)
            a_w_t = jnp.swapaxes(a_w_in, 1, 2)
            gc, beta, xb = _gates(xf, a_w_t, j, qkvz_w, a_a_log[j], a_dt_bias[j], chunk)
            jobs = list(cast_jobs.items()) if i == 0 else []
            h, cast = _matmul(xb, _cast_weight_transposed(a_w_t, j, qkvz_w), BF16,
                              [job for _, job in jobs])
            bf16_weights.update({name: wb for (name, _), wb in zip(jobs, cast)})
            o = _gated_delta_net(h, a_conv_w[j].T, gc, beta, a_norm_w[j], batch, a_heads)
        else:
            tq = min(ATT_TQ, T)
            h, _ = _matmul(xb, bf16_weights["b_in", j], BF16, slab_major=True)
            table = rel_bias[_t5_bucket_table(tq)] - rel_bias[REL_BUCKETS - 1][None, :]
            brow = jnp.roll(table[::-1], 1, axis=0).T.reshape(b_heads, 1, tq)
            lam_vecs = jnp.stack([b_lam_q1[j], b_lam_k1[j], b_lam_q2[j], b_lam_k2[j]])
            lam_init = 0.8 - 0.6 * math.exp(-0.3 * i)
            o = _diff_attention(h, brow, lam_vecs, b_subln_w[j], batch, b_heads, lam_init)
        r = _matmul_residual(o, bf16_weights["out", i], xf, alpha)
        xf, xb = _layer_norm(r, ln_g[i], ln_b[i], with_bf16=i + 1 < depth)
    return xf.reshape(batch, T, D)
```

```python
import functools
import math

import numpy as np
import jax
import jax.numpy as jnp
from jax import lax
from jax.experimental import pallas as pl
from jax.experimental.pallas import tpu as pltpu

F32 = jnp.float32
BF16 = jnp.bfloat16

LANES = 128
SUBLANES = 8
BF16_SUBLANES = 16
VMEM_LIMIT_BYTES = 56 * 1024 * 1024

CONV_TAPS = 4
HEAD_DIM = 128
LN_EPS = 1e-5
RMS_EPS_A = 1e-6
RMS_EPS_B = 1e-5
L2_EPS = 1e-6
REL_BUCKETS = 32
REL_MAX_DIST = 128
LOG2_E = math.log2(math.e)

MM_BM = 1024
MM_BN = 1024
GDN_CHUNK = 128
GDN_TB = 1024
GDN_HEADS_PER_STEP = 4
ATT_TQ = 512
LN_BM = 256
GATES_BM = 512
GATES_W_ROWS = 64
CAST_T_BN = 1024
CAST_T_BK = 1024


def _cparams(sem):
    return pltpu.CompilerParams(dimension_semantics=sem, vmem_limit_bytes=VMEM_LIMIT_BYTES)


def _silu(x):
    h = 0.5 * x
    return h + h * jnp.tanh(h)


def _cast_t_kernel(w_ref, o_ref):
    o_ref[...] = w_ref[...].T.astype(o_ref.dtype)


def _cast_weight_transposed(wt3, layer, nrows):
    K = wt3.shape[2]
    bn, bk = min(CAST_T_BN, nrows), min(CAST_T_BK, K)
    return pl.pallas_call(
        _cast_t_kernel,
        grid=(nrows // bn, K // bk),
        in_specs=[pl.BlockSpec((None, bn, bk), lambda n, k: (layer, n, k))],
        out_specs=pl.BlockSpec((bk, bn), lambda n, k: (k, n)),
        out_shape=jax.ShapeDtypeStruct((K, nrows), BF16),
        compiler_params=_cparams(("parallel", "parallel")),
        name="cast_weight_t",
    )(wt3)


def _mm_kernel(n_side, x_ref, w_ref, *refs):
    side_in, o_ref, side_out = refs[:2 * n_side], refs[2 * n_side], refs[2 * n_side + 1:]
    res = jnp.dot(x_ref[...], w_ref[...], preferred_element_type=F32).astype(o_ref.dtype)
    if len(o_ref.shape) == 3:
        for c in range(o_ref.shape[0]):
            o_ref[c] = res[:, c * LANES:(c + 1) * LANES]
    else:
        o_ref[...] = res
    for k in range(n_side):
        side_out[k][...] = (side_in[2 * k][...] * side_in[2 * k + 1][...]).astype(side_out[k].dtype)


def _matmul(x, w, out_dtype, side_casts=(), slab_major=False):
    M, K = x.shape
    N = w.shape[1]
    bm, bn = min(MM_BM, M), min(MM_BN, N)
    ni, nj = M // bm, N // bn
    side_args, side_in_specs, side_out_specs, side_out_shapes = [], [], [], []
    for w3, layer, col_scale in side_casts:
        ks, ns = w3.shape[1], w3.shape[2]
        rows = ks // (ni * nj)
        assert rows * ni * nj == ks and rows % BF16_SUBLANES == 0
        side_args += [w3, col_scale.reshape(1, ns).astype(F32)]
        side_in_specs += [pl.BlockSpec((None, rows, ns), lambda i, j, layer=layer: (layer, i * nj + j, 0)),
                          pl.BlockSpec((1, ns), lambda i, j: (0, 0))]
        side_out_specs.append(pl.BlockSpec((rows, ns), lambda i, j: (i * nj + j, 0)))
        side_out_shapes.append(jax.ShapeDtypeStruct((ks, ns), BF16))
    if slab_major:
        main_spec = pl.BlockSpec((bn // LANES, bm, LANES), lambda i, j: (j, i, 0))
        main_shape = jax.ShapeDtypeStruct((N // LANES, M, LANES), out_dtype)
    else:
        main_spec = pl.BlockSpec((bm, bn), lambda i, j: (i, j))
        main_shape = jax.ShapeDtypeStruct((M, N), out_dtype)
    out = pl.pallas_call(
        functools.partial(_mm_kernel, len(side_casts)),
        grid=(ni, nj),
        in_specs=[pl.BlockSpec((bm, K), lambda i, j: (i, 0)),
                  pl.BlockSpec((K, bn), lambda i, j: (0, j))] + side_in_specs,
        out_specs=[main_spec] + side_out_specs,
        out_shape=[main_shape] + side_out_shapes,
        compiler_params=_cparams(("parallel", "parallel")),
        name="in_proj",
    )(x, w, *side_args)
    return out[0], out[1:]


def _mm_res_kernel(alpha, o_ref, w_ref, x_ref, r_ref):
    y = jnp.dot(o_ref[...], w_ref[...], preferred_element_type=F32)
    r_ref[...] = alpha * x_ref[...] + y


def _matmul_residual(o, w, x, alpha):
    M, K = o.shape
    N = w.shape[1]
    bm, bn = min(MM_BM, M), min(MM_BN, N)
    return pl.pallas_call(
        functools.partial(_mm_res_kernel, alpha),
        grid=(M // bm, N // bn),
        in_specs=[pl.BlockSpec((bm, K), lambda i, j: (i, 0)),
                  pl.BlockSpec((K, bn), lambda i, j: (0, j)),
                  pl.BlockSpec((bm, bn), lambda i, j: (i, j))],
        out_specs=pl.BlockSpec((bm, bn), lambda i, j: (i, j)),
        out_shape=jax.ShapeDtypeStruct((M, N), F32),
        compiler_params=_cparams(("parallel", "parallel")),
        name="out_proj_residual",
    )(o, w, x)


def _ln_kernel(r_ref, g_ref, b_ref, o_ref, *ob_ref):
    r = r_ref[...]
    mu = jnp.mean(r, axis=-1, keepdims=True)
    d = r - mu
    var = jnp.mean(d * d, axis=-1, keepdims=True)
    y = d * lax.rsqrt(var + LN_EPS) * g_ref[...] + b_ref[...]
    o_ref[...] = y
    for ref in ob_ref:
        ref[...] = y.astype(ref.dtype)


def _layer_norm(r, g, b, with_bf16):
    M, D = r.shape
    bm = min(LN_BM, M)
    row_spec = pl.BlockSpec((bm, D), lambda i: (i, 0))
    out = pl.pallas_call(
        _ln_kernel,
        grid=(M // bm,),
        in_specs=[row_spec,
                  pl.BlockSpec((1, D), lambda i: (0, 0)),
                  pl.BlockSpec((1, D), lambda i: (0, 0))],
        out_specs=[row_spec] + [row_spec] * with_bf16,
        out_shape=[jax.ShapeDtypeStruct((M, D), F32)] + [jax.ShapeDtypeStruct((M, D), BF16)] * with_bf16,
        compiler_params=_cparams(("parallel",)),
        name="layer_norm",
    )(r, g.reshape(1, D), b.reshape(1, D))
    return out[0], (out[1] if with_bf16 else None)


def _gates_kernel(chunk, x_ref, w_ref, alog_ref, dtb_ref, gc_ref, beta_ref, xb_ref):
    nh = gc_ref.shape[0]
    xb = x_ref[...].astype(BF16)
    xb_ref[...] = xb
    row = lax.broadcasted_iota(jnp.int32, w_ref.shape, 0)
    w = jnp.where(row < 2 * nh, w_ref[...], 0.0).astype(BF16)
    ab = lax.dot_general(w, xb, (((1,), (1,)), ((), ())), preferred_element_type=F32)
    a = ab[:nh] + dtb_ref[...]
    b = ab[nh:2 * nh]
    softplus = jnp.maximum(a, 0.0) + jnp.log1p(jnp.exp(-jnp.abs(a)))
    g = -jnp.exp(alog_ref[...]) * softplus
    pos = lax.broadcasted_iota(jnp.int32, g.shape, 1) % chunk
    s = 1
    while s < chunk:
        g = g + jnp.where(pos >= s, pltpu.roll(g, s, axis=1), 0.0)
        s *= 2
    gc_ref[...] = g
    beta_ref[...] = 1.0 / (1.0 + jnp.exp(-b))


def _gates(x, wt3, layer, row0, a_log, dt_bias, chunk):
    M, K = x.shape
    nh = a_log.shape[0]
    bm = min(GATES_BM, M)
    assert row0 % GATES_W_ROWS == 0 and 2 * nh <= GATES_W_ROWS
    return pl.pallas_call(
        functools.partial(_gates_kernel, chunk),
        grid=(M // bm,),
        in_specs=[pl.BlockSpec((bm, K), lambda i: (i, 0)),
                  pl.BlockSpec((None, GATES_W_ROWS, K), lambda i: (layer, row0 // GATES_W_ROWS, 0)),
                  pl.BlockSpec((nh, 1), lambda i: (0, 0)),
                  pl.BlockSpec((nh, 1), lambda i: (0, 0))],
        out_specs=[pl.BlockSpec((nh, bm), lambda i: (0, i)),
                   pl.BlockSpec((nh, bm), lambda i: (0, i)),
                   pl.BlockSpec((bm, K), lambda i: (i, 0))],
        out_shape=[jax.ShapeDtypeStruct((nh, M), F32), jax.ShapeDtypeStruct((nh, M), F32),
                   jax.ShapeDtypeStruct((M, K), BF16)],
        compiler_params=_cparams(("parallel",)),
        name="gdn_gates",
    )(x, wt3, a_log.reshape(nh, 1), dt_bias.reshape(nh, 1))


def _dot(a, b):
    return jnp.dot(a.astype(BF16), b.astype(BF16), preferred_element_type=F32)


def _dot_nt(a, b):
    return lax.dot_general(a.astype(BF16), b.astype(BF16), (((1,), (1,)), ((), ())),
                           preferred_element_type=F32)


def _gdn_kernel(chunk, q_ref, k_ref, v_ref, z_ref, wq_ref, wk_ref, wv_ref, gc_ref, beta_ref,
                nw_ref, o_ref, s_ref, hq_ref, hk_ref, hv_ref):
    tb = q_ref.shape[0]
    dk = HEAD_DIM
    nheads = q_ref.shape[1] // dk
    halo = hq_ref.shape[0]
    nchunks = tb // chunk

    @pl.when(pl.program_id(2) == 0)
    def _():
        s_ref[...] = jnp.zeros_like(s_ref)
        hq_ref[...] = jnp.zeros_like(hq_ref)
        hk_ref[...] = jnp.zeros_like(hk_ref)
        hv_ref[...] = jnp.zeros_like(hv_ref)

    def conv_silu(x_ref, w_ref, h_ref):
        x = x_ref[...].astype(F32)
        xs = jnp.concatenate([h_ref[...], x], axis=0)
        w = w_ref[...]
        y = x * w[CONV_TAPS - 1:CONV_TAPS]
        for j in range(CONV_TAPS - 1):
            off = halo - (CONV_TAPS - 1) + j
            y = y + xs[off:off + tb] * w[j:j + 1]
        h_ref[...] = x[tb - halo:]
        return _silu(y)

    q_all = conv_silu(q_ref, wq_ref, hq_ref)
    k_all = conv_silu(k_ref, wk_ref, hk_ref)
    v_all = conv_silu(v_ref, wv_ref, hv_ref)

    ri = lax.broadcasted_iota(jnp.int32, (chunk, chunk), 0)
    ci = lax.broadcasted_iota(jnp.int32, (chunk, chunk), 1)
    incl = ri >= ci
    strict = ri > ci
    eye = (ri == ci).astype(F32)
    off_masks = []
    b = 1
    while b < chunk:
        off_masks.append((ri // (2 * b) == ci // (2 * b)) & (ri // b != ci // b))
        b *= 2

    chains = [(g, c) for g in range(nheads) for c in range(nchunks)]
    a_mat, attn, rhs, lhs_top, qd, cdec = {}, {}, {}, {}, {}, {}
    for g in range(nheads):
        hs = slice(g * dk, (g + 1) * dk)
        q = q_all[:, hs]
        k = k_all[:, hs]
        v = v_all[:, hs]
        q = q * (lax.rsqrt(jnp.sum(q * q, axis=-1, keepdims=True) + L2_EPS) * (dk ** -0.5))
        k = k * lax.rsqrt(jnp.sum(k * k, axis=-1, keepdims=True) + L2_EPS)
        gc_row = jnp.broadcast_to(gc_ref[g], (LANES, tb))
        gc_col = gc_row.T
        beta_col = jnp.broadcast_to(beta_ref[g], (LANES, tb)).T
        eg_col = jnp.exp(gc_col)
        for c in range(nchunks):
            sl = slice(c * chunk, (c + 1) * chunk)
            qc, kc, vc = q[sl], k[sl], v[sl]
            bc = beta_col[sl]
            gcc = gc_col[sl]
            kb = kc * bc
            diff = gcc[:, :chunk] - gc_row[:chunk, sl]
            decay = jnp.where(incl, jnp.exp(jnp.where(incl, diff, 0.0)), 0.0)
            qk = _dot_nt(jnp.concatenate([qc, kb], axis=0), kc)
            attn[g, c] = qk[:chunk] * decay
            a_mat[g, c] = jnp.where(strict, qk[chunk:] * decay, 0.0)
            rhs[g, c] = jnp.concatenate([kb * eg_col[sl], vc * bc], axis=1)
            g_last = gcc[chunk - 1:chunk, :]
            lhs_top[g, c] = (kc * jnp.exp(g_last - gcc)).T
            qd[g, c] = qc * eg_col[sl]
            cdec[g, c] = jnp.exp(g_last)

    x = {ch: eye - jnp.where(off_masks[0], a_mat[ch], 0.0) for ch in chains}
    for off_mask in off_masks[1:]:
        y = {ch: _dot(jnp.where(off_mask, a_mat[ch], 0.0), x[ch]) for ch in chains}
        x = {ch: x[ch] - _dot(x[ch], y[ch]) for ch in chains}
    wu = {ch: _dot(x[ch], rhs[ch]) for ch in chains}
    st = {ch: _dot(jnp.concatenate([lhs_top[ch], attn[ch]], axis=0), wu[ch]) for ch in chains}

    s = [s_ref[g] for g in range(nheads)]
    outs = {}
    for c in range(nchunks):
        for g in range(nheads):
            t = st[g, c]
            lhs = jnp.concatenate([-t[:dk, :dk], qd[g, c] - t[dk:, :dk]], axis=0)
            r = _dot(lhs, s[g])
            outs[g, c] = r[dk:] + t[dk:, dk:]
            s[g] = s[g] * cdec[g, c] + r[:dk] + t[:dk, dk:]
    for g in range(nheads):
        s_ref[g] = s[g]

    nw = nw_ref[...]
    cols = []
    for g in range(nheads):
        og = jnp.concatenate([outs[g, c] for c in range(nchunks)], axis=0)
        cols.append(og * lax.rsqrt(jnp.mean(og * og, axis=-1, keepdims=True) + RMS_EPS_A) * nw)
    o = jnp.concatenate(cols, axis=1)
    o_ref[...] = (o * _silu(z_ref[...].astype(F32))).astype(o_ref.dtype)


def _gated_delta_net(h, conv_w_t, gc, beta, norm_w, batch, heads):
    M = h.shape[0]
    T = M // batch
    tb = min(GDN_TB, T)
    nt = T // tb
    dk = HEAD_DIM
    chunk = min(GDN_CHUNK, tb)
    hps = min(GDN_HEADS_PER_STEP, heads)
    ng = heads // hps
    width = hps * dk

    def hspec(off):
        return pl.BlockSpec((tb, width), lambda b, hh, t: (b * nt + t, hh + off))

    def wspec(off):
        return pl.BlockSpec((CONV_TAPS, width), lambda b, hh, t: (0, hh + off))

    gspec = pl.BlockSpec((hps, 1, tb), lambda b, hh, t: (hh, 0, b * nt + t))
    return pl.pallas_call(
        functools.partial(_gdn_kernel, chunk),
        grid=(batch, ng, nt),
        in_specs=[hspec(0), hspec(ng), hspec(2 * ng), hspec(3 * ng),
                  wspec(0), wspec(ng), wspec(2 * ng), gspec, gspec,
                  pl.BlockSpec((1, dk), lambda b, hh, t: (0, 0))],
        out_specs=pl.BlockSpec((tb, width), lambda b, hh, t: (b * nt + t, hh)),
        out_shape=jax.ShapeDtypeStruct((M, heads * dk), BF16),
        scratch_shapes=[pltpu.VMEM((hps, dk, dk), F32),
                        pltpu.VMEM((SUBLANES, width), F32),
                        pltpu.VMEM((SUBLANES, width), F32),
                        pltpu.VMEM((SUBLANES, width), F32)],
        compiler_params=_cparams(("parallel", "parallel", "arbitrary")),
        name="gated_delta_rule",
    )(h, h, h, h, conv_w_t, conv_w_t, conv_w_t,
      gc.reshape(heads, 1, M), beta.reshape(heads, 1, M), norm_w.reshape(1, dk))


def _t5_bucket_table(n):
    rel = np.arange(n)
    max_exact = REL_BUCKETS // 2
    nf = np.maximum(rel, 1).astype(np.float32)
    large = max_exact + (np.log(nf / np.float32(max_exact)) / np.float32(math.log(REL_MAX_DIST / max_exact))
                         * np.float32(REL_BUCKETS - max_exact)).astype(np.int32)
    large = np.minimum(large, REL_BUCKETS - 1)
    return np.where(rel < max_exact, rel, large)


def _attn_kernel(lam_init, q1_ref, q2_ref, k1_ref, k2_ref, v_ref, z_ref, brow_ref, lam_ref, sw_ref,
                 o_ref, m1_ref, l1_ref, a1_ref, m2_ref, l2_ref, a2_ref, bd_ref, bs_ref, s_ref):
    tq = bd_ref.shape[0]
    dv = v_ref.shape[0] * LANES
    step = pl.program_id(2)

    def slabs(ref, rows):
        return jnp.concatenate([ref[c, rows, :] for c in range(ref.shape[0])], axis=1)

    @pl.when(step == 0)
    def _():
        r = pltpu.roll(jnp.broadcast_to(brow_ref[...], (tq, tq)), 0, axis=1, stride=1, stride_axis=0)
        r = r * LOG2_E
        ri = lax.broadcasted_iota(jnp.int32, (tq, tq), 0)
        ci = lax.broadcasted_iota(jnp.int32, (tq, tq), 1)
        bd_ref[...] = jnp.where(ri >= ci, r, -jnp.inf)
        bs_ref[...] = jnp.where(ri < ci, r, 0.0)[:LANES, tq - LANES:]

    def init_stats():
        m1_ref[...] = jnp.full_like(m1_ref, -jnp.inf)
        m2_ref[...] = jnp.full_like(m2_ref, -jnp.inf)
        l1_ref[...] = jnp.zeros_like(l1_ref)
        l2_ref[...] = jnp.zeros_like(l2_ref)
        a1_ref[...] = jnp.zeros_like(a1_ref)
        a2_ref[...] = jnp.zeros_like(a2_ref)

    def lanes(x, width):
        return jnp.concatenate([x] * (width // LANES), axis=1)

    def softmax_update(load_s, m_ref, l_ref):
        m_old = m_ref[...]
        m_new = jnp.maximum(m_old, jnp.max(load_s(), axis=-1, keepdims=True))
        p = jnp.exp2(load_s() - lanes(m_new, tq))
        alpha = jnp.exp2(m_old - m_new)
        l_ref[...] = alpha * l_ref[...] + jnp.sum(p, axis=-1, keepdims=True)
        m_ref[...] = m_new
        return p.astype(BF16), alpha

    def scores(q, kj, slot):
        start = pl.multiple_of(kj * tq, tq)
        s_ref[slot, 0] = lax.dot_general(q[0], k1_ref[pl.ds(start, tq), :], (((1,), (1,)), ((), ())),
                                         preferred_element_type=F32)
        s_ref[slot, 1] = lax.dot_general(q[1], k2_ref[pl.ds(start, tq), :], (((1,), (1,)), ((), ())),
                                         preferred_element_type=F32)

    def consume(kj, slot, bias_ref):
        start = pl.multiple_of(kj * tq, tq)
        vblk = slabs(v_ref, pl.ds(start, tq))
        for stream, (m_ref, l_ref, a_ref) in enumerate(((m1_ref, l1_ref, a1_ref), (m2_ref, l2_ref, a2_ref))):
            def load_s(stream=stream):
                s = s_ref[slot, stream]
                return s if bias_ref is None else s + bias_ref[...]

            p, alpha = softmax_update(load_s, m_ref, l_ref)
            a_ref[...] = lanes(alpha, dv) * a_ref[...] + jnp.dot(p, vblk, preferred_element_type=F32)

    def add_previous_block_bias(kj, slot, qi):
        flag = jnp.where(kj == qi - 1, 1.0, 0.0)
        corner = flag * bs_ref[...]
        for stream in range(2):
            s_ref[slot, stream, :LANES, tq - LANES:] = s_ref[slot, stream, :LANES, tq - LANES:] + corner

    def pair_loop(q, qi, first):
        def pair_body(t, carry):
            kj = first + 2 * t
            scores(q, kj + 1, 1)
            consume(kj, 0, None)
            add_previous_block_bias(kj + 1, 1, qi)
            scores(q, kj + 2, 0)
            consume(kj + 1, 1, None)
            return carry

        lax.fori_loop(0, step, pair_body, 0)

    lv = lam_ref[...]
    lam = (jnp.exp(jnp.sum(lv[0:1] * lv[1:2], axis=-1, keepdims=True))
           - jnp.exp(jnp.sum(lv[2:3] * lv[3:4], axis=-1, keepdims=True)) + lam_init)

    def finalize(rows):
        o = a1_ref[...] / lanes(l1_ref[...], dv) - lam * (a2_ref[...] / lanes(l2_ref[...], dv))
        o = o * lax.rsqrt(jnp.mean(o * o, axis=-1, keepdims=True) + RMS_EPS_B) * sw_ref[...]
        o = o * (1.0 - lam_init) * _silu(slabs(z_ref, rows).astype(F32))
        o_ref[rows, :] = o.astype(o_ref.dtype)

    rows_even, rows_odd = slice(0, tq), slice(tq, 2 * tq)
    q_even = (q1_ref[rows_even, :], q2_ref[rows_even, :])
    q_odd = (q1_ref[rows_odd, :], q2_ref[rows_odd, :])
    qi_even, qi_odd = 2 * step, 2 * step + 1

    init_stats()
    scores(q_even, 0, 0)
    pair_loop(q_even, qi_even, 0)
    scores(q_odd, 0, 1)
    consume(qi_even, 0, bd_ref)
    finalize(rows_even)

    init_stats()
    add_previous_block_bias(0, 1, qi_odd)
    scores(q_odd, 1, 0)
    consume(0, 1, None)
    pair_loop(q_odd, qi_odd, 1)
    consume(qi_odd, 0, bd_ref)
    finalize(rows_odd)


def _diff_attention(h, brow, lam_vecs, subln_w, batch, heads, lam_init):
    M = h.shape[1]
    T = M // batch
    tq = min(ATT_TQ, T // 2)
    ns = T // (2 * tq)
    dh = HEAD_DIM
    dv = 2 * dh
    return pl.pallas_call(
        functools.partial(_attn_kernel, lam_init),
        grid=(batch, heads, ns),
        in_specs=[pl.BlockSpec((None, 2 * tq, dh), lambda b, hh, i: (2 * hh, b * ns + i, 0)),
                  pl.BlockSpec((None, 2 * tq, dh), lambda b, hh, i: (2 * hh + 1, b * ns + i, 0)),
                  pl.BlockSpec((None, T, dh), lambda b, hh, i: (2 * heads + 2 * hh, b, 0)),
                  pl.BlockSpec((None, T, dh), lambda b, hh, i: (2 * heads + 2 * hh + 1, b, 0)),
                  pl.BlockSpec((2, T, dh), lambda b, hh, i: (2 * heads + hh, b, 0)),
                  pl.BlockSpec((2, 2 * tq, dh), lambda b, hh, i: (3 * heads + hh, b * ns + i, 0)),
                  pl.BlockSpec((None, 1, tq), lambda b, hh, i: (hh, 0, 0)),
                  pl.BlockSpec((4, dh), lambda b, hh, i: (0, 0)),
                  pl.BlockSpec((1, dv), lambda b, hh, i: (0, 0))],
        out_specs=pl.BlockSpec((2 * tq, dv), lambda b, hh, i: (b * ns + i, hh)),
        out_shape=jax.ShapeDtypeStruct((M, heads * dv), BF16),
        scratch_shapes=[pltpu.VMEM((tq, LANES), F32), pltpu.VMEM((tq, LANES), F32), pltpu.VMEM((tq, dv), F32),
                        pltpu.VMEM((tq, LANES), F32), pltpu.VMEM((tq, LANES), F32), pltpu.VMEM((tq, dv), F32),
                        pltpu.VMEM((tq, tq), F32), pltpu.VMEM((LANES, LANES), F32),
                        pltpu.VMEM((2, 2, tq, tq), F32)],
        compiler_params=_cparams(("parallel", "parallel", "arbitrary")),
        name="diff_attention",
    )(h, h, h, h, h, h, brow, lam_vecs, subln_w.reshape(1, dv))


def kernel(x, ln_g, ln_b, rel_bias, a_w_in, a_conv_w, a_a_log, a_dt_bias, a_norm_w, a_w_out,
           b_w_in, b_lam_q1, b_lam_k1, b_lam_q2, b_lam_k2, b_subln_w, b_w_out):
    batch, T, D = x.shape
    M = batch * T
    depth = ln_g.shape[0]
    a_heads = a_a_log.shape[1]
    b_heads = rel_bias.shape[1]
    alpha = (2.0 * depth) ** 0.25
    att_tq = min(ATT_TQ, T // 2)
    assert T % min(GDN_TB, T) == 0 and T % (2 * att_tq) == 0 and att_tq >= LANES

    q_w = 2 * b_heads * HEAD_DIM
    n_in_b = b_w_in.shape[2]
    b_col_scale = jnp.where(jnp.arange(n_in_b) < q_w, HEAD_DIM ** -0.5 * LOG2_E, 1.0)
    ones_d = jnp.ones((D,), F32)
    cast_jobs = {}
    for i in range(depth):
        if i % 2 == 1:
            cast_jobs["b_in", i // 2] = (b_w_in, i // 2, b_col_scale)
        cast_jobs["out", i] = (a_w_out if i % 2 == 0 else b_w_out, i // 2, ones_d)
    bf16_weights = {}

    xf = x.reshape(M, D)
    xb = None
    for i in range(depth):
        j = i // 2
        if i % 2 == 0:
            qkvz_w = 4 * a_heads * HEAD_DIM
            chunk = min(GDN_CHUNK, GDN_TB, T)
            a_w_t = jnp.swapaxes(a_w_in, 1, 2)
            gc, beta, xb = _gates(xf, a_w_t, j, qkvz_w, a_a_log[j], a_dt_bias[j], chunk)
            jobs = list(cast_jobs.items()) if i == 0 else []
            h, cast = _matmul(xb, _cast_weight_transposed(a_w_t, j, qkvz_w), BF16,
                              [job for _, job in jobs])
            bf16_weights.update({name: wb for (name, _), wb in zip(jobs, cast)})
            o = _gated_delta_net(h, a_conv_w[j].T, gc, beta, a_norm_w[j], batch, a_heads)
        else:
            tq = att_tq
            h, _ = _matmul(xb, bf16_weights["b_in", j], BF16, slab_major=True)
            table = rel_bias[_t5_bucket_table(tq)] - rel_bias[REL_BUCKETS - 1][None, :]
            brow = jnp.roll(table[::-1], 1, axis=0).T.reshape(b_heads, 1, tq)
            lam_vecs = jnp.stack([b_lam_q1[j], b_lam_k1[j], b_lam_q2[j], b_lam_k2[j]])
            lam_init = 0.8 - 0.6 * math.exp(-0.3 * i)
            o = _diff_attention(h, brow, lam_vecs, b_subln_w[j], batch, b_heads, lam_init)
        r = _matmul_residual(o, bf16_weights["out", i], xf, alpha)
        xf, xb = _layer_norm(r, ln_g[i], ln_b[i], with_bf16=i + 1 < depth)
    return xf.reshape(batch, T, D)
```

```python
import functools
import math

import numpy as np
import jax
import jax.numpy as jnp
from jax import lax
from jax.experimental import pallas as pl
from jax.experimental.pallas import tpu as pltpu

F32 = jnp.float32
BF16 = jnp.bfloat16

LANES = 128
SUBLANES = 8
BF16_SUBLANES = 16
VMEM_LIMIT_BYTES = 56 * 1024 * 1024

CONV_TAPS = 4
HEAD_DIM = 128
LN_EPS = 1e-5
RMS_EPS_A = 1e-6
RMS_EPS_B = 1e-5
L2_EPS = 1e-6
REL_BUCKETS = 32
REL_MAX_DIST = 128
LOG2_E = math.log2(math.e)

MM_BM = 1024
MM_BN = 1024
GDN_CHUNK = 128
GDN_TB = 1024
GDN_HEADS_PER_STEP = 4
ATT_TQ = 512
LN_BM = 256
GATES_BM = 512
GATES_W_ROWS = 64
CAST_T_BN = 1024
CAST_T_BK = 1024


def _cparams(sem):
    return pltpu.CompilerParams(dimension_semantics=sem, vmem_limit_bytes=VMEM_LIMIT_BYTES)


def _silu(x):
    h = 0.5 * x
    return h + h * jnp.tanh(h)


def _cast_t_kernel(w_ref, o_ref):
    o_ref[...] = w_ref[...].T.astype(o_ref.dtype)


def _cast_weight_transposed(wt3, layer, nrows):
    K = wt3.shape[2]
    bn, bk = min(CAST_T_BN, nrows), min(CAST_T_BK, K)
    return pl.pallas_call(
        _cast_t_kernel,
        grid=(nrows // bn, K // bk),
        in_specs=[pl.BlockSpec((None, bn, bk), lambda n, k: (layer, n, k))],
        out_specs=pl.BlockSpec((bk, bn), lambda n, k: (k, n)),
        out_shape=jax.ShapeDtypeStruct((K, nrows), BF16),
        compiler_params=_cparams(("parallel", "parallel")),
        name="cast_weight_t",
    )(wt3)


def _mm_kernel(n_side, x_ref, w_ref, *refs):
    side_in, o_ref, side_out = refs[:2 * n_side], refs[2 * n_side], refs[2 * n_side + 1:]
    res = jnp.dot(x_ref[...], w_ref[...], preferred_element_type=F32).astype(o_ref.dtype)
    if len(o_ref.shape) == 3:
        for c in range(o_ref.shape[0]):
            o_ref[c] = res[:, c * LANES:(c + 1) * LANES]
    else:
        o_ref[...] = res
    for k in range(n_side):
        side_out[k][...] = (side_in[2 * k][...] * side_in[2 * k + 1][...]).astype(side_out[k].dtype)


def _matmul(x, w, out_dtype, side_casts=(), slab_major=False):
    M, K = x.shape
    N = w.shape[1]
    bm, bn = min(MM_BM, M), min(MM_BN, N)
    ni, nj = M // bm, N // bn
    side_args, side_in_specs, side_out_specs, side_out_shapes = [], [], [], []
    for w3, layer, col_scale in side_casts:
        ks, ns = w3.shape[1], w3.shape[2]
        rows = ks // (ni * nj)
        assert rows * ni * nj == ks and rows % BF16_SUBLANES == 0
        side_args += [w3, col_scale.reshape(1, ns).astype(F32)]
        side_in_specs += [pl.BlockSpec((None, rows, ns), lambda i, j, layer=layer: (layer, i * nj + j, 0)),
                          pl.BlockSpec((1, ns), lambda i, j: (0, 0))]
        side_out_specs.append(pl.BlockSpec((rows, ns), lambda i, j: (i * nj + j, 0)))
        side_out_shapes.append(jax.ShapeDtypeStruct((ks, ns), BF16))
    if slab_major:
        main_spec = pl.BlockSpec((bn // LANES, bm, LANES), lambda i, j: (j, i, 0))
        main_shape = jax.ShapeDtypeStruct((N // LANES, M, LANES), out_dtype)
    else:
        main_spec = pl.BlockSpec((bm, bn), lambda i, j: (i, j))
        main_shape = jax.ShapeDtypeStruct((M, N), out_dtype)
    out = pl.pallas_call(
        functools.partial(_mm_kernel, len(side_casts)),
        grid=(ni, nj),
        in_specs=[pl.BlockSpec((bm, K), lambda i, j: (i, 0)),
                  pl.BlockSpec((K, bn), lambda i, j: (0, j))] + side_in_specs,
        out_specs=[main_spec] + side_out_specs,
        out_shape=[main_shape] + side_out_shapes,
        compiler_params=_cparams(("parallel", "parallel")),
        name="in_proj",
    )(x, w, *side_args)
    return out[0], out[1:]


def _mm_res_kernel(alpha, o_ref, w_ref, x_ref, r_ref):
    y = jnp.dot(o_ref[...], w_ref[...], preferred_element_type=F32)
    r_ref[...] = alpha * x_ref[...] + y


def _matmul_residual(o, w, x, alpha):
    M, K = o.shape
    N = w.shape[1]
    bm, bn = min(MM_BM, M), min(MM_BN, N)
    return pl.pallas_call(
        functools.partial(_mm_res_kernel, alpha),
        grid=(M // bm, N // bn),
        in_specs=[pl.BlockSpec((bm, K), lambda i, j: (i, 0)),
                  pl.BlockSpec((K, bn), lambda i, j: (0, j)),
                  pl.BlockSpec((bm, bn), lambda i, j: (i, j))],
        out_specs=pl.BlockSpec((bm, bn), lambda i, j: (i, j)),
        out_shape=jax.ShapeDtypeStruct((M, N), F32),
        compiler_params=_cparams(("parallel", "parallel")),
        name="out_proj_residual",
    )(o, w, x)


def _ln_kernel(r_ref, g_ref, b_ref, o_ref, *ob_ref):
    r = r_ref[...]
    mu = jnp.mean(r, axis=-1, keepdims=True)
    d = r - mu
    var = jnp.mean(d * d, axis=-1, keepdims=True)
    y = d * lax.rsqrt(var + LN_EPS) * g_ref[...] + b_ref[...]
    o_ref[...] = y
    for ref in ob_ref:
        ref[...] = y.astype(ref.dtype)


def _layer_norm(r, g, b, with_bf16):
    M, D = r.shape
    bm = min(LN_BM, M)
    row_spec = pl.BlockSpec((bm, D), lambda i: (i, 0))
    out = pl.pallas_call(
        _ln_kernel,
        grid=(M // bm,),
        in_specs=[row_spec,
                  pl.BlockSpec((1, D), lambda i: (0, 0)),
                  pl.BlockSpec((1, D), lambda i: (0, 0))],
        out_specs=[row_spec] + [row_spec] * with_bf16,
        out_shape=[jax.ShapeDtypeStruct((M, D), F32)] + [jax.ShapeDtypeStruct((M, D), BF16)] * with_bf16,
        compiler_params=_cparams(("parallel",)),
        name="layer_norm",
    )(r, g.reshape(1, D), b.reshape(1, D))
    return out[0], (out[1] if with_bf16 else None)


def _gates_kernel(chunk, x_ref, w_ref, alog_ref, dtb_ref, gc_ref, beta_ref, xb_ref):
    nh = gc_ref.shape[0]
    xb = x_ref[...].astype(BF16)
    xb_ref[...] = xb
    row = lax.broadcasted_iota(jnp.int32, w_ref.shape, 0)
    w = jnp.where(row < 2 * nh, w_ref[...], 0.0).astype(BF16)
    ab = lax.dot_general(w, xb, (((1,), (1,)), ((), ())), preferred_element_type=F32)
    a = ab[:nh] + dtb_ref[...]
    b = ab[nh:2 * nh]
    softplus = jnp.maximum(a, 0.0) + jnp.log1p(jnp.exp(-jnp.abs(a)))
    g = -jnp.exp(alog_ref[...]) * softplus
    pos = lax.broadcasted_iota(jnp.int32, g.shape, 1) % chunk
    s = 1
    while s < chunk:
        g = g + jnp.where(pos >= s, pltpu.roll(g, s, axis=1), 0.0)
        s *= 2
    gc_ref[...] = g
    beta_ref[...] = 1.0 / (1.0 + jnp.exp(-b))


def _gates(x, wt3, layer, row0, a_log, dt_bias, chunk):
    M, K = x.shape
    nh = a_log.shape[0]
    bm = min(GATES_BM, M)
    assert row0 % GATES_W_ROWS == 0 and 2 * nh <= GATES_W_ROWS
    return pl.pallas_call(
        functools.partial(_gates_kernel, chunk),
        grid=(M // bm,),
        in_specs=[pl.BlockSpec((bm, K), lambda i: (i, 0)),
                  pl.BlockSpec((None, GATES_W_ROWS, K), lambda i: (layer, row0 // GATES_W_ROWS, 0)),
                  pl.BlockSpec((nh, 1), lambda i: (0, 0)),
                  pl.BlockSpec((nh, 1), lambda i: (0, 0))],
        out_specs=[pl.BlockSpec((nh, bm), lambda i: (0, i)),
                   pl.BlockSpec((nh, bm), lambda i: (0, i)),
                   pl.BlockSpec((bm, K), lambda i: (i, 0))],
        out_shape=[jax.ShapeDtypeStruct((nh, M), F32), jax.ShapeDtypeStruct((nh, M), F32),
                   jax.ShapeDtypeStruct((M, K), BF16)],
        compiler_params=_cparams(("parallel",)),
        name="gdn_gates",
    )(x, wt3, a_log.reshape(nh, 1), dt_bias.reshape(nh, 1))


def _dot(a, b):
    return jnp.dot(a.astype(BF16), b.astype(BF16), preferred_element_type=F32)


def _dot_nt(a, b):
    return lax.dot_general(a.astype(BF16), b.astype(BF16), (((1,), (1,)), ((), ())),
                           preferred_element_type=F32)


def _gdn_kernel(chunk, q_ref, k_ref, v_ref, z_ref, wq_ref, wk_ref, wv_ref, gc_ref, beta_ref,
                nw_ref, o_ref, s_ref, hq_ref, hk_ref, hv_ref):
    tb = q_ref.shape[0]
    dk = HEAD_DIM
    nheads = q_ref.shape[1] // dk
    halo = hq_ref.shape[0]
    nchunks = tb // chunk

    @pl.when(pl.program_id(2) == 0)
    def _():
        s_ref[...] = jnp.zeros_like(s_ref)
        hq_ref[...] = jnp.zeros_like(hq_ref)
        hk_ref[...] = jnp.zeros_like(hk_ref)
        hv_ref[...] = jnp.zeros_like(hv_ref)

    def conv_silu(x_ref, w_ref, h_ref):
        x = x_ref[...].astype(F32)
        xs = jnp.concatenate([h_ref[...], x], axis=0)
        w = w_ref[...]
        y = x * w[CONV_TAPS - 1:CONV_TAPS]
        for j in range(CONV_TAPS - 1):
            off = halo - (CONV_TAPS - 1) + j
            y = y + xs[off:off + tb] * w[j:j + 1]
        h_ref[...] = x[tb - halo:]
        return _silu(y)

    q_all = conv_silu(q_ref, wq_ref, hq_ref)
    k_all = conv_silu(k_ref, wk_ref, hk_ref)
    v_all = conv_silu(v_ref, wv_ref, hv_ref)

    ri = lax.broadcasted_iota(jnp.int32, (chunk, chunk), 0)
    ci = lax.broadcasted_iota(jnp.int32, (chunk, chunk), 1)
    incl = ri >= ci
    strict = ri > ci
    eye = (ri == ci).astype(F32)
    off_masks = []
    b = 1
    while b < chunk:
        off_masks.append((ri // (2 * b) == ci // (2 * b)) & (ri // b != ci // b))
        b *= 2

    chains = [(g, c) for g in range(nheads) for c in range(nchunks)]
    a_mat, attn, rhs, lhs_top, qd, cdec = {}, {}, {}, {}, {}, {}
    for g in range(nheads):
        hs = slice(g * dk, (g + 1) * dk)
        q = q_all[:, hs]
        k = k_all[:, hs]
        v = v_all[:, hs]
        q = q * (lax.rsqrt(jnp.sum(q * q, axis=-1, keepdims=True) + L2_EPS) * (dk ** -0.5))
        k = k * lax.rsqrt(jnp.sum(k * k, axis=-1, keepdims=True) + L2_EPS)
        gc_row = jnp.broadcast_to(gc_ref[g], (LANES, tb))
        gc_col = gc_row.T
        beta_col = jnp.broadcast_to(beta_ref[g], (LANES, tb)).T
        eg_col = jnp.exp(gc_col)
        for c in range(nchunks):
            sl = slice(c * chunk, (c + 1) * chunk)
            qc, kc, vc = q[sl], k[sl], v[sl]
            bc = beta_col[sl]
            gcc = gc_col[sl]
            kb = kc * bc
            diff = gcc[:, :chunk] - gc_row[:chunk, sl]
            decay = jnp.where(incl, jnp.exp(jnp.where(incl, diff, 0.0)), 0.0)
            qk = _dot_nt(jnp.concatenate([qc, kb], axis=0), kc)
            attn[g, c] = qk[:chunk] * decay
            a_mat[g, c] = jnp.where(strict, qk[chunk:] * decay, 0.0)
            rhs[g, c] = jnp.concatenate([kb * eg_col[sl], vc * bc], axis=1)
            g_last = gcc[chunk - 1:chunk, :]
            lhs_top[g, c] = (kc * jnp.exp(g_last - gcc)).T
            qd[g, c] = qc * eg_col[sl]
            cdec[g, c] = jnp.exp(g_last)

    x = {ch: eye - jnp.where(off_masks[0], a_mat[ch], 0.0) for ch in chains}
    for off_mask in off_masks[1:]:
        y = {ch: _dot(jnp.where(off_mask, a_mat[ch], 0.0), x[ch]) for ch in chains}
        x = {ch: x[ch] - _dot(x[ch], y[ch]) for ch in chains}
    wu = {ch: _dot(x[ch], rhs[ch]) for ch in chains}
    st = {ch: _dot(jnp.concatenate([lhs_top[ch], attn[ch]], axis=0), wu[ch]) for ch in chains}

    s = [s_ref[g] for g in range(nheads)]
    outs = {}
    for c in range(nchunks):
        for g in range(nheads):
            t = st[g, c]
            lhs = jnp.concatenate([-t[:dk, :dk], qd[g, c] - t[dk:, :dk]], axis=0)
            r = _dot(lhs, s[g])
            outs[g, c] = r[dk:] + t[dk:, dk:]
            s[g] = s[g] * cdec[g, c] + r[:dk] + t[:dk, dk:]
    for g in range(nheads):
        s_ref[g] = s[g]

    nw = nw_ref[...]
    cols = []
    for g in range(nheads):
        og = jnp.concatenate([outs[g, c] for c in range(nchunks)], axis=0)
        cols.append(og * lax.rsqrt(jnp.mean(og * og, axis=-1, keepdims=True) + RMS_EPS_A) * nw)
    o = jnp.concatenate(cols, axis=1)
    o_ref[...] = (o * _silu(z_ref[...].astype(F32))).astype(o_ref.dtype)


def _gated_delta_net(h, conv_w_t, gc, beta, norm_w, batch, heads):
    M = h.shape[0]
    T = M // batch
    tb = min(GDN_TB, T)
    nt = T // tb
    dk = HEAD_DIM
    chunk = min(GDN_CHUNK, tb)
    hps = min(GDN_HEADS_PER_STEP, heads)
    ng = heads // hps
    width = hps * dk

    def hspec(off):
        return pl.BlockSpec((tb, width), lambda b, hh, t: (b * nt + t, hh + off))

    def wspec(off):
        return pl.BlockSpec((CONV_TAPS, width), lambda b, hh, t: (0, hh + off))

    gspec = pl.BlockSpec((hps, 1, tb), lambda b, hh, t: (hh, 0, b * nt + t))
    return pl.pallas_call(
        functools.partial(_gdn_kernel, chunk),
        grid=(batch, ng, nt),
        in_specs=[hspec(0), hspec(ng), hspec(2 * ng), hspec(3 * ng),
                  wspec(0), wspec(ng), wspec(2 * ng), gspec, gspec,
                  pl.BlockSpec((1, dk), lambda b, hh, t: (0, 0))],
        out_specs=pl.BlockSpec((tb, width), lambda b, hh, t: (b * nt + t, hh)),
        out_shape=jax.ShapeDtypeStruct((M, heads * dk), BF16),
        scratch_shapes=[pltpu.VMEM((hps, dk, dk), F32),
                        pltpu.VMEM((SUBLANES, width), F32),
                        pltpu.VMEM((SUBLANES, width), F32),
                        pltpu.VMEM((SUBLANES, width), F32)],
        compiler_params=_cparams(("parallel", "parallel", "arbitrary")),
        name="gated_delta_rule",
    )(h, h, h, h, conv_w_t, conv_w_t, conv_w_t,
      gc.reshape(heads, 1, M), beta.reshape(heads, 1, M), norm_w.reshape(1, dk))


def _t5_bucket_table(n):
    rel = np.arange(n)
    max_exact = REL_BUCKETS // 2
    nf = np.maximum(rel, 1).astype(np.float32)
    large = max_exact + (np.log(nf / np.float32(max_exact)) / np.float32(math.log(REL_MAX_DIST / max_exact))
                         * np.float32(REL_BUCKETS - max_exact)).astype(np.int32)
    large = np.minimum(large, REL_BUCKETS - 1)
    return np.where(rel < max_exact, rel, large)


def _attn_kernel(lam_init, q1_ref, q2_ref, k1_ref, k2_ref, v_ref, z_ref, brow_ref, lam_ref, sw_ref,
                 o_ref, m1_ref, l1_ref, a1_ref, m2_ref, l2_ref, a2_ref, bd_ref, bs_ref, s_ref, kt_ref):
    tq = bd_ref.shape[0]
    dv = v_ref.shape[0] * LANES
    step = pl.program_id(2)

    def slabs(ref, rows):
        return jnp.concatenate([ref[c, rows, :] for c in range(ref.shape[0])], axis=1)

    @pl.when(step == 0)
    def _():
        r = pltpu.roll(jnp.broadcast_to(brow_ref[...], (tq, tq)), 0, axis=1, stride=1, stride_axis=0)
        r = r * LOG2_E
        ri = lax.broadcasted_iota(jnp.int32, (tq, tq), 0)
        ci = lax.broadcasted_iota(jnp.int32, (tq, tq), 1)
        bd_ref[...] = jnp.where(ri >= ci, r, -jnp.inf)
        bs_ref[...] = jnp.where(ri < ci, r, 0.0)[:LANES, tq - LANES:]
        for stream, k_ref in enumerate((k1_ref, k2_ref)):
            for blk in range(k_ref.shape[0] // tq):
                rows = slice(blk * tq, (blk + 1) * tq)
                kt_ref[stream, :, rows] = k_ref[rows, :].astype(F32).T.astype(BF16)

    def init_stats():
        m1_ref[...] = jnp.full_like(m1_ref, -jnp.inf)
        m2_ref[...] = jnp.full_like(m2_ref, -jnp.inf)
        l1_ref[...] = jnp.zeros_like(l1_ref)
        l2_ref[...] = jnp.zeros_like(l2_ref)
        a1_ref[...] = jnp.zeros_like(a1_ref)
        a2_ref[...] = jnp.zeros_like(a2_ref)

    def lanes(x, width):
        return jnp.concatenate([x] * (width // LANES), axis=1)

    def softmax_update(load_s, m_ref, l_ref):
        m_old = m_ref[...]
        m_new = jnp.maximum(m_old, jnp.max(load_s(), axis=-1, keepdims=True))
        p = jnp.exp2(load_s() - lanes(m_new, tq))
        alpha = jnp.exp2(m_old - m_new)
        l_ref[...] = alpha * l_ref[...] + jnp.sum(p, axis=-1, keepdims=True)
        m_ref[...] = m_new
        return p.astype(BF16), alpha

    def scores(q, kj, slot):
        start = pl.multiple_of(kj * tq, tq)
        s_ref[slot, 0] = jnp.dot(q[0], kt_ref[0, :, pl.ds(start, tq)], preferred_element_type=F32)
        s_ref[slot, 1] = jnp.dot(q[1], kt_ref[1, :, pl.ds(start, tq)], preferred_element_type=F32)

    def consume(kj, slot, bias_ref):
        start = pl.multiple_of(kj * tq, tq)
        vblk = slabs(v_ref, pl.ds(start, tq))
        for stream, (m_ref, l_ref, a_ref) in enumerate(((m1_ref, l1_ref, a1_ref), (m2_ref, l2_ref, a2_ref))):
            def load_s(stream=stream):
                s = s_ref[slot, stream]
                return s if bias_ref is None else s + bias_ref[...]

            p, alpha = softmax_update(load_s, m_ref, l_ref)
            a_ref[...] = lanes(alpha, dv) * a_ref[...] + jnp.dot(p, vblk, preferred_element_type=F32)

    def add_previous_block_bias(kj, slot, qi):
        flag = jnp.where(kj == qi - 1, 1.0, 0.0)
        corner = flag * bs_ref[...]
        for stream in range(2):
            s_ref[slot, stream, :LANES, tq - LANES:] = s_ref[slot, stream, :LANES, tq - LANES:] + corner

    def pair_loop(q, qi, first):
        def pair_body(t, carry):
            kj = first + 2 * t
            scores(q, kj + 1, 1)
            consume(kj, 0, None)
            add_previous_block_bias(kj + 1, 1, qi)
            scores(q, kj + 2, 0)
            consume(kj + 1, 1, None)
            return carry

        lax.fori_loop(0, step, pair_body, 0)

    lv = lam_ref[...]
    lam = (jnp.exp(jnp.sum(lv[0:1] * lv[1:2], axis=-1, keepdims=True))
           - jnp.exp(jnp.sum(lv[2:3] * lv[3:4], axis=-1, keepdims=True)) + lam_init)

    def finalize(rows):
        o = a1_ref[...] / lanes(l1_ref[...], dv) - lam * (a2_ref[...] / lanes(l2_ref[...], dv))
        o = o * lax.rsqrt(jnp.mean(o * o, axis=-1, keepdims=True) + RMS_EPS_B) * sw_ref[...]
        o = o * (1.0 - lam_init) * _silu(slabs(z_ref, rows).astype(F32))
        o_ref[rows, :] = o.astype(o_ref.dtype)

    rows_even, rows_odd = slice(0, tq), slice(tq, 2 * tq)
    q_even = (q1_ref[rows_even, :], q2_ref[rows_even, :])
    q_odd = (q1_ref[rows_odd, :], q2_ref[rows_odd, :])
    qi_even, qi_odd = 2 * step, 2 * step + 1

    init_stats()
    scores(q_even, 0, 0)
    pair_loop(q_even, qi_even, 0)
    scores(q_odd, 0, 1)
    consume(qi_even, 0, bd_ref)
    finalize(rows_even)

    init_stats()
    add_previous_block_bias(0, 1, qi_odd)
    scores(q_odd, 1, 0)
    consume(0, 1, None)
    pair_loop(q_odd, qi_odd, 1)
    consume(qi_odd, 0, bd_ref)
    finalize(rows_odd)


def _diff_attention(h, brow, lam_vecs, subln_w, batch, heads, lam_init):
    M = h.shape[1]
    T = M // batch
    tq = min(ATT_TQ, T // 2)
    ns = T // (2 * tq)
    dh = HEAD_DIM
    dv = 2 * dh
    return pl.pallas_call(
        functools.partial(_attn_kernel, lam_init),
        grid=(batch, heads, ns),
        in_specs=[pl.BlockSpec((None, 2 * tq, dh), lambda b, hh, i: (2 * hh, b * ns + i, 0)),
                  pl.BlockSpec((None, 2 * tq, dh), lambda b, hh, i: (2 * hh + 1, b * ns + i, 0)),
                  pl.BlockSpec((None, T, dh), lambda b, hh, i: (2 * heads + 2 * hh, b, 0)),
                  pl.BlockSpec((None, T, dh), lambda b, hh, i: (2 * heads + 2 * hh + 1, b, 0)),
                  pl.BlockSpec((2, T, dh), lambda b, hh, i: (2 * heads + hh, b, 0)),
                  pl.BlockSpec((2, 2 * tq, dh), lambda b, hh, i: (3 * heads + hh, b * ns + i, 0)),
                  pl.BlockSpec((None, 1, tq), lambda b, hh, i: (hh, 0, 0)),
                  pl.BlockSpec((4, dh), lambda b, hh, i: (0, 0)),
                  pl.BlockSpec((1, dv), lambda b, hh, i: (0, 0))],
        out_specs=pl.BlockSpec((2 * tq, dv), lambda b, hh, i: (b * ns + i, hh)),
        out_shape=jax.ShapeDtypeStruct((M, heads * dv), BF16),
        scratch_shapes=[pltpu.VMEM((tq, LANES), F32), pltpu.VMEM((tq, LANES), F32), pltpu.VMEM((tq, dv), F32),
                        pltpu.VMEM((tq, LANES), F32), pltpu.VMEM((tq, LANES), F32), pltpu.VMEM((tq, dv), F32),
                        pltpu.VMEM((tq, tq), F32), pltpu.VMEM((LANES, LANES), F32),
                        pltpu.VMEM((2, 2, tq, tq), F32), pltpu.VMEM((2, dh, T), BF16)],
        compiler_params=_cparams(("parallel", "parallel", "arbitrary")),
        name="diff_attention",
    )(h, h, h, h, h, h, brow, lam_vecs, subln_w.reshape(1, dv))


def kernel(x, ln_g, ln_b, rel_bias, a_w_in, a_conv_w, a_a_log, a_dt_bias, a_norm_w, a_w_out,
           b_w_in, b_lam_q1, b_lam_k1, b_lam_q2, b_lam_k2, b_subln_w, b_w_out):
    batch, T, D = x.shape
    M = batch * T
    depth = ln_g.shape[0]
    a_heads = a_a_log.shape[1]
    b_heads = rel_bias.shape[1]
    alpha = (2.0 * depth) ** 0.25
    att_tq = min(ATT_TQ, T // 2)
    assert T % min(GDN_TB, T) == 0 and T % (2 * att_tq) == 0 and att_tq >= LANES

    q_w = 2 * b_heads * HEAD_DIM
    n_in_b = b_w_in.shape[2]
    b_col_scale = jnp.where(jnp.arange(n_in_b) < q_w, HEAD_DIM ** -0.5 * LOG2_E, 1.0)
    ones_d = jnp.ones((D,), F32)
    cast_jobs = {}
    for i in range(depth):
        if i % 2 == 1:
            cast_jobs["b_in", i // 2] = (b_w_in, i // 2, b_col_scale)
        cast_jobs["out", i] = (a_w_out if i % 2 == 0 else b_w_out, i // 2, ones_d)
    bf16_weights = {}

    xf = x.reshape(M, D)
    xb = None
    for i in range(depth):
        j = i // 2
        if i % 2 == 0:
            qkvz_w = 4 * a_heads * HEAD_DIM
            chunk = min(GDN_CHUNK, GDN_TB, T)
            a_w_t = jnp.swapaxes(a_w_in, 1, 2)
            gc, beta, xb = _gates(xf, a_w_t, j, qkvz_w, a_a_log[j], a_dt_bias[j], chunk)
            jobs = list(cast_jobs.items()) if i == 0 else []
            h, cast = _matmul(xb, _cast_weight_transposed(a_w_t, j, qkvz_w), BF16,
                              [job for _, job in jobs])
            bf16_weights.update({name: wb for (name, _), wb in zip(jobs, cast)})
            o = _gated_delta_net(h, a_conv_w[j].T, gc, beta, a_norm_w[j], batch, a_heads)
        else:
            tq = att_tq
            h, _ = _matmul(xb, bf16_weights["b_in", j], BF16, slab_major=True)
            table = rel_bias[_t5_bucket_table(tq)] - rel_bias[REL_BUCKETS - 1][None, :]
            brow = jnp.roll(table[::-1], 1, axis=0).T.reshape(b_heads, 1, tq)
            lam_vecs = jnp.stack([b_lam_q1[j], b_lam_k1[j], b_lam_q2[j], b_lam_k2[j]])
            lam_init = 0.8 - 0.6 * math.exp(-0.3 * i)
            o = _diff_attention(h, brow, lam_vecs, b_subln_w[j], batch, b_heads, lam_init)
        r = _matmul_residual(o, bf16_weights["out", i], xf, alpha)
        xf, xb = _layer_norm(r, ln_g[i], ln_b[i], with_bf16=i + 1 < depth)
    return xf.reshape(batch, T, D)
```

```python
import functools
import math

import numpy as np
import jax
import jax.numpy as jnp
from jax import lax
from jax.experimental import pallas as pl
from jax.experimental.pallas import tpu as pltpu

F32 = jnp.float32
BF16 = jnp.bfloat16

LANES = 128
SUBLANES = 8
BF16_SUBLANES = 16
VMEM_LIMIT_BYTES = 56 * 1024 * 1024

CONV_TAPS = 4
HEAD_DIM = 128
LN_EPS = 1e-5
RMS_EPS_A = 1e-6
RMS_EPS_B = 1e-5
L2_EPS = 1e-6
REL_BUCKETS = 32
REL_MAX_DIST = 128
LOG2_E = math.log2(math.e)

MM_BM = 1024
MM_BN = 1024
GDN_CHUNK = 128
GDN_TB = 1024
GDN_HEADS_PER_STEP = 4
ATT_TQ = 512
LN_BM = 512
GATES_BM = 512
GATES_W_ROWS = 64
CAST_T_BN = 1024
CAST_T_BK = 1024


def _cparams(sem):
    return pltpu.CompilerParams(dimension_semantics=sem, vmem_limit_bytes=VMEM_LIMIT_BYTES)


def _silu(x):
    h = 0.5 * x
    return h + h * jnp.tanh(h)


def _cast_t_kernel(w_ref, o_ref):
    o_ref[...] = w_ref[...].T.astype(o_ref.dtype)


def _cast_weight_transposed(wt3, layer, nrows):
    K = wt3.shape[2]
    bn, bk = min(CAST_T_BN, nrows), min(CAST_T_BK, K)
    return pl.pallas_call(
        _cast_t_kernel,
        grid=(nrows // bn, K // bk),
        in_specs=[pl.BlockSpec((None, bn, bk), lambda n, k: (layer, n, k))],
        out_specs=pl.BlockSpec((bk, bn), lambda n, k: (k, n)),
        out_shape=jax.ShapeDtypeStruct((K, nrows), BF16),
        compiler_params=_cparams(("parallel", "parallel")),
        name="cast_weight_t",
    )(wt3)


def _mm_kernel(n_side, x_ref, w_ref, *refs):
    side_in, o_ref, side_out = refs[:2 * n_side], refs[2 * n_side], refs[2 * n_side + 1:]
    res = jnp.dot(x_ref[...], w_ref[...], preferred_element_type=F32).astype(o_ref.dtype)
    if len(o_ref.shape) == 3:
        for c in range(o_ref.shape[0]):
            o_ref[c] = res[:, c * LANES:(c + 1) * LANES]
    else:
        o_ref[...] = res
    for k in range(n_side):
        side_out[k][...] = (side_in[2 * k][...] * side_in[2 * k + 1][...]).astype(side_out[k].dtype)


def _matmul(x, w, out_dtype, side_casts=(), slab_major=False):
    M, K = x.shape
    N = w.shape[1]
    bm, bn = min(MM_BM, M), min(MM_BN, N)
    ni, nj = M // bm, N // bn
    side_args, side_in_specs, side_out_specs, side_out_shapes = [], [], [], []
    for w3, layer, col_scale in side_casts:
        ks, ns = w3.shape[1], w3.shape[2]
        rows = ks // (ni * nj)
        assert rows * ni * nj == ks and rows % BF16_SUBLANES == 0
        side_args += [w3, col_scale.reshape(1, ns).astype(F32)]
        side_in_specs += [pl.BlockSpec((None, rows, ns), lambda i, j, layer=layer: (layer, i * nj + j, 0)),
                          pl.BlockSpec((1, ns), lambda i, j: (0, 0))]
        side_out_specs.append(pl.BlockSpec((rows, ns), lambda i, j: (i * nj + j, 0)))
        side_out_shapes.append(jax.ShapeDtypeStruct((ks, ns), BF16))
    if slab_major:
        main_spec = pl.BlockSpec((bn // LANES, bm, LANES), lambda i, j: (j, i, 0))
        main_shape = jax.ShapeDtypeStruct((N // LANES, M, LANES), out_dtype)
    else:
        main_spec = pl.BlockSpec((bm, bn), lambda i, j: (i, j))
        main_shape = jax.ShapeDtypeStruct((M, N), out_dtype)
    out = pl.pallas_call(
        functools.partial(_mm_kernel, len(side_casts)),
        grid=(ni, nj),
        in_specs=[pl.BlockSpec((bm, K), lambda i, j: (i, 0)),
                  pl.BlockSpec((K, bn), lambda i, j: (0, j))] + side_in_specs,
        out_specs=[main_spec] + side_out_specs,
        out_shape=[main_shape] + side_out_shapes,
        compiler_params=_cparams(("parallel", "parallel")),
        name="in_proj",
    )(x, w, *side_args)
    return out[0], out[1:]


def _mm_res_kernel(alpha, o_ref, w_ref, x_ref, r_ref):
    y = jnp.dot(o_ref[...], w_ref[...], preferred_element_type=F32)
    r_ref[...] = alpha * x_ref[...] + y


def _matmul_residual(o, w, x, alpha):
    M, K = o.shape
    N = w.shape[1]
    bm, bn = min(MM_BM, M), min(MM_BN, N)
    return pl.pallas_call(
        functools.partial(_mm_res_kernel, alpha),
        grid=(M // bm, N // bn),
        in_specs=[pl.BlockSpec((bm, K), lambda i, j: (i, 0)),
                  pl.BlockSpec((K, bn), lambda i, j: (0, j)),
                  pl.BlockSpec((bm, bn), lambda i, j: (i, j))],
        out_specs=pl.BlockSpec((bm, bn), lambda i, j: (i, j)),
        out_shape=jax.ShapeDtypeStruct((M, N), F32),
        compiler_params=_cparams(("parallel", "parallel")),
        name="out_proj_residual",
    )(o, w, x)


def _ln_kernel(r_ref, g_ref, b_ref, o_ref, *ob_ref):
    r = r_ref[...]
    mu = jnp.mean(r, axis=-1, keepdims=True)
    d = r - mu
    var = jnp.mean(d * d, axis=-1, keepdims=True)
    y = d * lax.rsqrt(var + LN_EPS) * g_ref[...] + b_ref[...]
    o_ref[...] = y
    for ref in ob_ref:
        ref[...] = y.astype(ref.dtype)


def _layer_norm(r, g, b, with_bf16):
    M, D = r.shape
    bm = min(LN_BM, M)
    row_spec = pl.BlockSpec((bm, D), lambda i: (i, 0))
    out = pl.pallas_call(
        _ln_kernel,
        grid=(M // bm,),
        in_specs=[row_spec,
                  pl.BlockSpec((1, D), lambda i: (0, 0)),
                  pl.BlockSpec((1, D), lambda i: (0, 0))],
        out_specs=[row_spec] + [row_spec] * with_bf16,
        out_shape=[jax.ShapeDtypeStruct((M, D), F32)] + [jax.ShapeDtypeStruct((M, D), BF16)] * with_bf16,
        compiler_params=_cparams(("parallel",)),
        name="layer_norm",
    )(r, g.reshape(1, D), b.reshape(1, D))
    return out[0], (out[1] if with_bf16 else None)


def _gates_kernel(chunk, x_ref, w_ref, alog_ref, dtb_ref, gc_ref, beta_ref, xb_ref):
    nh = gc_ref.shape[0]
    xb = x_ref[...].astype(BF16)
    xb_ref[...] = xb
    row = lax.broadcasted_iota(jnp.int32, w_ref.shape, 0)
    w = jnp.where(row < 2 * nh, w_ref[...], 0.0).astype(BF16)
    ab = lax.dot_general(w, xb, (((1,), (1,)), ((), ())), preferred_element_type=F32)
    a = ab[:nh] + dtb_ref[...]
    b = ab[nh:2 * nh]
    softplus = jnp.maximum(a, 0.0) + jnp.log1p(jnp.exp(-jnp.abs(a)))
    g = -jnp.exp(alog_ref[...]) * softplus
    pos = lax.broadcasted_iota(jnp.int32, g.shape, 1) % chunk
    s = 1
    while s < chunk:
        g = g + jnp.where(pos >= s, pltpu.roll(g, s, axis=1), 0.0)
        s *= 2
    gc_ref[...] = g
    beta_ref[...] = 1.0 / (1.0 + jnp.exp(-b))


def _gates(x, wt3, layer, row0, a_log, dt_bias, chunk):
    M, K = x.shape
    nh = a_log.shape[0]
    bm = min(GATES_BM, M)
    assert row0 % GATES_W_ROWS == 0 and 2 * nh <= GATES_W_ROWS
    return pl.pallas_call(
        functools.partial(_gates_kernel, chunk),
        grid=(M // bm,),
        in_specs=[pl.BlockSpec((bm, K), lambda i: (i, 0)),
                  pl.BlockSpec((None, GATES_W_ROWS, K), lambda i: (layer, row0 // GATES_W_ROWS, 0)),
                  pl.BlockSpec((nh, 1), lambda i: (0, 0)),
                  pl.BlockSpec((nh, 1), lambda i: (0, 0))],
        out_specs=[pl.BlockSpec((nh, bm), lambda i: (0, i)),
                   pl.BlockSpec((nh, bm), lambda i: (0, i)),
                   pl.BlockSpec((bm, K), lambda i: (i, 0))],
        out_shape=[jax.ShapeDtypeStruct((nh, M), F32), jax.ShapeDtypeStruct((nh, M), F32),
                   jax.ShapeDtypeStruct((M, K), BF16)],
        compiler_params=_cparams(("parallel",)),
        name="gdn_gates",
    )(x, wt3, a_log.reshape(nh, 1), dt_bias.reshape(nh, 1))


def _dot(a, b):
    return jnp.dot(a.astype(BF16), b.astype(BF16), preferred_element_type=F32)


def _dot_nt(a, b):
    return lax.dot_general(a.astype(BF16), b.astype(BF16), (((1,), (1,)), ((), ())),
                           preferred_element_type=F32)


def _gdn_kernel(chunk, q_ref, k_ref, v_ref, z_ref, wq_ref, wk_ref, wv_ref, gc_ref, beta_ref,
                nw_ref, o_ref, s_ref, hq_ref, hk_ref, hv_ref):
    tb = q_ref.shape[0]
    dk = HEAD_DIM
    nheads = q_ref.shape[1] // dk
    halo = hq_ref.shape[0]
    nchunks = tb // chunk

    @pl.when(pl.program_id(2) == 0)
    def _():
        s_ref[...] = jnp.zeros_like(s_ref)
        hq_ref[...] = jnp.zeros_like(hq_ref)
        hk_ref[...] = jnp.zeros_like(hk_ref)
        hv_ref[...] = jnp.zeros_like(hv_ref)

    def conv_silu(x_ref, w_ref, h_ref):
        x = x_ref[...].astype(F32)
        xs = jnp.concatenate([h_ref[...], x], axis=0)
        w = w_ref[...]
        y = x * w[CONV_TAPS - 1:CONV_TAPS]
        for j in range(CONV_TAPS - 1):
            off = halo - (CONV_TAPS - 1) + j
            y = y + xs[off:off + tb] * w[j:j + 1]
        h_ref[...] = x[tb - halo:]
        return _silu(y)

    q_all = conv_silu(q_ref, wq_ref, hq_ref)
    k_all = conv_silu(k_ref, wk_ref, hk_ref)
    v_all = conv_silu(v_ref, wv_ref, hv_ref)

    ri = lax.broadcasted_iota(jnp.int32, (chunk, chunk), 0)
    ci = lax.broadcasted_iota(jnp.int32, (chunk, chunk), 1)
    incl = ri >= ci
    strict = ri > ci
    eye = (ri == ci).astype(F32)
    off_masks = []
    b = 1
    while b < chunk:
        off_masks.append((ri // (2 * b) == ci // (2 * b)) & (ri // b != ci // b))
        b *= 2

    chains = [(g, c) for g in range(nheads) for c in range(nchunks)]
    a_mat, attn, rhs, lhs_top, qd, cdec = {}, {}, {}, {}, {}, {}
    for g in range(nheads):
        hs = slice(g * dk, (g + 1) * dk)
        q = q_all[:, hs]
        k = k_all[:, hs]
        v = v_all[:, hs]
        q = q * (lax.rsqrt(jnp.sum(q * q, axis=-1, keepdims=True) + L2_EPS) * (dk ** -0.5))
        k = k * lax.rsqrt(jnp.sum(k * k, axis=-1, keepdims=True) + L2_EPS)
        gc_row = jnp.broadcast_to(gc_ref[g], (LANES, tb))
        gc_col = gc_row.T
        beta_col = jnp.broadcast_to(beta_ref[g], (LANES, tb)).T
        eg_col = jnp.exp(gc_col)
        for c in range(nchunks):
            sl = slice(c * chunk, (c + 1) * chunk)
            qc, kc, vc = q[sl], k[sl], v[sl]
            bc = beta_col[sl]
            gcc = gc_col[sl]
            kb = kc * bc
            diff = gcc[:, :chunk] - gc_row[:chunk, sl]
            decay = jnp.where(incl, jnp.exp(jnp.where(incl, diff, 0.0)), 0.0)
            qk = _dot_nt(jnp.concatenate([qc, kb], axis=0), kc)
            attn[g, c] = qk[:chunk] * decay
            a_mat[g, c] = jnp.where(strict, qk[chunk:] * decay, 0.0)
            rhs[g, c] = jnp.concatenate([kb * eg_col[sl], vc * bc], axis=1)
            g_last = gcc[chunk - 1:chunk, :]
            lhs_top[g, c] = (kc * jnp.exp(g_last - gcc)).T
            qd[g, c] = qc * eg_col[sl]
            cdec[g, c] = jnp.exp(g_last)

    x = {ch: eye - jnp.where(off_masks[0], a_mat[ch], 0.0) for ch in chains}
    for off_mask in off_masks[1:]:
        y = {ch: _dot(jnp.where(off_mask, a_mat[ch], 0.0), x[ch]) for ch in chains}
        x = {ch: x[ch] - _dot(x[ch], y[ch]) for ch in chains}
    wu = {ch: _dot(x[ch], rhs[ch]) for ch in chains}
    st = {ch: _dot(jnp.concatenate([lhs_top[ch], attn[ch]], axis=0), wu[ch]) for ch in chains}

    s = [s_ref[g] for g in range(nheads)]
    outs = {}
    for c in range(nchunks):
        for g in range(nheads):
            t = st[g, c]
            lhs = jnp.concatenate([-t[:dk, :dk], qd[g, c] - t[dk:, :dk]], axis=0)
            r = _dot(lhs, s[g])
            outs[g, c] = r[dk:] + t[dk:, dk:]
            s[g] = s[g] * cdec[g, c] + r[:dk] + t[:dk, dk:]
    for g in range(nheads):
        s_ref[g] = s[g]

    nw = nw_ref[...]
    cols = []
    for g in range(nheads):
        og = jnp.concatenate([outs[g, c] for c in range(nchunks)], axis=0)
        cols.append(og * lax.rsqrt(jnp.mean(og * og, axis=-1, keepdims=True) + RMS_EPS_A) * nw)
    o = jnp.concatenate(cols, axis=1)
    o_ref[...] = (o * _silu(z_ref[...].astype(F32))).astype(o_ref.dtype)


def _gated_delta_net(h, conv_w_t, gc, beta, norm_w, batch, heads):
    M = h.shape[0]
    T = M // batch
    tb = min(GDN_TB, T)
    nt = T // tb
    dk = HEAD_DIM
    chunk = min(GDN_CHUNK, tb)
    hps = min(GDN_HEADS_PER_STEP, heads)
    ng = heads // hps
    width = hps * dk

    def hspec(off):
        return pl.BlockSpec((tb, width), lambda b, hh, t: (b * nt + t, hh + off))

    def wspec(off):
        return pl.BlockSpec((CONV_TAPS, width), lambda b, hh, t: (0, hh + off))

    gspec = pl.BlockSpec((hps, 1, tb), lambda b, hh, t: (hh, 0, b * nt + t))
    return pl.pallas_call(
        functools.partial(_gdn_kernel, chunk),
        grid=(batch, ng, nt),
        in_specs=[hspec(0), hspec(ng), hspec(2 * ng), hspec(3 * ng),
                  wspec(0), wspec(ng), wspec(2 * ng), gspec, gspec,
                  pl.BlockSpec((1, dk), lambda b, hh, t: (0, 0))],
        out_specs=pl.BlockSpec((tb, width), lambda b, hh, t: (b * nt + t, hh)),
        out_shape=jax.ShapeDtypeStruct((M, heads * dk), BF16),
        scratch_shapes=[pltpu.VMEM((hps, dk, dk), F32),
                        pltpu.VMEM((SUBLANES, width), F32),
                        pltpu.VMEM((SUBLANES, width), F32),
                        pltpu.VMEM((SUBLANES, width), F32)],
        compiler_params=_cparams(("parallel", "parallel", "arbitrary")),
        name="gated_delta_rule",
    )(h, h, h, h, conv_w_t, conv_w_t, conv_w_t,
      gc.reshape(heads, 1, M), beta.reshape(heads, 1, M), norm_w.reshape(1, dk))


def _t5_bucket_table(n):
    rel = np.arange(n)
    max_exact = REL_BUCKETS // 2
    nf = np.maximum(rel, 1).astype(np.float32)
    large = max_exact + (np.log(nf / np.float32(max_exact)) / np.float32(math.log(REL_MAX_DIST / max_exact))
                         * np.float32(REL_BUCKETS - max_exact)).astype(np.int32)
    large = np.minimum(large, REL_BUCKETS - 1)
    return np.where(rel < max_exact, rel, large)


def _attn_kernel(lam_init, q1_ref, q2_ref, k1_ref, k2_ref, v_ref, z_ref, brow_ref, lam_ref, sw_ref,
                 o_ref, m1_ref, l1_ref, a1_ref, m2_ref, l2_ref, a2_ref, bd_ref, bs_ref, s_ref):
    tq = bd_ref.shape[0]
    dv = v_ref.shape[0] * LANES
    step = pl.program_id(2)

    def slabs(ref, rows):
        return jnp.concatenate([ref[c, rows, :] for c in range(ref.shape[0])], axis=1)

    @pl.when(step == 0)
    def _():
        r = pltpu.roll(jnp.broadcast_to(brow_ref[...], (tq, tq)), 0, axis=1, stride=1, stride_axis=0)
        r = r * LOG2_E
        ri = lax.broadcasted_iota(jnp.int32, (tq, tq), 0)
        ci = lax.broadcasted_iota(jnp.int32, (tq, tq), 1)
        bd_ref[...] = jnp.where(ri >= ci, r, -jnp.inf)
        bs_ref[...] = jnp.where(ri < ci, r, 0.0)[:LANES, tq - LANES:]

    def init_stats():
        m1_ref[...] = jnp.full_like(m1_ref, -jnp.inf)
        m2_ref[...] = jnp.full_like(m2_ref, -jnp.inf)
        l1_ref[...] = jnp.zeros_like(l1_ref)
        l2_ref[...] = jnp.zeros_like(l2_ref)
        a1_ref[...] = jnp.zeros_like(a1_ref)
        a2_ref[...] = jnp.zeros_like(a2_ref)

    def lanes(x, width):
        return jnp.concatenate([x] * (width // LANES), axis=1)

    def softmax_update(s, m_ref, l_ref):
        m_old = m_ref[...]
        m_new = jnp.maximum(m_old, jnp.max(s, axis=-1, keepdims=True))
        p = jnp.exp2(s - lanes(m_new, tq))
        alpha = jnp.exp2(m_old - m_new)
        l_ref[...] = alpha * l_ref[...] + jnp.sum(p, axis=-1, keepdims=True)
        m_ref[...] = m_new
        return p.astype(BF16), alpha

    def scores(q, kj, slot):
        start = pl.multiple_of(kj * tq, tq)
        s_ref[slot, 0] = lax.dot_general(q[0], k1_ref[pl.ds(start, tq), :], (((1,), (1,)), ((), ())),
                                         preferred_element_type=F32)
        s_ref[slot, 1] = lax.dot_general(q[1], k2_ref[pl.ds(start, tq), :], (((1,), (1,)), ((), ())),
                                         preferred_element_type=F32)

    def consume(kj, slot, bias_ref):
        start = pl.multiple_of(kj * tq, tq)
        vblk = slabs(v_ref, pl.ds(start, tq))
        for stream, (m_ref, l_ref, a_ref) in enumerate(((m1_ref, l1_ref, a1_ref), (m2_ref, l2_ref, a2_ref))):
            s = s_ref[slot, stream]
            if bias_ref is not None:
                s = s + bias_ref[...]
            p, alpha = softmax_update(s, m_ref, l_ref)
            a_ref[...] = lanes(alpha, dv) * a_ref[...] + jnp.dot(p, vblk, preferred_element_type=F32)

    def add_previous_block_bias(kj, slot, qi):
        flag = jnp.where(kj == qi - 1, 1.0, 0.0)
        corner = flag * bs_ref[...]
        for stream in range(2):
            s_ref[slot, stream, :LANES, tq - LANES:] = s_ref[slot, stream, :LANES, tq - LANES:] + corner

    def pair_loop(q, qi, first):
        def pair_body(t, carry):
            kj = first + 2 * t
            scores(q, kj + 1, 1)
            consume(kj, 0, None)
            add_previous_block_bias(kj + 1, 1, qi)
            scores(q, kj + 2, 0)
            consume(kj + 1, 1, None)
            return carry

        lax.fori_loop(0, step, pair_body, 0)

    lv = lam_ref[...]
    lam = (jnp.exp(jnp.sum(lv[0:1] * lv[1:2], axis=-1, keepdims=True))
           - jnp.exp(jnp.sum(lv[2:3] * lv[3:4], axis=-1, keepdims=True)) + lam_init)

    def finalize(rows):
        o = a1_ref[...] / lanes(l1_ref[...], dv) - lam * (a2_ref[...] / lanes(l2_ref[...], dv))
        o = o * lax.rsqrt(jnp.mean(o * o, axis=-1, keepdims=True) + RMS_EPS_B) * sw_ref[...]
        o = o * (1.0 - lam_init) * _silu(slabs(z_ref, rows).astype(F32))
        o_ref[rows, :] = o.astype(o_ref.dtype)

    rows_even, rows_odd = slice(0, tq), slice(tq, 2 * tq)
    q_even = (q1_ref[rows_even, :], q2_ref[rows_even, :])
    q_odd = (q1_ref[rows_odd, :], q2_ref[rows_odd, :])
    qi_even, qi_odd = 2 * step, 2 * step + 1

    init_stats()
    scores(q_even, 0, 0)
    pair_loop(q_even, qi_even, 0)
    scores(q_odd, 0, 1)
    consume(qi_even, 0, bd_ref)
    finalize(rows_even)

    init_stats()
    add_previous_block_bias(0, 1, qi_odd)
    scores(q_odd, 1, 0)
    consume(0, 1, None)
    pair_loop(q_odd, qi_odd, 1)
    consume(qi_odd, 0, bd_ref)
    finalize(rows_odd)


def _diff_attention(h, brow, lam_vecs, subln_w, batch, heads, lam_init):
    M = h.shape[1]
    T = M // batch
    tq = min(ATT_TQ, T // 2)
    ns = T // (2 * tq)
    dh = HEAD_DIM
    dv = 2 * dh
    return pl.pallas_call(
        functools.partial(_attn_kernel, lam_init),
        grid=(batch, heads, ns),
        in_specs=[pl.BlockSpec((None, 2 * tq, dh), lambda b, hh, i: (2 * hh, b * ns + i, 0)),
                  pl.BlockSpec((None, 2 * tq, dh), lambda b, hh, i: (2 * hh + 1, b * ns + i, 0)),
                  pl.BlockSpec((None, T, dh), lambda b, hh, i: (2 * heads + 2 * hh, b, 0)),
                  pl.BlockSpec((None, T, dh), lambda b, hh, i: (2 * heads + 2 * hh + 1, b, 0)),
                  pl.BlockSpec((2, T, dh), lambda b, hh, i: (2 * heads + hh, b, 0)),
                  pl.BlockSpec((2, 2 * tq, dh), lambda b, hh, i: (3 * heads + hh, b * ns + i, 0)),
                  pl.BlockSpec((None, 1, tq), lambda b, hh, i: (hh, 0, 0)),
                  pl.BlockSpec((4, dh), lambda b, hh, i: (0, 0)),
                  pl.BlockSpec((1, dv), lambda b, hh, i: (0, 0))],
        out_specs=pl.BlockSpec((2 * tq, dv), lambda b, hh, i: (b * ns + i, hh)),
        out_shape=jax.ShapeDtypeStruct((M, heads * dv), BF16),
        scratch_shapes=[pltpu.VMEM((tq, LANES), F32), pltpu.VMEM((tq, LANES), F32), pltpu.VMEM((tq, dv), F32),
                        pltpu.VMEM((tq, LANES), F32), pltpu.VMEM((tq, LANES), F32), pltpu.VMEM((tq, dv), F32),
                        pltpu.VMEM((tq, tq), F32), pltpu.VMEM((LANES, LANES), F32),
                        pltpu.VMEM((2, 2, tq, tq), F32)],
        compiler_params=_cparams(("parallel", "parallel", "arbitrary")),
        name="diff_attention",
    )(h, h, h, h, h, h, brow, lam_vecs, subln_w.reshape(1, dv))


def kernel(x, ln_g, ln_b, rel_bias, a_w_in, a_conv_w, a_a_log, a_dt_bias, a_norm_w, a_w_out,
           b_w_in, b_lam_q1, b_lam_k1, b_lam_q2, b_lam_k2, b_subln_w, b_w_out):
    batch, T, D = x.shape
    M = batch * T
    depth = ln_g.shape[0]
    a_heads = a_a_log.shape[1]
    b_heads = rel_bias.shape[1]
    alpha = (2.0 * depth) ** 0.25
    att_tq = min(ATT_TQ, T // 2)
    assert T % min(GDN_TB, T) == 0 and T % (2 * att_tq) == 0 and att_tq >= LANES

    q_w = 2 * b_heads * HEAD_DIM
    n_in_b = b_w_in.shape[2]
    b_col_scale = jnp.where(jnp.arange(n_in_b) < q_w, HEAD_DIM ** -0.5 * LOG2_E, 1.0)
    ones_d = jnp.ones((D,), F32)
    cast_jobs = {}
    for i in range(depth):
        if i % 2 == 1:
            cast_jobs["b_in", i // 2] = (b_w_in, i // 2, b_col_scale)
        cast_jobs["out", i] = (a_w_out if i % 2 == 0 else b_w_out, i // 2, ones_d)
    bf16_weights = {}

    xf = x.reshape(M, D)
    xb = None
    for i in range(depth):
        j = i // 2
        if i % 2 == 0:
            qkvz_w = 4 * a_heads * HEAD_DIM
            chunk = min(GDN_CHUNK, GDN_TB, T)
            a_w_t = jnp.swapaxes(a_w_in, 1, 2)
            gc, beta, xb = _gates(xf, a_w_t, j, qkvz_w, a_a_log[j], a_dt_bias[j], chunk)
            jobs = list(cast_jobs.items()) if i == 0 else []
            h, cast = _matmul(xb, _cast_weight_transposed(a_w_t, j, qkvz_w), BF16,
                              [job for _, job in jobs])
            bf16_weights.update({name: wb for (name, _), wb in zip(jobs, cast)})
            o = _gated_delta_net(h, a_conv_w[j].T, gc, beta, a_norm_w[j], batch, a_heads)
        else:
            tq = att_tq
            h, _ = _matmul(xb, bf16_weights["b_in", j], BF16, slab_major=True)
            table = rel_bias[_t5_bucket_table(tq)] - rel_bias[REL_BUCKETS - 1][None, :]
            brow = jnp.roll(table[::-1], 1, axis=0).T.reshape(b_heads, 1, tq)
            lam_vecs = jnp.stack([b_lam_q1[j], b_lam_k1[j], b_lam_q2[j], b_lam_k2[j]])
            lam_init = 0.8 - 0.6 * math.exp(-0.3 * i)
            o = _diff_attention(h, brow, lam_vecs, b_subln_w[j], batch, b_heads, lam_init)
        r = _matmul_residual(o, bf16_weights["out", i], xf, alpha)
        xf, xb = _layer_norm(r, ln_g[i], ln_b[i], with_bf16=i + 1 < depth)
    return xf.reshape(batch, T, D)
```

```python
import functools
import math

import numpy as np
import jax
import jax.numpy as jnp
from jax import lax
from jax.experimental import pallas as pl
from jax.experimental.pallas import tpu as pltpu

F32 = jnp.float32
BF16 = jnp.bfloat16

LANES = 128
SUBLANES = 8
BF16_SUBLANES = 16
VMEM_LIMIT_BYTES = 56 * 1024 * 1024

CONV_TAPS = 4
HEAD_DIM = 128
LN_EPS = 1e-5
RMS_EPS_A = 1e-6
RMS_EPS_B = 1e-5
L2_EPS = 1e-6
REL_BUCKETS = 32
REL_MAX_DIST = 128
LOG2_E = math.log2(math.e)

MM_BM = 1024
MM_BN = 1024
GDN_CHUNK = 128
GDN_TB = 1024
GDN_HEADS_PER_STEP = 4
ATT_TQ = 512
ATT_PAIRS_PER_STEP = 2
LN_BM = 512
GATES_BM = 512
GATES_W_ROWS = 64
CAST_T_BN = 1024
CAST_T_BK = 1024


def _cparams(sem):
    return pltpu.CompilerParams(dimension_semantics=sem, vmem_limit_bytes=VMEM_LIMIT_BYTES)


def _silu(x):
    h = 0.5 * x
    return h + h * jnp.tanh(h)


def _cast_t_kernel(w_ref, o_ref):
    o_ref[...] = w_ref[...].T.astype(o_ref.dtype)


def _cast_weight_transposed(wt3, layer, nrows):
    K = wt3.shape[2]
    bn, bk = min(CAST_T_BN, nrows), min(CAST_T_BK, K)
    return pl.pallas_call(
        _cast_t_kernel,
        grid=(nrows // bn, K // bk),
        in_specs=[pl.BlockSpec((None, bn, bk), lambda n, k: (layer, n, k))],
        out_specs=pl.BlockSpec((bk, bn), lambda n, k: (k, n)),
        out_shape=jax.ShapeDtypeStruct((K, nrows), BF16),
        compiler_params=_cparams(("parallel", "parallel")),
        name="cast_weight_t",
    )(wt3)


def _mm_kernel(n_side, x_ref, w_ref, *refs):
    side_in, o_ref, side_out = refs[:2 * n_side], refs[2 * n_side], refs[2 * n_side + 1:]
    res = jnp.dot(x_ref[...], w_ref[...], preferred_element_type=F32).astype(o_ref.dtype)
    if len(o_ref.shape) == 3:
        for c in range(o_ref.shape[0]):
            o_ref[c] = res[:, c * LANES:(c + 1) * LANES]
    else:
        o_ref[...] = res
    for k in range(n_side):
        side_out[k][...] = (side_in[2 * k][...] * side_in[2 * k + 1][...]).astype(side_out[k].dtype)


def _matmul(x, w, out_dtype, side_casts=(), slab_major=False):
    M, K = x.shape
    N = w.shape[1]
    bm, bn = min(MM_BM, M), min(MM_BN, N)
    ni, nj = M // bm, N // bn
    side_args, side_in_specs, side_out_specs, side_out_shapes = [], [], [], []
    for w3, layer, col_scale in side_casts:
        ks, ns = w3.shape[1], w3.shape[2]
        rows = ks // (ni * nj)
        assert rows * ni * nj == ks and rows % BF16_SUBLANES == 0
        side_args += [w3, col_scale.reshape(1, ns).astype(F32)]
        side_in_specs += [pl.BlockSpec((None, rows, ns), lambda i, j, layer=layer: (layer, i * nj + j, 0)),
                          pl.BlockSpec((1, ns), lambda i, j: (0, 0))]
        side_out_specs.append(pl.BlockSpec((rows, ns), lambda i, j: (i * nj + j, 0)))
        side_out_shapes.append(jax.ShapeDtypeStruct((ks, ns), BF16))
    if slab_major:
        main_spec = pl.BlockSpec((bn // LANES, bm, LANES), lambda i, j: (j, i, 0))
        main_shape = jax.ShapeDtypeStruct((N // LANES, M, LANES), out_dtype)
    else:
        main_spec = pl.BlockSpec((bm, bn), lambda i, j: (i, j))
        main_shape = jax.ShapeDtypeStruct((M, N), out_dtype)
    out = pl.pallas_call(
        functools.partial(_mm_kernel, len(side_casts)),
        grid=(ni, nj),
        in_specs=[pl.BlockSpec((bm, K), lambda i, j: (i, 0)),
                  pl.BlockSpec((K, bn), lambda i, j: (0, j))] + side_in_specs,
        out_specs=[main_spec] + side_out_specs,
        out_shape=[main_shape] + side_out_shapes,
        compiler_params=_cparams(("parallel", "parallel")),
        name="in_proj",
    )(x, w, *side_args)
    return out[0], out[1:]


def _mm_res_kernel(alpha, o_ref, w_ref, x_ref, r_ref):
    y = jnp.dot(o_ref[...], w_ref[...], preferred_element_type=F32)
    r_ref[...] = alpha * x_ref[...] + y


def _matmul_residual(o, w, x, alpha):
    M, K = o.shape
    N = w.shape[1]
    bm, bn = min(MM_BM, M), min(MM_BN, N)
    return pl.pallas_call(
        functools.partial(_mm_res_kernel, alpha),
        grid=(M // bm, N // bn),
        in_specs=[pl.BlockSpec((bm, K), lambda i, j: (i, 0)),
                  pl.BlockSpec((K, bn), lambda i, j: (0, j)),
                  pl.BlockSpec((bm, bn), lambda i, j: (i, j))],
        out_specs=pl.BlockSpec((bm, bn), lambda i, j: (i, j)),
        out_shape=jax.ShapeDtypeStruct((M, N), F32),
        compiler_params=_cparams(("parallel", "parallel")),
        name="out_proj_residual",
    )(o, w, x)


def _ln_kernel(r_ref, g_ref, b_ref, o_ref, *ob_ref):
    r = r_ref[...]
    mu = jnp.mean(r, axis=-1, keepdims=True)
    d = r - mu
    var = jnp.mean(d * d, axis=-1, keepdims=True)
    y = d * lax.rsqrt(var + LN_EPS) * g_ref[...] + b_ref[...]
    o_ref[...] = y
    for ref in ob_ref:
        ref[...] = y.astype(ref.dtype)


def _layer_norm(r, g, b, with_bf16):
    M, D = r.shape
    bm = min(LN_BM, M)
    row_spec = pl.BlockSpec((bm, D), lambda i: (i, 0))
    out = pl.pallas_call(
        _ln_kernel,
        grid=(M // bm,),
        in_specs=[row_spec,
                  pl.BlockSpec((1, D), lambda i: (0, 0)),
                  pl.BlockSpec((1, D), lambda i: (0, 0))],
        out_specs=[row_spec] + [row_spec] * with_bf16,
        out_shape=[jax.ShapeDtypeStruct((M, D), F32)] + [jax.ShapeDtypeStruct((M, D), BF16)] * with_bf16,
        compiler_params=_cparams(("parallel",)),
        name="layer_norm",
    )(r, g.reshape(1, D), b.reshape(1, D))
    return out[0], (out[1] if with_bf16 else None)


def _gates_kernel(chunk, x_ref, w_ref, alog_ref, dtb_ref, gc_ref, beta_ref, xb_ref):
    nh = gc_ref.shape[0]
    xb = x_ref[...].astype(BF16)
    xb_ref[...] = xb
    row = lax.broadcasted_iota(jnp.int32, w_ref.shape, 0)
    w = jnp.where(row < 2 * nh, w_ref[...], 0.0).astype(BF16)
    ab = lax.dot_general(w, xb, (((1,), (1,)), ((), ())), preferred_element_type=F32)
    a = ab[:nh] + dtb_ref[...]
    b = ab[nh:2 * nh]
    softplus = jnp.maximum(a, 0.0) + jnp.log1p(jnp.exp(-jnp.abs(a)))
    g = -jnp.exp(alog_ref[...]) * softplus
    pos = lax.broadcasted_iota(jnp.int32, g.shape, 1) % chunk
    s = 1
    while s < chunk:
        g = g + jnp.where(pos >= s, pltpu.roll(g, s, axis=1), 0.0)
        s *= 2
    gc_ref[...] = g
    beta_ref[...] = 1.0 / (1.0 + jnp.exp(-b))


def _gates(x, wt3, layer, row0, a_log, dt_bias, chunk):
    M, K = x.shape
    nh = a_log.shape[0]
    bm = min(GATES_BM, M)
    assert row0 % GATES_W_ROWS == 0 and 2 * nh <= GATES_W_ROWS
    return pl.pallas_call(
        functools.partial(_gates_kernel, chunk),
        grid=(M // bm,),
        in_specs=[pl.BlockSpec((bm, K), lambda i: (i, 0)),
                  pl.BlockSpec((None, GATES_W_ROWS, K), lambda i: (layer, row0 // GATES_W_ROWS, 0)),
                  pl.BlockSpec((nh, 1), lambda i: (0, 0)),
                  pl.BlockSpec((nh, 1), lambda i: (0, 0))],
        out_specs=[pl.BlockSpec((nh, bm), lambda i: (0, i)),
                   pl.BlockSpec((nh, bm), lambda i: (0, i)),
                   pl.BlockSpec((bm, K), lambda i: (i, 0))],
        out_shape=[jax.ShapeDtypeStruct((nh, M), F32), jax.ShapeDtypeStruct((nh, M), F32),
                   jax.ShapeDtypeStruct((M, K), BF16)],
        compiler_params=_cparams(("parallel",)),
        name="gdn_gates",
    )(x, wt3, a_log.reshape(nh, 1), dt_bias.reshape(nh, 1))


def _dot(a, b):
    return jnp.dot(a.astype(BF16), b.astype(BF16), preferred_element_type=F32)


def _dot_nt(a, b):
    return lax.dot_general(a.astype(BF16), b.astype(BF16), (((1,), (1,)), ((), ())),
                           preferred_element_type=F32)


def _gdn_kernel(chunk, q_ref, k_ref, v_ref, z_ref, wq_ref, wk_ref, wv_ref, gc_ref, beta_ref,
                nw_ref, o_ref, s_ref, hq_ref, hk_ref, hv_ref):
    tb = q_ref.shape[0]
    dk = HEAD_DIM
    nheads = q_ref.shape[1] // dk
    halo = hq_ref.shape[0]
    nchunks = tb // chunk

    @pl.when(pl.program_id(2) == 0)
    def _():
        s_ref[...] = jnp.zeros_like(s_ref)
        hq_ref[...] = jnp.zeros_like(hq_ref)
        hk_ref[...] = jnp.zeros_like(hk_ref)
        hv_ref[...] = jnp.zeros_like(hv_ref)

    def conv_silu(x_ref, w_ref, h_ref):
        x = x_ref[...].astype(F32)
        xs = jnp.concatenate([h_ref[...], x], axis=0)
        w = w_ref[...]
        y = x * w[CONV_TAPS - 1:CONV_TAPS]
        for j in range(CONV_TAPS - 1):
            off = halo - (CONV_TAPS - 1) + j
            y = y + xs[off:off + tb] * w[j:j + 1]
        h_ref[...] = x[tb - halo:]
        return _silu(y)

    q_all = conv_silu(q_ref, wq_ref, hq_ref)
    k_all = conv_silu(k_ref, wk_ref, hk_ref)
    v_all = conv_silu(v_ref, wv_ref, hv_ref)

    ri = lax.broadcasted_iota(jnp.int32, (chunk, chunk), 0)
    ci = lax.broadcasted_iota(jnp.int32, (chunk, chunk), 1)
    incl = ri >= ci
    strict = ri > ci
    eye = (ri == ci).astype(F32)
    off_masks = []
    b = 1
    while b < chunk:
        off_masks.append((ri // (2 * b) == ci // (2 * b)) & (ri // b != ci // b))
        b *= 2

    chains = [(g, c) for g in range(nheads) for c in range(nchunks)]
    a_mat, attn, rhs, lhs_top, qd, cdec = {}, {}, {}, {}, {}, {}
    for g in range(nheads):
        hs = slice(g * dk, (g + 1) * dk)
        q = q_all[:, hs]
        k = k_all[:, hs]
        v = v_all[:, hs]
        q = q * (lax.rsqrt(jnp.sum(q * q, axis=-1, keepdims=True) + L2_EPS) * (dk ** -0.5))
        k = k * lax.rsqrt(jnp.sum(k * k, axis=-1, keepdims=True) + L2_EPS)
        gc_row = jnp.broadcast_to(gc_ref[g], (LANES, tb))
        gc_col = gc_row.T
        beta_col = jnp.broadcast_to(beta_ref[g], (LANES, tb)).T
        eg_col = jnp.exp(gc_col)
        for c in range(nchunks):
            sl = slice(c * chunk, (c + 1) * chunk)
            qc, kc, vc = q[sl], k[sl], v[sl]
            bc = beta_col[sl]
            gcc = gc_col[sl]
            kb = kc * bc
            diff = gcc[:, :chunk] - gc_row[:chunk, sl]
            decay = jnp.where(incl, jnp.exp(jnp.where(incl, diff, 0.0)), 0.0)
            qk = _dot_nt(jnp.concatenate([qc, kb], axis=0), kc)
            attn[g, c] = qk[:chunk] * decay
            a_mat[g, c] = jnp.where(strict, qk[chunk:] * decay, 0.0)
            rhs[g, c] = jnp.concatenate([kb * eg_col[sl], vc * bc], axis=1)
            g_last = gcc[chunk - 1:chunk, :]
            lhs_top[g, c] = (kc * jnp.exp(g_last - gcc)).T
            qd[g, c] = qc * eg_col[sl]
            cdec[g, c] = jnp.exp(g_last)

    x = {ch: eye - jnp.where(off_masks[0], a_mat[ch], 0.0) for ch in chains}
    for off_mask in off_masks[1:]:
        y = {ch: _dot(jnp.where(off_mask, a_mat[ch], 0.0), x[ch]) for ch in chains}
        x = {ch: x[ch] - _dot(x[ch], y[ch]) for ch in chains}
    wu = {ch: _dot(x[ch], rhs[ch]) for ch in chains}
    st = {ch: _dot(jnp.concatenate([lhs_top[ch], attn[ch]], axis=0), wu[ch]) for ch in chains}

    s = [s_ref[g] for g in range(nheads)]
    outs = {}
    for c in range(nchunks):
        for g in range(nheads):
            t = st[g, c]
            lhs = jnp.concatenate([-t[:dk, :dk], qd[g, c] - t[dk:, :dk]], axis=0)
            r = _dot(lhs, s[g])
            outs[g, c] = r[dk:] + t[dk:, dk:]
            s[g] = s[g] * cdec[g, c] + r[:dk] + t[:dk, dk:]
    for g in range(nheads):
        s_ref[g] = s[g]

    nw = nw_ref[...]
    cols = []
    for g in range(nheads):
        og = jnp.concatenate([outs[g, c] for c in range(nchunks)], axis=0)
        cols.append(og * lax.rsqrt(jnp.mean(og * og, axis=-1, keepdims=True) + RMS_EPS_A) * nw)
    o = jnp.concatenate(cols, axis=1)
    o_ref[...] = (o * _silu(z_ref[...].astype(F32))).astype(o_ref.dtype)


def _gated_delta_net(h, conv_w_t, gc, beta, norm_w, batch, heads):
    M = h.shape[0]
    T = M // batch
    tb = min(GDN_TB, T)
    nt = T // tb
    dk = HEAD_DIM
    chunk = min(GDN_CHUNK, tb)
    hps = min(GDN_HEADS_PER_STEP, heads)
    ng = heads // hps
    width = hps * dk

    def hspec(off):
        return pl.BlockSpec((tb, width), lambda b, hh, t: (b * nt + t, hh + off))

    def wspec(off):
        return pl.BlockSpec((CONV_TAPS, width), lambda b, hh, t: (0, hh + off))

    gspec = pl.BlockSpec((hps, 1, tb), lambda b, hh, t: (hh, 0, b * nt + t))
    return pl.pallas_call(
        functools.partial(_gdn_kernel, chunk),
        grid=(batch, ng, nt),
        in_specs=[hspec(0), hspec(ng), hspec(2 * ng), hspec(3 * ng),
                  wspec(0), wspec(ng), wspec(2 * ng), gspec, gspec,
                  pl.BlockSpec((1, dk), lambda b, hh, t: (0, 0))],
        out_specs=pl.BlockSpec((tb, width), lambda b, hh, t: (b * nt + t, hh)),
        out_shape=jax.ShapeDtypeStruct((M, heads * dk), BF16),
        scratch_shapes=[pltpu.VMEM((hps, dk, dk), F32),
                        pltpu.VMEM((SUBLANES, width), F32),
                        pltpu.VMEM((SUBLANES, width), F32),
                        pltpu.VMEM((SUBLANES, width), F32)],
        compiler_params=_cparams(("parallel", "parallel", "arbitrary")),
        name="gated_delta_rule",
    )(h, h, h, h, conv_w_t, conv_w_t, conv_w_t,
      gc.reshape(heads, 1, M), beta.reshape(heads, 1, M), norm_w.reshape(1, dk))


def _t5_bucket_table(n):
    rel = np.arange(n)
    max_exact = REL_BUCKETS // 2
    nf = np.maximum(rel, 1).astype(np.float32)
    large = max_exact + (np.log(nf / np.float32(max_exact)) / np.float32(math.log(REL_MAX_DIST / max_exact))
                         * np.float32(REL_BUCKETS - max_exact)).astype(np.int32)
    large = np.minimum(large, REL_BUCKETS - 1)
    return np.where(rel < max_exact, rel, large)


def _attn_kernel(lam_init, q1_ref, q2_ref, k1_ref, k2_ref, v_ref, z_ref, brow_ref, lam_ref, sw_ref,
                 o_ref, m1_ref, l1_ref, a1_ref, m2_ref, l2_ref, a2_ref, bd_ref, bs_ref, s_ref):
    tq = bd_ref.shape[0]
    dv = v_ref.shape[0] * LANES
    step = pl.program_id(2)

    def slabs(ref, rows):
        return jnp.concatenate([ref[c, rows, :] for c in range(ref.shape[0])], axis=1)

    @pl.when(step == 0)
    def _():
        r = pltpu.roll(jnp.broadcast_to(brow_ref[...], (tq, tq)), 0, axis=1, stride=1, stride_axis=0)
        r = r * LOG2_E
        ri = lax.broadcasted_iota(jnp.int32, (tq, tq), 0)
        ci = lax.broadcasted_iota(jnp.int32, (tq, tq), 1)
        bd_ref[...] = jnp.where(ri >= ci, r, -jnp.inf)
        bs_ref[...] = jnp.where(ri < ci, r, 0.0)[:LANES, tq - LANES:]

    def init_stats():
        m1_ref[...] = jnp.full_like(m1_ref, -jnp.inf)
        m2_ref[...] = jnp.full_like(m2_ref, -jnp.inf)
        l1_ref[...] = jnp.zeros_like(l1_ref)
        l2_ref[...] = jnp.zeros_like(l2_ref)
        a1_ref[...] = jnp.zeros_like(a1_ref)
        a2_ref[...] = jnp.zeros_like(a2_ref)

    def lanes(x, width):
        return jnp.concatenate([x] * (width // LANES), axis=1)

    def softmax_update(s, m_ref, l_ref):
        m_old = m_ref[...]
        m_new = jnp.maximum(m_old, jnp.max(s, axis=-1, keepdims=True))
        p = jnp.exp2(s - lanes(m_new, tq))
        alpha = jnp.exp2(m_old - m_new)
        l_ref[...] = alpha * l_ref[...] + jnp.sum(p, axis=-1, keepdims=True)
        m_ref[...] = m_new
        return p.astype(BF16), alpha

    def scores(q, kj, slot):
        start = pl.multiple_of(kj * tq, tq)
        s_ref[slot, 0] = lax.dot_general(q[0], k1_ref[pl.ds(start, tq), :], (((1,), (1,)), ((), ())),
                                         preferred_element_type=F32)
        s_ref[slot, 1] = lax.dot_general(q[1], k2_ref[pl.ds(start, tq), :], (((1,), (1,)), ((), ())),
                                         preferred_element_type=F32)

    def consume(kj, slot, bias_ref):
        start = pl.multiple_of(kj * tq, tq)
        vblk = slabs(v_ref, pl.ds(start, tq))
        for stream, (m_ref, l_ref, a_ref) in enumerate(((m1_ref, l1_ref, a1_ref), (m2_ref, l2_ref, a2_ref))):
            s = s_ref[slot, stream]
            if bias_ref is not None:
                s = s + bias_ref[...]
            p, alpha = softmax_update(s, m_ref, l_ref)
            a_ref[...] = lanes(alpha, dv) * a_ref[...] + jnp.dot(p, vblk, preferred_element_type=F32)

    def add_previous_block_bias(kj, slot, qi):
        flag = jnp.where(kj == qi - 1, 1.0, 0.0)
        corner = flag * bs_ref[...]
        for stream in range(2):
            s_ref[slot, stream, :LANES, tq - LANES:] = s_ref[slot, stream, :LANES, tq - LANES:] + corner

    def pair_loop(q, qi, first):
        def pair_body(t, carry):
            kj = first + 2 * t
            scores(q, kj + 1, 1)
            consume(kj, 0, None)
            add_previous_block_bias(kj + 1, 1, qi)
            scores(q, kj + 2, 0)
            consume(kj + 1, 1, None)
            return carry

        lax.fori_loop(0, qi // 2, pair_body, 0)

    lv = lam_ref[...]
    lam = (jnp.exp(jnp.sum(lv[0:1] * lv[1:2], axis=-1, keepdims=True))
           - jnp.exp(jnp.sum(lv[2:3] * lv[3:4], axis=-1, keepdims=True)) + lam_init)

    def finalize(rows):
        o = a1_ref[...] / lanes(l1_ref[...], dv) - lam * (a2_ref[...] / lanes(l2_ref[...], dv))
        o = o * lax.rsqrt(jnp.mean(o * o, axis=-1, keepdims=True) + RMS_EPS_B) * sw_ref[...]
        o = o * (1.0 - lam_init) * _silu(slabs(z_ref, rows).astype(F32))
        o_ref[rows, :] = o.astype(o_ref.dtype)

    for pair in range(q1_ref.shape[0] // (2 * tq)):
        base = 2 * pair * tq
        rows_even, rows_odd = slice(base, base + tq), slice(base + tq, base + 2 * tq)
        q_even = (q1_ref[rows_even, :], q2_ref[rows_even, :])
        q_odd = (q1_ref[rows_odd, :], q2_ref[rows_odd, :])
        qi_even = 2 * (step * (q1_ref.shape[0] // (2 * tq)) + pair)
        qi_odd = qi_even + 1

        init_stats()
        scores(q_even, 0, 0)
        pair_loop(q_even, qi_even, 0)
        scores(q_odd, 0, 1)
        consume(qi_even, 0, bd_ref)
        finalize(rows_even)

        init_stats()
        add_previous_block_bias(0, 1, qi_odd)
        scores(q_odd, 1, 0)
        consume(0, 1, None)
        pair_loop(q_odd, qi_odd, 1)
        consume(qi_odd, 0, bd_ref)
        finalize(rows_odd)


def _diff_attention(h, brow, lam_vecs, subln_w, batch, heads, lam_init):
    M = h.shape[1]
    T = M // batch
    tq = min(ATT_TQ, T // 2)
    rows = 2 * tq * min(ATT_PAIRS_PER_STEP, T // (2 * tq))
    ns = T // rows
    assert ns * rows == T
    dh = HEAD_DIM
    dv = 2 * dh
    return pl.pallas_call(
        functools.partial(_attn_kernel, lam_init),
        grid=(batch, heads, ns),
        in_specs=[pl.BlockSpec((None, rows, dh), lambda b, hh, i: (2 * hh, b * ns + i, 0)),
                  pl.BlockSpec((None, rows, dh), lambda b, hh, i: (2 * hh + 1, b * ns + i, 0)),
                  pl.BlockSpec((None, T, dh), lambda b, hh, i: (2 * heads + 2 * hh, b, 0)),
                  pl.BlockSpec((None, T, dh), lambda b, hh, i: (2 * heads + 2 * hh + 1, b, 0)),
                  pl.BlockSpec((2, T, dh), lambda b, hh, i: (2 * heads + hh, b, 0)),
                  pl.BlockSpec((2, rows, dh), lambda b, hh, i: (3 * heads + hh, b * ns + i, 0)),
                  pl.BlockSpec((None, 1, tq), lambda b, hh, i: (hh, 0, 0)),
                  pl.BlockSpec((4, dh), lambda b, hh, i: (0, 0)),
                  pl.BlockSpec((1, dv), lambda b, hh, i: (0, 0))],
        out_specs=pl.BlockSpec((rows, dv), lambda b, hh, i: (b * ns + i, hh)),
        out_shape=jax.ShapeDtypeStruct((M, heads * dv), BF16),
        scratch_shapes=[pltpu.VMEM((tq, LANES), F32), pltpu.VMEM((tq, LANES), F32), pltpu.VMEM((tq, dv), F32),
                        pltpu.VMEM((tq, LANES), F32), pltpu.VMEM((tq, LANES), F32), pltpu.VMEM((tq, dv), F32),
                        pltpu.VMEM((tq, tq), F32), pltpu.VMEM((LANES, LANES), F32),
                        pltpu.VMEM((2, 2, tq, tq), F32)],
        compiler_params=_cparams(("parallel", "parallel", "arbitrary")),
        name="diff_attention",
    )(h, h, h, h, h, h, brow, lam_vecs, subln_w.reshape(1, dv))


def kernel(x, ln_g, ln_b, rel_bias, a_w_in, a_conv_w, a_a_log, a_dt_bias, a_norm_w, a_w_out,
           b_w_in, b_lam_q1, b_lam_k1, b_lam_q2, b_lam_k2, b_subln_w, b_w_out):
    batch, T, D = x.shape
    M = batch * T
    depth = ln_g.shape[0]
    a_heads = a_a_log.shape[1]
    b_heads = rel_bias.shape[1]
    alpha = (2.0 * depth) ** 0.25
    att_tq = min(ATT_TQ, T // 2)
    assert T % min(GDN_TB, T) == 0 and T % (2 * att_tq) == 0 and att_tq >= LANES

    q_w = 2 * b_heads * HEAD_DIM
    n_in_b = b_w_in.shape[2]
    b_col_scale = jnp.where(jnp.arange(n_in_b) < q_w, HEAD_DIM ** -0.5 * LOG2_E, 1.0)
    ones_d = jnp.ones((D,), F32)
    cast_jobs = {}
    for i in range(depth):
        if i % 2 == 1:
            cast_jobs["b_in", i // 2] = (b_w_in, i // 2, b_col_scale)
        cast_jobs["out", i] = (a_w_out if i % 2 == 0 else b_w_out, i // 2, ones_d)
    bf16_weights = {}

    xf = x.reshape(M, D)
    xb = None
    for i in range(depth):
        j = i // 2
        if i % 2 == 0:
            qkvz_w = 4 * a_heads * HEAD_DIM
            chunk = min(GDN_CHUNK, GDN_TB, T)
            a_w_t = jnp.swapaxes(a_w_in, 1, 2)
            gc, beta, xb = _gates(xf, a_w_t, j, qkvz_w, a_a_log[j], a_dt_bias[j], chunk)
            jobs = list(cast_jobs.items()) if i == 0 else []
            h, cast = _matmul(xb, _cast_weight_transposed(a_w_t, j, qkvz_w), BF16,
                              [job for _, job in jobs])
            bf16_weights.update({name: wb for (name, _), wb in zip(jobs, cast)})
            o = _gated_delta_net(h, a_conv_w[j].T, gc, beta, a_norm_w[j], batch, a_heads)
        else:
            tq = att_tq
            h, _ = _matmul(xb, bf16_weights["b_in", j], BF16, slab_major=True)
            table = rel_bias[_t5_bucket_table(tq)] - rel_bias[REL_BUCKETS - 1][None, :]
            brow = jnp.roll(table[::-1], 1, axis=0).T.reshape(b_heads, 1, tq)
            lam_vecs = jnp.stack([b_lam_q1[j], b_lam_k1[j], b_lam_q2[j], b_lam_k2[j]])
            lam_init = 0.8 - 0.6 * math.exp(-0.3 * i)
            o = _diff_attention(h, brow, lam_vecs, b_subln_w[j], batch, b_heads, lam_init)
        r = _matmul_residual(o, bf16_weights["out", i], xf, alpha)
        xf, xb = _layer_norm(r, ln_g[i], ln_b[i], with_bf16=i + 1 < depth)
    return xf.reshape(batch, T, D)
```

```python
import functools
import math

import numpy as np
import jax
import jax.numpy as jnp
from jax import lax
from jax.experimental import pallas as pl
from jax.experimental.pallas import tpu as pltpu

F32 = jnp.float32
BF16 = jnp.bfloat16

LANES = 128
SUBLANES = 8
BF16_SUBLANES = 16
VMEM_LIMIT_BYTES = 56 * 1024 * 1024

CONV_TAPS = 4
HEAD_DIM = 128
LN_EPS = 1e-5
RMS_EPS_A = 1e-6
RMS_EPS_B = 1e-5
L2_EPS = 1e-6
REL_BUCKETS = 32
REL_MAX_DIST = 128
LOG2_E = math.log2(math.e)

MM_BM = 1024
MM_BN = 1024
GDN_CHUNK = 128
GDN_TB = 2048
GDN_HEADS_PER_STEP = 4
ATT_TQ = 512
ATT_PAIRS_PER_STEP = 4
LN_BM = 512
GATES_BM = 512
GATES_W_ROWS = 64
CAST_T_BN = 1024
CAST_T_BK = 1024


def _cparams(sem):
    return pltpu.CompilerParams(dimension_semantics=sem, vmem_limit_bytes=VMEM_LIMIT_BYTES)


def _silu(x):
    h = 0.5 * x
    return h + h * jnp.tanh(h)


def _cast_t_kernel(w_ref, o_ref):
    o_ref[...] = w_ref[...].T.astype(o_ref.dtype)


def _cast_weight_transposed(wt3, layer, nrows):
    K = wt3.shape[2]
    bn, bk = min(CAST_T_BN, nrows), min(CAST_T_BK, K)
    return pl.pallas_call(
        _cast_t_kernel,
        grid=(nrows // bn, K // bk),
        in_specs=[pl.BlockSpec((None, bn, bk), lambda n, k: (layer, n, k))],
        out_specs=pl.BlockSpec((bk, bn), lambda n, k: (k, n)),
        out_shape=jax.ShapeDtypeStruct((K, nrows), BF16),
        compiler_params=_cparams(("parallel", "parallel")),
        name="cast_weight_t",
    )(wt3)


def _mm_kernel(n_side, x_ref, w_ref, *refs):
    side_in, o_ref, side_out = refs[:2 * n_side], refs[2 * n_side], refs[2 * n_side + 1:]
    res = jnp.dot(x_ref[...], w_ref[...], preferred_element_type=F32).astype(o_ref.dtype)
    if len(o_ref.shape) == 3:
        for c in range(o_ref.shape[0]):
            o_ref[c] = res[:, c * LANES:(c + 1) * LANES]
    else:
        o_ref[...] = res
    for k in range(n_side):
        side_out[k][...] = (side_in[2 * k][...] * side_in[2 * k + 1][...]).astype(side_out[k].dtype)


def _matmul(x, w, out_dtype, side_casts=(), slab_major=False):
    M, K = x.shape
    N = w.shape[1]
    bm, bn = min(MM_BM, M), min(MM_BN, N)
    ni, nj = M // bm, N // bn
    side_args, side_in_specs, side_out_specs, side_out_shapes = [], [], [], []
    for w3, layer, col_scale in side_casts:
        ks, ns = w3.shape[1], w3.shape[2]
        rows = ks // (ni * nj)
        assert rows * ni * nj == ks and rows % BF16_SUBLANES == 0
        side_args += [w3, col_scale.reshape(1, ns).astype(F32)]
        side_in_specs += [pl.BlockSpec((None, rows, ns), lambda i, j, layer=layer: (layer, i * nj + j, 0)),
                          pl.BlockSpec((1, ns), lambda i, j: (0, 0))]
        side_out_specs.append(pl.BlockSpec((rows, ns), lambda i, j: (i * nj + j, 0)))
        side_out_shapes.append(jax.ShapeDtypeStruct((ks, ns), BF16))
    if slab_major:
        main_spec = pl.BlockSpec((bn // LANES, bm, LANES), lambda i, j: (j, i, 0))
        main_shape = jax.ShapeDtypeStruct((N // LANES, M, LANES), out_dtype)
    else:
        main_spec = pl.BlockSpec((bm, bn), lambda i, j: (i, j))
        main_shape = jax.ShapeDtypeStruct((M, N), out_dtype)
    out = pl.pallas_call(
        functools.partial(_mm_kernel, len(side_casts)),
        grid=(ni, nj),
        in_specs=[pl.BlockSpec((bm, K), lambda i, j: (i, 0)),
                  pl.BlockSpec((K, bn), lambda i, j: (0, j))] + side_in_specs,
        out_specs=[main_spec] + side_out_specs,
        out_shape=[main_shape] + side_out_shapes,
        compiler_params=_cparams(("parallel", "parallel")),
        name="in_proj",
    )(x, w, *side_args)
    return out[0], out[1:]


def _mm_res_kernel(alpha, o_ref, w_ref, x_ref, r_ref):
    y = jnp.dot(o_ref[...], w_ref[...], preferred_element_type=F32)
    r_ref[...] = alpha * x_ref[...] + y


def _matmul_residual(o, w, x, alpha):
    M, K = o.shape
    N = w.shape[1]
    bm, bn = min(MM_BM, M), min(MM_BN, N)
    return pl.pallas_call(
        functools.partial(_mm_res_kernel, alpha),
        grid=(M // bm, N // bn),
        in_specs=[pl.BlockSpec((bm, K), lambda i, j: (i, 0)),
                  pl.BlockSpec((K, bn), lambda i, j: (0, j)),
                  pl.BlockSpec((bm, bn), lambda i, j: (i, j))],
        out_specs=pl.BlockSpec((bm, bn), lambda i, j: (i, j)),
        out_shape=jax.ShapeDtypeStruct((M, N), F32),
        compiler_params=_cparams(("parallel", "parallel")),
        name="out_proj_residual",
    )(o, w, x)


def _ln_kernel(r_ref, g_ref, b_ref, o_ref, *ob_ref):
    r = r_ref[...]
    mu = jnp.mean(r, axis=-1, keepdims=True)
    d = r - mu
    var = jnp.mean(d * d, axis=-1, keepdims=True)
    y = d * lax.rsqrt(var + LN_EPS) * g_ref[...] + b_ref[...]
    o_ref[...] = y
    for ref in ob_ref:
        ref[...] = y.astype(ref.dtype)


def _layer_norm(r, g, b, with_bf16):
    M, D = r.shape
    bm = min(LN_BM, M)
    row_spec = pl.BlockSpec((bm, D), lambda i: (i, 0))
    out = pl.pallas_call(
        _ln_kernel,
        grid=(M // bm,),
        in_specs=[row_spec,
                  pl.BlockSpec((1, D), lambda i: (0, 0)),
                  pl.BlockSpec((1, D), lambda i: (0, 0))],
        out_specs=[row_spec] + [row_spec] * with_bf16,
        out_shape=[jax.ShapeDtypeStruct((M, D), F32)] + [jax.ShapeDtypeStruct((M, D), BF16)] * with_bf16,
        compiler_params=_cparams(("parallel",)),
        name="layer_norm",
    )(r, g.reshape(1, D), b.reshape(1, D))
    return out[0], (out[1] if with_bf16 else None)


def _gates_kernel(chunk, x_ref, w_ref, alog_ref, dtb_ref, gc_ref, beta_ref, xb_ref):
    nh = gc_ref.shape[0]
    xb = x_ref[...].astype(BF16)
    xb_ref[...] = xb
    row = lax.broadcasted_iota(jnp.int32, w_ref.shape, 0)
    w = jnp.where(row < 2 * nh, w_ref[...], 0.0).astype(BF16)
    ab = lax.dot_general(w, xb, (((1,), (1,)), ((), ())), preferred_element_type=F32)
    a = ab[:nh] + dtb_ref[...]
    b = ab[nh:2 * nh]
    softplus = jnp.maximum(a, 0.0) + jnp.log1p(jnp.exp(-jnp.abs(a)))
    g = -jnp.exp(alog_ref[...]) * softplus
    pos = lax.broadcasted_iota(jnp.int32, g.shape, 1) % chunk
    s = 1
    while s < chunk:
        g = g + jnp.where(pos >= s, pltpu.roll(g, s, axis=1), 0.0)
        s *= 2
    gc_ref[...] = g
    beta_ref[...] = 1.0 / (1.0 + jnp.exp(-b))


def _gates(x, wt3, layer, row0, a_log, dt_bias, chunk):
    M, K = x.shape
    nh = a_log.shape[0]
    bm = min(GATES_BM, M)
    assert row0 % GATES_W_ROWS == 0 and 2 * nh <= GATES_W_ROWS
    return pl.pallas_call(
        functools.partial(_gates_kernel, chunk),
        grid=(M // bm,),
        in_specs=[pl.BlockSpec((bm, K), lambda i: (i, 0)),
                  pl.BlockSpec((None, GATES_W_ROWS, K), lambda i: (layer, row0 // GATES_W_ROWS, 0)),
                  pl.BlockSpec((nh, 1), lambda i: (0, 0)),
                  pl.BlockSpec((nh, 1), lambda i: (0, 0))],
        out_specs=[pl.BlockSpec((nh, bm), lambda i: (0, i)),
                   pl.BlockSpec((nh, bm), lambda i: (0, i)),
                   pl.BlockSpec((bm, K), lambda i: (i, 0))],
        out_shape=[jax.ShapeDtypeStruct((nh, M), F32), jax.ShapeDtypeStruct((nh, M), F32),
                   jax.ShapeDtypeStruct((M, K), BF16)],
        compiler_params=_cparams(("parallel",)),
        name="gdn_gates",
    )(x, wt3, a_log.reshape(nh, 1), dt_bias.reshape(nh, 1))


def _dot(a, b):
    return jnp.dot(a.astype(BF16), b.astype(BF16), preferred_element_type=F32)


def _dot_nt(a, b):
    return lax.dot_general(a.astype(BF16), b.astype(BF16), (((1,), (1,)), ((), ())),
                           preferred_element_type=F32)


def _gdn_kernel(chunk, q_ref, k_ref, v_ref, z_ref, wq_ref, wk_ref, wv_ref, gc_ref, beta_ref,
                nw_ref, o_ref, s_ref, hq_ref, hk_ref, hv_ref):
    tb = q_ref.shape[0]
    dk = HEAD_DIM
    nheads = q_ref.shape[1] // dk
    halo = hq_ref.shape[0]
    nchunks = tb // chunk

    @pl.when(pl.program_id(2) == 0)
    def _():
        s_ref[...] = jnp.zeros_like(s_ref)
        hq_ref[...] = jnp.zeros_like(hq_ref)
        hk_ref[...] = jnp.zeros_like(hk_ref)
        hv_ref[...] = jnp.zeros_like(hv_ref)

    def conv_silu(x_ref, w_ref, h_ref):
        x = x_ref[...].astype(F32)
        xs = jnp.concatenate([h_ref[...], x], axis=0)
        w = w_ref[...]
        y = x * w[CONV_TAPS - 1:CONV_TAPS]
        for j in range(CONV_TAPS - 1):
            off = halo - (CONV_TAPS - 1) + j
            y = y + xs[off:off + tb] * w[j:j + 1]
        h_ref[...] = x[tb - halo:]
        return _silu(y)

    q_all = conv_silu(q_ref, wq_ref, hq_ref)
    k_all = conv_silu(k_ref, wk_ref, hk_ref)
    v_all = conv_silu(v_ref, wv_ref, hv_ref)

    ri = lax.broadcasted_iota(jnp.int32, (chunk, chunk), 0)
    ci = lax.broadcasted_iota(jnp.int32, (chunk, chunk), 1)
    incl = ri >= ci
    strict = ri > ci
    eye = (ri == ci).astype(F32)
    off_masks = []
    b = 1
    while b < chunk:
        off_masks.append((ri // (2 * b) == ci // (2 * b)) & (ri // b != ci // b))
        b *= 2

    chains = [(g, c) for g in range(nheads) for c in range(nchunks)]
    a_mat, attn, rhs, lhs_top, qd, cdec = {}, {}, {}, {}, {}, {}
    for g in range(nheads):
        hs = slice(g * dk, (g + 1) * dk)
        q = q_all[:, hs]
        k = k_all[:, hs]
        v = v_all[:, hs]
        q = q * (lax.rsqrt(jnp.sum(q * q, axis=-1, keepdims=True) + L2_EPS) * (dk ** -0.5))
        k = k * lax.rsqrt(jnp.sum(k * k, axis=-1, keepdims=True) + L2_EPS)
        gc_row = jnp.broadcast_to(gc_ref[g], (LANES, tb))
        gc_col = gc_row.T
        beta_col = jnp.broadcast_to(beta_ref[g], (LANES, tb)).T
        eg_col = jnp.exp(gc_col)
        for c in range(nchunks):
            sl = slice(c * chunk, (c + 1) * chunk)
            qc, kc, vc = q[sl], k[sl], v[sl]
            bc = beta_col[sl]
            gcc = gc_col[sl]
            kb = kc * bc
            diff = gcc[:, :chunk] - gc_row[:chunk, sl]
            decay = jnp.where(incl, jnp.exp(jnp.where(incl, diff, 0.0)), 0.0)
            qk = _dot_nt(jnp.concatenate([qc, kb], axis=0), kc)
            attn[g, c] = qk[:chunk] * decay
            a_mat[g, c] = jnp.where(strict, qk[chunk:] * decay, 0.0)
            rhs[g, c] = jnp.concatenate([kb * eg_col[sl], vc * bc], axis=1)
            g_last = gcc[chunk - 1:chunk, :]
            lhs_top[g, c] = (kc * jnp.exp(g_last - gcc)).T
            qd[g, c] = qc * eg_col[sl]
            cdec[g, c] = jnp.exp(g_last)

    x = {ch: eye - jnp.where(off_masks[0], a_mat[ch], 0.0) for ch in chains}
    for off_mask in off_masks[1:]:
        y = {ch: _dot(jnp.where(off_mask, a_mat[ch], 0.0), x[ch]) for ch in chains}
        x = {ch: x[ch] - _dot(x[ch], y[ch]) for ch in chains}
    wu = {ch: _dot(x[ch], rhs[ch]) for ch in chains}
    st = {ch: _dot(jnp.concatenate([lhs_top[ch], attn[ch]], axis=0), wu[ch]) for ch in chains}

    s = [s_ref[g] for g in range(nheads)]
    outs = {}
    for c in range(nchunks):
        for g in range(nheads):
            t = st[g, c]
            lhs = jnp.concatenate([-t[:dk, :dk], qd[g, c] - t[dk:, :dk]], axis=0)
            r = _dot(lhs, s[g])
            outs[g, c] = r[dk:] + t[dk:, dk:]
            s[g] = s[g] * cdec[g, c] + r[:dk] + t[:dk, dk:]
    for g in range(nheads):
        s_ref[g] = s[g]

    nw = nw_ref[...]
    cols = []
    for g in range(nheads):
        og = jnp.concatenate([outs[g, c] for c in range(nchunks)], axis=0)
        cols.append(og * lax.rsqrt(jnp.mean(og * og, axis=-1, keepdims=True) + RMS_EPS_A) * nw)
    o = jnp.concatenate(cols, axis=1)
    o_ref[...] = (o * _silu(z_ref[...].astype(F32))).astype(o_ref.dtype)


def _gated_delta_net(h, conv_w_t, gc, beta, norm_w, batch, heads):
    M = h.shape[0]
    T = M // batch
    tb = min(GDN_TB, T)
    nt = T // tb
    dk = HEAD_DIM
    chunk = min(GDN_CHUNK, tb)
    hps = min(GDN_HEADS_PER_STEP, heads)
    ng = heads // hps
    width = hps * dk

    def hspec(off):
        return pl.BlockSpec((tb, width), lambda b, hh, t: (b * nt + t, hh + off))

    def wspec(off):
        return pl.BlockSpec((CONV_TAPS, width), lambda b, hh, t: (0, hh + off))

    gspec = pl.BlockSpec((hps, 1, tb), lambda b, hh, t: (hh, 0, b * nt + t))
    return pl.pallas_call(
        functools.partial(_gdn_kernel, chunk),
        grid=(batch, ng, nt),
        in_specs=[hspec(0), hspec(ng), hspec(2 * ng), hspec(3 * ng),
                  wspec(0), wspec(ng), wspec(2 * ng), gspec, gspec,
                  pl.BlockSpec((1, dk), lambda b, hh, t: (0, 0))],
        out_specs=pl.BlockSpec((tb, width), lambda b, hh, t: (b * nt + t, hh)),
        out_shape=jax.ShapeDtypeStruct((M, heads * dk), BF16),
        scratch_shapes=[pltpu.VMEM((hps, dk, dk), F32),
                        pltpu.VMEM((SUBLANES, width), F32),
                        pltpu.VMEM((SUBLANES, width), F32),
                        pltpu.VMEM((SUBLANES, width), F32)],
        compiler_params=_cparams(("parallel", "parallel", "arbitrary")),
        name="gated_delta_rule",
    )(h, h, h, h, conv_w_t, conv_w_t, conv_w_t,
      gc.reshape(heads, 1, M), beta.reshape(heads, 1, M), norm_w.reshape(1, dk))


def _t5_bucket_table(n):
    rel = np.arange(n)
    max_exact = REL_BUCKETS // 2
    nf = np.maximum(rel, 1).astype(np.float32)
    large = max_exact + (np.log(nf / np.float32(max_exact)) / np.float32(math.log(REL_MAX_DIST / max_exact))
                         * np.float32(REL_BUCKETS - max_exact)).astype(np.int32)
    large = np.minimum(large, REL_BUCKETS - 1)
    return np.where(rel < max_exact, rel, large)


def _attn_kernel(lam_init, q1_ref, q2_ref, k1_ref, k2_ref, v_ref, z_ref, brow_ref, lam_ref, sw_ref,
                 o_ref, m1_ref, l1_ref, a1_ref, m2_ref, l2_ref, a2_ref, bd_ref, bs_ref, s_ref):
    tq = bd_ref.shape[0]
    dv = v_ref.shape[0] * LANES
    step = pl.program_id(2)

    def slabs(ref, rows):
        return jnp.concatenate([ref[c, rows, :] for c in range(ref.shape[0])], axis=1)

    @pl.when(step == 0)
    def _():
        r = pltpu.roll(jnp.broadcast_to(brow_ref[...], (tq, tq)), 0, axis=1, stride=1, stride_axis=0)
        r = r * LOG2_E
        ri = lax.broadcasted_iota(jnp.int32, (tq, tq), 0)
        ci = lax.broadcasted_iota(jnp.int32, (tq, tq), 1)
        bd_ref[...] = jnp.where(ri >= ci, r, -jnp.inf)
        bs_ref[...] = jnp.where(ri < ci, r, 0.0)[:LANES, tq - LANES:]

    def init_stats():
        m1_ref[...] = jnp.full_like(m1_ref, -jnp.inf)
        m2_ref[...] = jnp.full_like(m2_ref, -jnp.inf)
        l1_ref[...] = jnp.zeros_like(l1_ref)
        l2_ref[...] = jnp.zeros_like(l2_ref)
        a1_ref[...] = jnp.zeros_like(a1_ref)
        a2_ref[...] = jnp.zeros_like(a2_ref)

    def lanes(x, width):
        return jnp.concatenate([x] * (width // LANES), axis=1)

    def softmax_update(s, m_ref, l_ref):
        m_old = m_ref[...]
        m_new = jnp.maximum(m_old, jnp.max(s, axis=-1, keepdims=True))
        p = jnp.exp2(s - lanes(m_new, tq))
        alpha = jnp.exp2(m_old - m_new)
        l_ref[...] = alpha * l_ref[...] + jnp.sum(p, axis=-1, keepdims=True)
        m_ref[...] = m_new
        return p.astype(BF16), alpha

    def scores(q, kj, slot):
        start = pl.multiple_of(kj * tq, tq)
        s_ref[slot, 0] = lax.dot_general(q[0], k1_ref[pl.ds(start, tq), :], (((1,), (1,)), ((), ())),
                                         preferred_element_type=F32)
        s_ref[slot, 1] = lax.dot_general(q[1], k2_ref[pl.ds(start, tq), :], (((1,), (1,)), ((), ())),
                                         preferred_element_type=F32)

    def consume(kj, slot, bias_ref):
        start = pl.multiple_of(kj * tq, tq)
        vblk = slabs(v_ref, pl.ds(start, tq))
        for stream, (m_ref, l_ref, a_ref) in enumerate(((m1_ref, l1_ref, a1_ref), (m2_ref, l2_ref, a2_ref))):
            s = s_ref[slot, stream]
            if bias_ref is not None:
                s = s + bias_ref[...]
            p, alpha = softmax_update(s, m_ref, l_ref)
            a_ref[...] = lanes(alpha, dv) * a_ref[...] + jnp.dot(p, vblk, preferred_element_type=F32)

    def add_previous_block_bias(kj, slot, qi):
        flag = jnp.where(kj == qi - 1, 1.0, 0.0)
        corner = flag * bs_ref[...]
        for stream in range(2):
            s_ref[slot, stream, :LANES, tq - LANES:] = s_ref[slot, stream, :LANES, tq - LANES:] + corner

    def pair_loop(q, qi, first):
        def pair_body(t, carry):
            kj = first + 2 * t
            scores(q, kj + 1, 1)
            consume(kj, 0, None)
            add_previous_block_bias(kj + 1, 1, qi)
            scores(q, kj + 2, 0)
            consume(kj + 1, 1, None)
            return carry

        lax.fori_loop(0, qi // 2, pair_body, 0)

    lv = lam_ref[...]
    lam = (jnp.exp(jnp.sum(lv[0:1] * lv[1:2], axis=-1, keepdims=True))
           - jnp.exp(jnp.sum(lv[2:3] * lv[3:4], axis=-1, keepdims=True)) + lam_init)

    def finalize(rows):
        o = a1_ref[...] / lanes(l1_ref[...], dv) - lam * (a2_ref[...] / lanes(l2_ref[...], dv))
        o = o * lax.rsqrt(jnp.mean(o * o, axis=-1, keepdims=True) + RMS_EPS_B) * sw_ref[...]
        o = o * (1.0 - lam_init) * _silu(slabs(z_ref, rows).astype(F32))
        o_ref[rows, :] = o.astype(o_ref.dtype)

    for pair in range(q1_ref.shape[0] // (2 * tq)):
        base = 2 * pair * tq
        rows_even, rows_odd = slice(base, base + tq), slice(base + tq, base + 2 * tq)
        q_even = (q1_ref[rows_even, :], q2_ref[rows_even, :])
        q_odd = (q1_ref[rows_odd, :], q2_ref[rows_odd, :])
        qi_even = 2 * (step * (q1_ref.shape[0] // (2 * tq)) + pair)
        qi_odd = qi_even + 1

        init_stats()
        scores(q_even, 0, 0)
        pair_loop(q_even, qi_even, 0)
        scores(q_odd, 0, 1)
        consume(qi_even, 0, bd_ref)
        finalize(rows_even)

        init_stats()
        add_previous_block_bias(0, 1, qi_odd)
        scores(q_odd, 1, 0)
        consume(0, 1, None)
        pair_loop(q_odd, qi_odd, 1)
        consume(qi_odd, 0, bd_ref)
        finalize(rows_odd)


def _diff_attention(h, brow, lam_vecs, subln_w, batch, heads, lam_init):
    M = h.shape[1]
    T = M // batch
    tq = min(ATT_TQ, T // 2)
    rows = 2 * tq * min(ATT_PAIRS_PER_STEP, T // (2 * tq))
    ns = T // rows
    assert ns * rows == T
    dh = HEAD_DIM
    dv = 2 * dh
    return pl.pallas_call(
        functools.partial(_attn_kernel, lam_init),
        grid=(batch, heads, ns),
        in_specs=[pl.BlockSpec((None, rows, dh), lambda b, hh, i: (2 * hh, b * ns + i, 0)),
                  pl.BlockSpec((None, rows, dh), lambda b, hh, i: (2 * hh + 1, b * ns + i, 0)),
                  pl.BlockSpec((None, T, dh), lambda b, hh, i: (2 * heads + 2 * hh, b, 0)),
                  pl.BlockSpec((None, T, dh), lambda b, hh, i: (2 * heads + 2 * hh + 1, b, 0)),
                  pl.BlockSpec((2, T, dh), lambda b, hh, i: (2 * heads + hh, b, 0)),
                  pl.BlockSpec((2, rows, dh), lambda b, hh, i: (3 * heads + hh, b * ns + i, 0)),
                  pl.BlockSpec((None, 1, tq), lambda b, hh, i: (hh, 0, 0)),
                  pl.BlockSpec((4, dh), lambda b, hh, i: (0, 0)),
                  pl.BlockSpec((1, dv), lambda b, hh, i: (0, 0))],
        out_specs=pl.BlockSpec((rows, dv), lambda b, hh, i: (b * ns + i, hh)),
        out_shape=jax.ShapeDtypeStruct((M, heads * dv), BF16),
        scratch_shapes=[pltpu.VMEM((tq, LANES), F32), pltpu.VMEM((tq, LANES), F32), pltpu.VMEM((tq, dv), F32),
                        pltpu.VMEM((tq, LANES), F32), pltpu.VMEM((tq, LANES), F32), pltpu.VMEM((tq, dv), F32),
                        pltpu.VMEM((tq, tq), F32), pltpu.VMEM((LANES, LANES), F32),
                        pltpu.VMEM((2, 2, tq, tq), F32)],
        compiler_params=_cparams(("parallel", "parallel", "arbitrary")),
        name="diff_attention",
    )(h, h, h, h, h, h, brow, lam_vecs, subln_w.reshape(1, dv))


def kernel(x, ln_g, ln_b, rel_bias, a_w_in, a_conv_w, a_a_log, a_dt_bias, a_norm_w, a_w_out,
           b_w_in, b_lam_q1, b_lam_k1, b_lam_q2, b_lam_k2, b_subln_w, b_w_out):
    batch, T, D = x.shape
    M = batch * T
    depth = ln_g.shape[0]
    a_heads = a_a_log.shape[1]
    b_heads = rel_bias.shape[1]
    alpha = (2.0 * depth) ** 0.25
    att_tq = min(ATT_TQ, T // 2)
    assert T % min(GDN_TB, T) == 0 and T % (2 * att_tq) == 0 and att_tq >= LANES

    q_w = 2 * b_heads * HEAD_DIM
    n_in_b = b_w_in.shape[2]
    b_col_scale = jnp.where(jnp.arange(n_in_b) < q_w, HEAD_DIM ** -0.5 * LOG2_E, 1.0)
    ones_d = jnp.ones((D,), F32)
    cast_jobs = {}
    for i in range(depth):
        if i % 2 == 1:
            cast_jobs["b_in", i // 2] = (b_w_in, i // 2, b_col_scale)
        cast_jobs["out", i] = (a_w_out if i % 2 == 0 else b_w_out, i // 2, ones_d)
    bf16_weights = {}

    xf = x.reshape(M, D)
    xb = None
    for i in range(depth):
        j = i // 2
        if i % 2 == 0:
            qkvz_w = 4 * a_heads * HEAD_DIM
            chunk = min(GDN_CHUNK, GDN_TB, T)
            a_w_t = jnp.swapaxes(a_w_in, 1, 2)
            gc, beta, xb = _gates(xf, a_w_t, j, qkvz_w, a_a_log[j], a_dt_bias[j], chunk)
            jobs = list(cast_jobs.items()) if i == 0 else []
            h, cast = _matmul(xb, _cast_weight_transposed(a_w_t, j, qkvz_w), BF16,
                              [job for _, job in jobs])
            bf16_weights.update({name: wb for (name, _), wb in zip(jobs, cast)})
            o = _gated_delta_net(h, a_conv_w[j].T, gc, beta, a_norm_w[j], batch, a_heads)
        else:
            tq = att_tq
            h, _ = _matmul(xb, bf16_weights["b_in", j], BF16, slab_major=True)
            table = rel_bias[_t5_bucket_table(tq)] - rel_bias[REL_BUCKETS - 1][None, :]
            brow = jnp.roll(table[::-1], 1, axis=0).T.reshape(b_heads, 1, tq)
            lam_vecs = jnp.stack([b_lam_q1[j], b_lam_k1[j], b_lam_q2[j], b_lam_k2[j]])
            lam_init = 0.8 - 0.6 * math.exp(-0.3 * i)
            o = _diff_attention(h, brow, lam_vecs, b_subln_w[j], batch, b_heads, lam_init)
        r = _matmul_residual(o, bf16_weights["out", i], xf, alpha)
        xf, xb = _layer_norm(r, ln_g[i], ln_b[i], with_bf16=i + 1 < depth)
    return xf.reshape(batch, T, D)
```

```python
import functools
import math

import numpy as np
import jax
import jax.numpy as jnp
from jax import lax
from jax.experimental import pallas as pl
from jax.experimental.pallas import tpu as pltpu

F32 = jnp.float32
BF16 = jnp.bfloat16

LANES = 128
SUBLANES = 8
BF16_SUBLANES = 16
VMEM_LIMIT_BYTES = 56 * 1024 * 1024

CONV_TAPS = 4
HEAD_DIM = 128
LN_EPS = 1e-5
RMS_EPS_A = 1e-6
RMS_EPS_B = 1e-5
L2_EPS = 1e-6
REL_BUCKETS = 32
REL_MAX_DIST = 128
LOG2_E = math.log2(math.e)

MM_BM = 1024
MM_BN = 1024
IN_BM = 2048
IN_BN = 512
GDN_CHUNK = 128
GDN_TB = 2048
GDN_HEADS_PER_STEP = 4
ATT_TQ = 512
ATT_PAIRS_PER_STEP = 4
LN_BM = 512
GATES_BM = 512
GATES_W_ROWS = 64
CAST_T_BN = 1024
CAST_T_BK = 1024


def _cparams(sem):
    return pltpu.CompilerParams(dimension_semantics=sem, vmem_limit_bytes=VMEM_LIMIT_BYTES)


def _silu(x):
    h = 0.5 * x
    return h + h * jnp.tanh(h)


def _cast_t_kernel(w_ref, o_ref):
    o_ref[...] = w_ref[...].T.astype(o_ref.dtype)


def _cast_weight_transposed(wt3, layer, nrows):
    K = wt3.shape[2]
    bn, bk = min(CAST_T_BN, nrows), min(CAST_T_BK, K)
    return pl.pallas_call(
        _cast_t_kernel,
        grid=(nrows // bn, K // bk),
        in_specs=[pl.BlockSpec((None, bn, bk), lambda n, k: (layer, n, k))],
        out_specs=pl.BlockSpec((bk, bn), lambda n, k: (k, n)),
        out_shape=jax.ShapeDtypeStruct((K, nrows), BF16),
        compiler_params=_cparams(("parallel", "parallel")),
        name="cast_weight_t",
    )(wt3)


def _mm_kernel(n_side, x_ref, w_ref, *refs):
    side_in, o_ref, side_out = refs[:2 * n_side], refs[2 * n_side], refs[2 * n_side + 1:]
    res = jnp.dot(x_ref[...], w_ref[...], preferred_element_type=F32).astype(o_ref.dtype)
    if len(o_ref.shape) == 3:
        for c in range(o_ref.shape[0]):
            o_ref[c] = res[:, c * LANES:(c + 1) * LANES]
    else:
        o_ref[...] = res
    for k in range(n_side):
        side_out[k][...] = (side_in[2 * k][...] * side_in[2 * k + 1][...]).astype(side_out[k].dtype)


def _matmul(x, w, out_dtype, side_casts=(), slab_major=False):
    M, K = x.shape
    N = w.shape[1]
    bm, bn = min(IN_BM, M), min(IN_BN, N)
    ni, nj = M // bm, N // bn
    side_args, side_in_specs, side_out_specs, side_out_shapes = [], [], [], []
    for w3, layer, col_scale in side_casts:
        ks, ns = w3.shape[1], w3.shape[2]
        rows = ks // (ni * nj)
        assert rows * ni * nj == ks and rows % BF16_SUBLANES == 0
        side_args += [w3, col_scale.reshape(1, ns).astype(F32)]
        side_in_specs += [pl.BlockSpec((None, rows, ns), lambda i, j, layer=layer: (layer, i * nj + j, 0)),
                          pl.BlockSpec((1, ns), lambda i, j: (0, 0))]
        side_out_specs.append(pl.BlockSpec((rows, ns), lambda i, j: (i * nj + j, 0)))
        side_out_shapes.append(jax.ShapeDtypeStruct((ks, ns), BF16))
    if slab_major:
        main_spec = pl.BlockSpec((bn // LANES, bm, LANES), lambda i, j: (j, i, 0))
        main_shape = jax.ShapeDtypeStruct((N // LANES, M, LANES), out_dtype)
    else:
        main_spec = pl.BlockSpec((bm, bn), lambda i, j: (i, j))
        main_shape = jax.ShapeDtypeStruct((M, N), out_dtype)
    out = pl.pallas_call(
        functools.partial(_mm_kernel, len(side_casts)),
        grid=(ni, nj),
        in_specs=[pl.BlockSpec((bm, K), lambda i, j: (i, 0)),
                  pl.BlockSpec((K, bn), lambda i, j: (0, j))] + side_in_specs,
        out_specs=[main_spec] + side_out_specs,
        out_shape=[main_shape] + side_out_shapes,
        compiler_params=_cparams(("parallel", "parallel")),
        name="in_proj",
    )(x, w, *side_args)
    return out[0], out[1:]


def _mm_res_kernel(alpha, o_ref, w_ref, x_ref, r_ref):
    y = jnp.dot(o_ref[...], w_ref[...], preferred_element_type=F32)
    r_ref[...] = alpha * x_ref[...] + y


def _matmul_residual(o, w, x, alpha):
    M, K = o.shape
    N = w.shape[1]
    bm, bn = min(MM_BM, M), min(MM_BN, N)
    return pl.pallas_call(
        functools.partial(_mm_res_kernel, alpha),
        grid=(M // bm, N // bn),
        in_specs=[pl.BlockSpec((bm, K), lambda i, j: (i, 0)),
                  pl.BlockSpec((K, bn), lambda i, j: (0, j)),
                  pl.BlockSpec((bm, bn), lambda i, j: (i, j))],
        out_specs=pl.BlockSpec((bm, bn), lambda i, j: (i, j)),
        out_shape=jax.ShapeDtypeStruct((M, N), F32),
        compiler_params=_cparams(("parallel", "parallel")),
        name="out_proj_residual",
    )(o, w, x)


def _ln_kernel(r_ref, g_ref, b_ref, o_ref, *ob_ref):
    r = r_ref[...]
    mu = jnp.mean(r, axis=-1, keepdims=True)
    d = r - mu
    var = jnp.mean(d * d, axis=-1, keepdims=True)
    y = d * lax.rsqrt(var + LN_EPS) * g_ref[...] + b_ref[...]
    o_ref[...] = y
    for ref in ob_ref:
        ref[...] = y.astype(ref.dtype)


def _layer_norm(r, g, b, with_bf16):
    M, D = r.shape
    bm = min(LN_BM, M)
    row_spec = pl.BlockSpec((bm, D), lambda i: (i, 0))
    out = pl.pallas_call(
        _ln_kernel,
        grid=(M // bm,),
        in_specs=[row_spec,
                  pl.BlockSpec((1, D), lambda i: (0, 0)),
                  pl.BlockSpec((1, D), lambda i: (0, 0))],
        out_specs=[row_spec] + [row_spec] * with_bf16,
        out_shape=[jax.ShapeDtypeStruct((M, D), F32)] + [jax.ShapeDtypeStruct((M, D), BF16)] * with_bf16,
        compiler_params=_cparams(("parallel",)),
        name="layer_norm",
    )(r, g.reshape(1, D), b.reshape(1, D))
    return out[0], (out[1] if with_bf16 else None)


def _gates_kernel(chunk, x_ref, w_ref, alog_ref, dtb_ref, gc_ref, beta_ref, xb_ref):
    nh = gc_ref.shape[0]
    xb = x_ref[...].astype(BF16)
    xb_ref[...] = xb
    row = lax.broadcasted_iota(jnp.int32, w_ref.shape, 0)
    w = jnp.where(row < 2 * nh, w_ref[...], 0.0).astype(BF16)
    ab = lax.dot_general(w, xb, (((1,), (1,)), ((), ())), preferred_element_type=F32)
    a = ab[:nh] + dtb_ref[...]
    b = ab[nh:2 * nh]
    softplus = jnp.maximum(a, 0.0) + jnp.log1p(jnp.exp(-jnp.abs(a)))
    g = -jnp.exp(alog_ref[...]) * softplus
    pos = lax.broadcasted_iota(jnp.int32, g.shape, 1) % chunk
    s = 1
    while s < chunk:
        g = g + jnp.where(pos >= s, pltpu.roll(g, s, axis=1), 0.0)
        s *= 2
    gc_ref[...] = g
    beta_ref[...] = 1.0 / (1.0 + jnp.exp(-b))


def _gates(x, wt3, layer, row0, a_log, dt_bias, chunk):
    M, K = x.shape
    nh = a_log.shape[0]
    bm = min(GATES_BM, M)
    assert row0 % GATES_W_ROWS == 0 and 2 * nh <= GATES_W_ROWS
    return pl.pallas_call(
        functools.partial(_gates_kernel, chunk),
        grid=(M // bm,),
        in_specs=[pl.BlockSpec((bm, K), lambda i: (i, 0)),
                  pl.BlockSpec((None, GATES_W_ROWS, K), lambda i: (layer, row0 // GATES_W_ROWS, 0)),
                  pl.BlockSpec((nh, 1), lambda i: (0, 0)),
                  pl.BlockSpec((nh, 1), lambda i: (0, 0))],
        out_specs=[pl.BlockSpec((nh, bm), lambda i: (0, i)),
                   pl.BlockSpec((nh, bm), lambda i: (0, i)),
                   pl.BlockSpec((bm, K), lambda i: (i, 0))],
        out_shape=[jax.ShapeDtypeStruct((nh, M), F32), jax.ShapeDtypeStruct((nh, M), F32),
                   jax.ShapeDtypeStruct((M, K), BF16)],
        compiler_params=_cparams(("parallel",)),
        name="gdn_gates",
    )(x, wt3, a_log.reshape(nh, 1), dt_bias.reshape(nh, 1))


def _dot(a, b):
    return jnp.dot(a.astype(BF16), b.astype(BF16), preferred_element_type=F32)


def _dot_nt(a, b):
    return lax.dot_general(a.astype(BF16), b.astype(BF16), (((1,), (1,)), ((), ())),
                           preferred_element_type=F32)


def _gdn_kernel(chunk, q_ref, k_ref, v_ref, z_ref, wq_ref, wk_ref, wv_ref, gc_ref, beta_ref,
                nw_ref, o_ref, s_ref, hq_ref, hk_ref, hv_ref):
    tb = q_ref.shape[0]
    dk = HEAD_DIM
    nheads = q_ref.shape[1] // dk
    halo = hq_ref.shape[0]
    nchunks = tb // chunk

    @pl.when(pl.program_id(2) == 0)
    def _():
        s_ref[...] = jnp.zeros_like(s_ref)
        hq_ref[...] = jnp.zeros_like(hq_ref)
        hk_ref[...] = jnp.zeros_like(hk_ref)
        hv_ref[...] = jnp.zeros_like(hv_ref)

    def conv_silu(x_ref, w_ref, h_ref):
        x = x_ref[...].astype(F32)
        xs = jnp.concatenate([h_ref[...], x], axis=0)
        w = w_ref[...]
        y = x * w[CONV_TAPS - 1:CONV_TAPS]
        for j in range(CONV_TAPS - 1):
            off = halo - (CONV_TAPS - 1) + j
            y = y + xs[off:off + tb] * w[j:j + 1]
        h_ref[...] = x[tb - halo:]
        return _silu(y)

    q_all = conv_silu(q_ref, wq_ref, hq_ref)
    k_all = conv_silu(k_ref, wk_ref, hk_ref)
    v_all = conv_silu(v_ref, wv_ref, hv_ref)

    ri = lax.broadcasted_iota(jnp.int32, (chunk, chunk), 0)
    ci = lax.broadcasted_iota(jnp.int32, (chunk, chunk), 1)
    incl = ri >= ci
    strict = ri > ci
    eye = (ri == ci).astype(F32)
    off_masks = []
    b = 1
    while b < chunk:
        off_masks.append((ri // (2 * b) == ci // (2 * b)) & (ri // b != ci // b))
        b *= 2

    chains = [(g, c) for g in range(nheads) for c in range(nchunks)]
    a_mat, attn, rhs, lhs_top, qd, cdec = {}, {}, {}, {}, {}, {}
    for g in range(nheads):
        hs = slice(g * dk, (g + 1) * dk)
        q = q_all[:, hs]
        k = k_all[:, hs]
        v = v_all[:, hs]
        q = q * (lax.rsqrt(jnp.sum(q * q, axis=-1, keepdims=True) + L2_EPS) * (dk ** -0.5))
        k = k * lax.rsqrt(jnp.sum(k * k, axis=-1, keepdims=True) + L2_EPS)
        gc_row = jnp.broadcast_to(gc_ref[g], (LANES, tb))
        gc_col = gc_row.T
        beta_col = jnp.broadcast_to(beta_ref[g], (LANES, tb)).T
        eg_col = jnp.exp(gc_col)
        for c in range(nchunks):
            sl = slice(c * chunk, (c + 1) * chunk)
            qc, kc, vc = q[sl], k[sl], v[sl]
            bc = beta_col[sl]
            gcc = gc_col[sl]
            kb = kc * bc
            diff = gcc[:, :chunk] - gc_row[:chunk, sl]
            decay = jnp.where(incl, jnp.exp(jnp.where(incl, diff, 0.0)), 0.0)
            qk = _dot_nt(jnp.concatenate([qc, kb], axis=0), kc)
            attn[g, c] = qk[:chunk] * decay
            a_mat[g, c] = jnp.where(strict, qk[chunk:] * decay, 0.0)
            rhs[g, c] = jnp.concatenate([kb * eg_col[sl], vc * bc], axis=1)
            g_last = gcc[chunk - 1:chunk, :]
            lhs_top[g, c] = (kc * jnp.exp(g_last - gcc)).T
            qd[g, c] = qc * eg_col[sl]
            cdec[g, c] = jnp.exp(g_last)

    x = {ch: eye - jnp.where(off_masks[0], a_mat[ch], 0.0) for ch in chains}
    for off_mask in off_masks[1:]:
        y = {ch: _dot(jnp.where(off_mask, a_mat[ch], 0.0), x[ch]) for ch in chains}
        x = {ch: x[ch] - _dot(x[ch], y[ch]) for ch in chains}
    wu = {ch: _dot(x[ch], rhs[ch]) for ch in chains}
    st = {ch: _dot(jnp.concatenate([lhs_top[ch], attn[ch]], axis=0), wu[ch]) for ch in chains}

    s = [s_ref[g] for g in range(nheads)]
    outs = {}
    for c in range(nchunks):
        for g in range(nheads):
            t = st[g, c]
            lhs = jnp.concatenate([-t[:dk, :dk], qd[g, c] - t[dk:, :dk]], axis=0)
            r = _dot(lhs, s[g])
            outs[g, c] = r[dk:] + t[dk:, dk:]
            s[g] = s[g] * cdec[g, c] + r[:dk] + t[:dk, dk:]
    for g in range(nheads):
        s_ref[g] = s[g]

    nw = nw_ref[...]
    cols = []
    for g in range(nheads):
        og = jnp.concatenate([outs[g, c] for c in range(nchunks)], axis=0)
        cols.append(og * lax.rsqrt(jnp.mean(og * og, axis=-1, keepdims=True) + RMS_EPS_A) * nw)
    o = jnp.concatenate(cols, axis=1)
    o_ref[...] = (o * _silu(z_ref[...].astype(F32))).astype(o_ref.dtype)


def _gated_delta_net(h, conv_w_t, gc, beta, norm_w, batch, heads):
    M = h.shape[0]
    T = M // batch
    tb = min(GDN_TB, T)
    nt = T // tb
    dk = HEAD_DIM
    chunk = min(GDN_CHUNK, tb)
    hps = min(GDN_HEADS_PER_STEP, heads)
    ng = heads // hps
    width = hps * dk

    def hspec(off):
        return pl.BlockSpec((tb, width), lambda b, hh, t: (b * nt + t, hh + off))

    def wspec(off):
        return pl.BlockSpec((CONV_TAPS, width), lambda b, hh, t: (0, hh + off))

    gspec = pl.BlockSpec((hps, 1, tb), lambda b, hh, t: (hh, 0, b * nt + t))
    return pl.pallas_call(
        functools.partial(_gdn_kernel, chunk),
        grid=(batch, ng, nt),
        in_specs=[hspec(0), hspec(ng), hspec(2 * ng), hspec(3 * ng),
                  wspec(0), wspec(ng), wspec(2 * ng), gspec, gspec,
                  pl.BlockSpec((1, dk), lambda b, hh, t: (0, 0))],
        out_specs=pl.BlockSpec((tb, width), lambda b, hh, t: (b * nt + t, hh)),
        out_shape=jax.ShapeDtypeStruct((M, heads * dk), BF16),
        scratch_shapes=[pltpu.VMEM((hps, dk, dk), F32),
                        pltpu.VMEM((SUBLANES, width), F32),
                        pltpu.VMEM((SUBLANES, width), F32),
                        pltpu.VMEM((SUBLANES, width), F32)],
        compiler_params=_cparams(("parallel", "parallel", "arbitrary")),
        name="gated_delta_rule",
    )(h, h, h, h, conv_w_t, conv_w_t, conv_w_t,
      gc.reshape(heads, 1, M), beta.reshape(heads, 1, M), norm_w.reshape(1, dk))


def _t5_bucket_table(n):
    rel = np.arange(n)
    max_exact = REL_BUCKETS // 2
    nf = np.maximum(rel, 1).astype(np.float32)
    large = max_exact + (np.log(nf / np.float32(max_exact)) / np.float32(math.log(REL_MAX_DIST / max_exact))
                         * np.float32(REL_BUCKETS - max_exact)).astype(np.int32)
    large = np.minimum(large, REL_BUCKETS - 1)
    return np.where(rel < max_exact, rel, large)


def _attn_kernel(lam_init, q1_ref, q2_ref, k1_ref, k2_ref, v_ref, z_ref, brow_ref, lam_ref, sw_ref,
                 o_ref, m1_ref, l1_ref, a1_ref, m2_ref, l2_ref, a2_ref, bd_ref, bs_ref, s_ref):
    tq = bd_ref.shape[0]
    dv = v_ref.shape[0] * LANES
    step = pl.program_id(2)

    def slabs(ref, rows):
        return jnp.concatenate([ref[c, rows, :] for c in range(ref.shape[0])], axis=1)

    @pl.when(step == 0)
    def _():
        r = pltpu.roll(jnp.broadcast_to(brow_ref[...], (tq, tq)), 0, axis=1, stride=1, stride_axis=0)
        r = r * LOG2_E
        ri = lax.broadcasted_iota(jnp.int32, (tq, tq), 0)
        ci = lax.broadcasted_iota(jnp.int32, (tq, tq), 1)
        bd_ref[...] = jnp.where(ri >= ci, r, -jnp.inf)
        bs_ref[...] = jnp.where(ri < ci, r, 0.0)[:LANES, tq - LANES:]

    def init_stats():
        m1_ref[...] = jnp.full_like(m1_ref, -jnp.inf)
        m2_ref[...] = jnp.full_like(m2_ref, -jnp.inf)
        l1_ref[...] = jnp.zeros_like(l1_ref)
        l2_ref[...] = jnp.zeros_like(l2_ref)
        a1_ref[...] = jnp.zeros_like(a1_ref)
        a2_ref[...] = jnp.zeros_like(a2_ref)

    def lanes(x, width):
        return jnp.concatenate([x] * (width // LANES), axis=1)

    def softmax_update(s, m_ref, l_ref):
        m_old = m_ref[...]
        m_new = jnp.maximum(m_old, jnp.max(s, axis=-1, keepdims=True))
        p = jnp.exp2(s - lanes(m_new, tq))
        alpha = jnp.exp2(m_old - m_new)
        l_ref[...] = alpha * l_ref[...] + jnp.sum(p, axis=-1, keepdims=True)
        m_ref[...] = m_new
        return p.astype(BF16), alpha

    def scores(q, kj, slot):
        start = pl.multiple_of(kj * tq, tq)
        s_ref[slot, 0] = lax.dot_general(q[0], k1_ref[pl.ds(start, tq), :], (((1,), (1,)), ((), ())),
                                         preferred_element_type=F32)
        s_ref[slot, 1] = lax.dot_general(q[1], k2_ref[pl.ds(start, tq), :], (((1,), (1,)), ((), ())),
                                         preferred_element_type=F32)

    def consume(kj, slot, bias_ref):
        start = pl.multiple_of(kj * tq, tq)
        vblk = slabs(v_ref, pl.ds(start, tq))
        for stream, (m_ref, l_ref, a_ref) in enumerate(((m1_ref, l1_ref, a1_ref), (m2_ref, l2_ref, a2_ref))):
            s = s_ref[slot, stream]
            if bias_ref is not None:
                s = s + bias_ref[...]
            p, alpha = softmax_update(s, m_ref, l_ref)
            a_ref[...] = lanes(alpha, dv) * a_ref[...] + jnp.dot(p, vblk, preferred_element_type=F32)

    def add_previous_block_bias(kj, slot, qi):
        flag = jnp.where(kj == qi - 1, 1.0, 0.0)
        corner = flag * bs_ref[...]
        for stream in range(2):
            s_ref[slot, stream, :LANES, tq - LANES:] = s_ref[slot, stream, :LANES, tq - LANES:] + corner

    def pair_loop(q, qi, first):
        def pair_body(t, carry):
            kj = first + 2 * t
            scores(q, kj + 1, 1)
            consume(kj, 0, None)
            add_previous_block_bias(kj + 1, 1, qi)
            scores(q, kj + 2, 0)
            consume(kj + 1, 1, None)
            return carry

        lax.fori_loop(0, qi // 2, pair_body, 0)

    lv = lam_ref[...]
    lam = (jnp.exp(jnp.sum(lv[0:1] * lv[1:2], axis=-1, keepdims=True))
           - jnp.exp(jnp.sum(lv[2:3] * lv[3:4], axis=-1, keepdims=True)) + lam_init)

    def finalize(rows):
        o = a1_ref[...] / lanes(l1_ref[...], dv) - lam * (a2_ref[...] / lanes(l2_ref[...], dv))
        o = o * lax.rsqrt(jnp.mean(o * o, axis=-1, keepdims=True) + RMS_EPS_B) * sw_ref[...]
        o = o * (1.0 - lam_init) * _silu(slabs(z_ref, rows).astype(F32))
        o_ref[rows, :] = o.astype(o_ref.dtype)

    for pair in range(q1_ref.shape[0] // (2 * tq)):
        base = 2 * pair * tq
        rows_even, rows_odd = slice(base, base + tq), slice(base + tq, base + 2 * tq)
        q_even = (q1_ref[rows_even, :], q2_ref[rows_even, :])
        q_odd = (q1_ref[rows_odd, :], q2_ref[rows_odd, :])
        qi_even = 2 * (step * (q1_ref.shape[0] // (2 * tq)) + pair)
        qi_odd = qi_even + 1

        init_stats()
        scores(q_even, 0, 0)
        pair_loop(q_even, qi_even, 0)
        scores(q_odd, 0, 1)
        consume(qi_even, 0, bd_ref)
        finalize(rows_even)

        init_stats()
        add_previous_block_bias(0, 1, qi_odd)
        scores(q_odd, 1, 0)
        consume(0, 1, None)
        pair_loop(q_odd, qi_odd, 1)
        consume(qi_odd, 0, bd_ref)
        finalize(rows_odd)


def _diff_attention(h, brow, lam_vecs, subln_w, batch, heads, lam_init):
    M = h.shape[1]
    T = M // batch
    tq = min(ATT_TQ, T // 2)
    rows = 2 * tq * min(ATT_PAIRS_PER_STEP, T // (2 * tq))
    ns = T // rows
    assert ns * rows == T
    dh = HEAD_DIM
    dv = 2 * dh
    return pl.pallas_call(
        functools.partial(_attn_kernel, lam_init),
        grid=(batch, heads, ns),
        in_specs=[pl.BlockSpec((None, rows, dh), lambda b, hh, i: (2 * hh, b * ns + i, 0)),
                  pl.BlockSpec((None, rows, dh), lambda b, hh, i: (2 * hh + 1, b * ns + i, 0)),
                  pl.BlockSpec((None, T, dh), lambda b, hh, i: (2 * heads + 2 * hh, b, 0)),
                  pl.BlockSpec((None, T, dh), lambda b, hh, i: (2 * heads + 2 * hh + 1, b, 0)),
                  pl.BlockSpec((2, T, dh), lambda b, hh, i: (2 * heads + hh, b, 0)),
                  pl.BlockSpec((2, rows, dh), lambda b, hh, i: (3 * heads + hh, b * ns + i, 0)),
                  pl.BlockSpec((None, 1, tq), lambda b, hh, i: (hh, 0, 0)),
                  pl.BlockSpec((4, dh), lambda b, hh, i: (0, 0)),
                  pl.BlockSpec((1, dv), lambda b, hh, i: (0, 0))],
        out_specs=pl.BlockSpec((rows, dv), lambda b, hh, i: (b * ns + i, hh)),
        out_shape=jax.ShapeDtypeStruct((M, heads * dv), BF16),
        scratch_shapes=[pltpu.VMEM((tq, LANES), F32), pltpu.VMEM((tq, LANES), F32), pltpu.VMEM((tq, dv), F32),
                        pltpu.VMEM((tq, LANES), F32), pltpu.VMEM((tq, LANES), F32), pltpu.VMEM((tq, dv), F32),
                        pltpu.VMEM((tq, tq), F32), pltpu.VMEM((LANES, LANES), F32),
                        pltpu.VMEM((2, 2, tq, tq), F32)],
        compiler_params=_cparams(("parallel", "parallel", "arbitrary")),
        name="diff_attention",
    )(h, h, h, h, h, h, brow, lam_vecs, subln_w.reshape(1, dv))


def kernel(x, ln_g, ln_b, rel_bias, a_w_in, a_conv_w, a_a_log, a_dt_bias, a_norm_w, a_w_out,
           b_w_in, b_lam_q1, b_lam_k1, b_lam_q2, b_lam_k2, b_subln_w, b_w_out):
    batch, T, D = x.shape
    M = batch * T
    depth = ln_g.shape[0]
    a_heads = a_a_log.shape[1]
    b_heads = rel_bias.shape[1]
    alpha = (2.0 * depth) ** 0.25
    att_tq = min(ATT_TQ, T // 2)
    assert T % min(GDN_TB, T) == 0 and T % (2 * att_tq) == 0 and att_tq >= LANES

    q_w = 2 * b_heads * HEAD_DIM
    n_in_b = b_w_in.shape[2]
    b_col_scale = jnp.where(jnp.arange(n_in_b) < q_w, HEAD_DIM ** -0.5 * LOG2_E, 1.0)
    ones_d = jnp.ones((D,), F32)
    cast_jobs = {}
    for i in range(depth):
        if i % 2 == 1:
            cast_jobs["b_in", i // 2] = (b_w_in, i // 2, b_col_scale)
        cast_jobs["out", i] = (a_w_out if i % 2 == 0 else b_w_out, i // 2, ones_d)
    bf16_weights = {}

    xf = x.reshape(M, D)
    xb = None
    for i in range(depth):
        j = i // 2
        if i % 2 == 0:
            qkvz_w = 4 * a_heads * HEAD_DIM
            chunk = min(GDN_CHUNK, GDN_TB, T)
            a_w_t = jnp.swapaxes(a_w_in, 1, 2)
            gc, beta, xb = _gates(xf, a_w_t, j, qkvz_w, a_a_log[j], a_dt_bias[j], chunk)
            jobs = list(cast_jobs.items()) if i == 0 else []
            h, cast = _matmul(xb, _cast_weight_transposed(a_w_t, j, qkvz_w), BF16,
                              [job for _, job in jobs])
            bf16_weights.update({name: wb for (name, _), wb in zip(jobs, cast)})
            o = _gated_delta_net(h, a_conv_w[j].T, gc, beta, a_norm_w[j], batch, a_heads)
        else:
            tq = att_tq
            h, _ = _matmul(xb, bf16_weights["b_in", j], BF16, slab_major=True)
            table = rel_bias[_t5_bucket_table(tq)] - rel_bias[REL_BUCKETS - 1][None, :]
            brow = jnp.roll(table[::-1], 1, axis=0).T.reshape(b_heads, 1, tq)
            lam_vecs = jnp.stack([b_lam_q1[j], b_lam_k1[j], b_lam_q2[j], b_lam_k2[j]])
            lam_init = 0.8 - 0.6 * math.exp(-0.3 * i)
            o = _diff_attention(h, brow, lam_vecs, b_subln_w[j], batch, b_heads, lam_init)
        r = _matmul_residual(o, bf16_weights["out", i], xf, alpha)
        xf, xb = _layer_norm(r, ln_g[i], ln_b[i], with_bf16=i + 1 < depth)
    return xf.reshape(batch, T, D)
```

```python
import functools
import math

import numpy as np
import jax
import jax.numpy as jnp
from jax import lax
from jax.experimental import pallas as pl
from jax.experimental.pallas import tpu as pltpu

F32 = jnp.float32
BF16 = jnp.bfloat16

LANES = 128
SUBLANES = 8
BF16_SUBLANES = 16
VMEM_LIMIT_BYTES = 56 * 1024 * 1024

CONV_TAPS = 4
HEAD_DIM = 128
LN_EPS = 1e-5
RMS_EPS_A = 1e-6
RMS_EPS_B = 1e-5
L2_EPS = 1e-6
REL_BUCKETS = 32
REL_MAX_DIST = 128
LOG2_E = math.log2(math.e)

MM_BM = 1024
MM_BN = 1024
GDN_CHUNK = 128
GDN_TB = 2048
GDN_HEADS_PER_STEP = 4
ATT_TQ = 512
ATT_PAIRS_PER_STEP = 4
LN_BM = 256
LN_BUFFERS = 3
GATES_BM = 512
GATES_W_ROWS = 64
CAST_T_BN = 1024
CAST_T_BK = 1024


def _cparams(sem):
    return pltpu.CompilerParams(dimension_semantics=sem, vmem_limit_bytes=VMEM_LIMIT_BYTES)


def _silu(x):
    h = 0.5 * x
    return h + h * jnp.tanh(h)


def _cast_t_kernel(w_ref, o_ref):
    o_ref[...] = w_ref[...].T.astype(o_ref.dtype)


def _cast_weight_transposed(wt3, layer, nrows):
    K = wt3.shape[2]
    bn, bk = min(CAST_T_BN, nrows), min(CAST_T_BK, K)
    return pl.pallas_call(
        _cast_t_kernel,
        grid=(nrows // bn, K // bk),
        in_specs=[pl.BlockSpec((None, bn, bk), lambda n, k: (layer, n, k))],
        out_specs=pl.BlockSpec((bk, bn), lambda n, k: (k, n)),
        out_shape=jax.ShapeDtypeStruct((K, nrows), BF16),
        compiler_params=_cparams(("parallel", "parallel")),
        name="cast_weight_t",
    )(wt3)


def _mm_kernel(n_side, x_ref, w_ref, *refs):
    side_in, o_ref, side_out = refs[:2 * n_side], refs[2 * n_side], refs[2 * n_side + 1:]
    res = jnp.dot(x_ref[...], w_ref[...], preferred_element_type=F32).astype(o_ref.dtype)
    if len(o_ref.shape) == 3:
        for c in range(o_ref.shape[0]):
            o_ref[c] = res[:, c * LANES:(c + 1) * LANES]
    else:
        o_ref[...] = res
    for k in range(n_side):
        side_out[k][...] = (side_in[2 * k][...] * side_in[2 * k + 1][...]).astype(side_out[k].dtype)


def _matmul(x, w, out_dtype, side_casts=(), slab_major=False):
    M, K = x.shape
    N = w.shape[1]
    bm, bn = min(MM_BM, M), min(MM_BN, N)
    ni, nj = M // bm, N // bn
    side_args, side_in_specs, side_out_specs, side_out_shapes = [], [], [], []
    for w3, layer, col_scale in side_casts:
        ks, ns = w3.shape[1], w3.shape[2]
        rows = ks // (ni * nj)
        assert rows * ni * nj == ks and rows % BF16_SUBLANES == 0
        side_args += [w3, col_scale.reshape(1, ns).astype(F32)]
        side_in_specs += [pl.BlockSpec((None, rows, ns), lambda i, j, layer=layer: (layer, i * nj + j, 0)),
                          pl.BlockSpec((1, ns), lambda i, j: (0, 0))]
        side_out_specs.append(pl.BlockSpec((rows, ns), lambda i, j: (i * nj + j, 0)))
        side_out_shapes.append(jax.ShapeDtypeStruct((ks, ns), BF16))
    if slab_major:
        main_spec = pl.BlockSpec((bn // LANES, bm, LANES), lambda i, j: (j, i, 0))
        main_shape = jax.ShapeDtypeStruct((N // LANES, M, LANES), out_dtype)
    else:
        main_spec = pl.BlockSpec((bm, bn), lambda i, j: (i, j))
        main_shape = jax.ShapeDtypeStruct((M, N), out_dtype)
    out = pl.pallas_call(
        functools.partial(_mm_kernel, len(side_casts)),
        grid=(ni, nj),
        in_specs=[pl.BlockSpec((bm, K), lambda i, j: (i, 0)),
                  pl.BlockSpec((K, bn), lambda i, j: (0, j))] + side_in_specs,
        out_specs=[main_spec] + side_out_specs,
        out_shape=[main_shape] + side_out_shapes,
        compiler_params=_cparams(("parallel", "parallel")),
        name="in_proj",
    )(x, w, *side_args)
    return out[0], out[1:]


def _mm_res_kernel(alpha, o_ref, w_ref, x_ref, r_ref):
    y = jnp.dot(o_ref[...], w_ref[...], preferred_element_type=F32)
    r_ref[...] = alpha * x_ref[...] + y


def _matmul_residual(o, w, x, alpha):
    M, K = o.shape
    N = w.shape[1]
    bm, bn = min(MM_BM, M), min(MM_BN, N)
    return pl.pallas_call(
        functools.partial(_mm_res_kernel, alpha),
        grid=(M // bm, N // bn),
        in_specs=[pl.BlockSpec((bm, K), lambda i, j: (i, 0)),
                  pl.BlockSpec((K, bn), lambda i, j: (0, j)),
                  pl.BlockSpec((bm, bn), lambda i, j: (i, j))],
        out_specs=pl.BlockSpec((bm, bn), lambda i, j: (i, j)),
        out_shape=jax.ShapeDtypeStruct((M, N), F32),
        compiler_params=_cparams(("parallel", "parallel")),
        name="out_proj_residual",
    )(o, w, x)


def _ln_kernel(r_ref, g_ref, b_ref, o_ref, *ob_ref):
    r = r_ref[...]
    mu = jnp.mean(r, axis=-1, keepdims=True)
    d = r - mu
    var = jnp.mean(d * d, axis=-1, keepdims=True)
    y = d * lax.rsqrt(var + LN_EPS) * g_ref[...] + b_ref[...]
    o_ref[...] = y
    for ref in ob_ref:
        ref[...] = y.astype(ref.dtype)


def _layer_norm(r, g, b, with_bf16):
    M, D = r.shape
    bm = min(LN_BM, M)
    row_spec = pl.BlockSpec((bm, D), lambda i: (i, 0))
    n_out = 1 + with_bf16

    def streamed(r_hbm, g_ref, b_ref, *out_hbm):
        def body(r_ref, *o_refs):
            _ln_kernel(r_ref, g_ref, b_ref, *o_refs)

        pltpu.emit_pipeline(
            body, grid=(M // bm,),
            in_specs=[pl.BlockSpec((bm, D), lambda i: (i, 0), pipeline_mode=pl.Buffered(LN_BUFFERS))],
            out_specs=[row_spec] * n_out,
        )(r_hbm, *out_hbm)

    out = pl.pallas_call(
        streamed,
        in_specs=[pl.BlockSpec(memory_space=pl.ANY),
                  pl.BlockSpec(memory_space=pltpu.VMEM),
                  pl.BlockSpec(memory_space=pltpu.VMEM)],
        out_specs=[pl.BlockSpec(memory_space=pl.ANY)] * n_out,
        out_shape=[jax.ShapeDtypeStruct((M, D), F32)] + [jax.ShapeDtypeStruct((M, D), BF16)] * with_bf16,
        compiler_params=pltpu.CompilerParams(vmem_limit_bytes=VMEM_LIMIT_BYTES),
        name="layer_norm",
    )(r, g.reshape(1, D), b.reshape(1, D))
    return out[0], (out[1] if with_bf16 else None)


def _gates_kernel(chunk, x_ref, w_ref, alog_ref, dtb_ref, gc_ref, beta_ref, xb_ref):
    nh = gc_ref.shape[0]
    xb = x_ref[...].astype(BF16)
    xb_ref[...] = xb
    row = lax.broadcasted_iota(jnp.int32, w_ref.shape, 0)
    w = jnp.where(row < 2 * nh, w_ref[...], 0.0).astype(BF16)
    ab = lax.dot_general(w, xb, (((1,), (1,)), ((), ())), preferred_element_type=F32)
    a = ab[:nh] + dtb_ref[...]
    b = ab[nh:2 * nh]
    softplus = jnp.maximum(a, 0.0) + jnp.log1p(jnp.exp(-jnp.abs(a)))
    g = -jnp.exp(alog_ref[...]) * softplus
    pos = lax.broadcasted_iota(jnp.int32, g.shape, 1) % chunk
    s = 1
    while s < chunk:
        g = g + jnp.where(pos >= s, pltpu.roll(g, s, axis=1), 0.0)
        s *= 2
    gc_ref[...] = g
    beta_ref[...] = 1.0 / (1.0 + jnp.exp(-b))


def _gates(x, wt3, layer, row0, a_log, dt_bias, chunk):
    M, K = x.shape
    nh = a_log.shape[0]
    bm = min(GATES_BM, M)
    assert row0 % GATES_W_ROWS == 0 and 2 * nh <= GATES_W_ROWS
    return pl.pallas_call(
        functools.partial(_gates_kernel, chunk),
        grid=(M // bm,),
        in_specs=[pl.BlockSpec((bm, K), lambda i: (i, 0)),
                  pl.BlockSpec((None, GATES_W_ROWS, K), lambda i: (layer, row0 // GATES_W_ROWS, 0)),
                  pl.BlockSpec((nh, 1), lambda i: (0, 0)),
                  pl.BlockSpec((nh, 1), lambda i: (0, 0))],
        out_specs=[pl.BlockSpec((nh, bm), lambda i: (0, i)),
                   pl.BlockSpec((nh, bm), lambda i: (0, i)),
                   pl.BlockSpec((bm, K), lambda i: (i, 0))],
        out_shape=[jax.ShapeDtypeStruct((nh, M), F32), jax.ShapeDtypeStruct((nh, M), F32),
                   jax.ShapeDtypeStruct((M, K), BF16)],
        compiler_params=_cparams(("parallel",)),
        name="gdn_gates",
    )(x, wt3, a_log.reshape(nh, 1), dt_bias.reshape(nh, 1))


def _dot(a, b):
    return jnp.dot(a.astype(BF16), b.astype(BF16), preferred_element_type=F32)


def _dot_nt(a, b):
    return lax.dot_general(a.astype(BF16), b.astype(BF16), (((1,), (1,)), ((), ())),
                           preferred_element_type=F32)


def _gdn_kernel(chunk, q_ref, k_ref, v_ref, z_ref, wq_ref, wk_ref, wv_ref, gc_ref, beta_ref,
                nw_ref, o_ref, s_ref, hq_ref, hk_ref, hv_ref):
    tb = q_ref.shape[0]
    dk = HEAD_DIM
    nheads = q_ref.shape[1] // dk
    halo = hq_ref.shape[0]
    nchunks = tb // chunk

    @pl.when(pl.program_id(2) == 0)
    def _():
        s_ref[...] = jnp.zeros_like(s_ref)
        hq_ref[...] = jnp.zeros_like(hq_ref)
        hk_ref[...] = jnp.zeros_like(hk_ref)
        hv_ref[...] = jnp.zeros_like(hv_ref)

    def conv_silu(x_ref, w_ref, h_ref):
        x = x_ref[...].astype(F32)
        xs = jnp.concatenate([h_ref[...], x], axis=0)
        w = w_ref[...]
        y = x * w[CONV_TAPS - 1:CONV_TAPS]
        for j in range(CONV_TAPS - 1):
            off = halo - (CONV_TAPS - 1) + j
            y = y + xs[off:off + tb] * w[j:j + 1]
        h_ref[...] = x[tb - halo:]
        return _silu(y)

    q_all = conv_silu(q_ref, wq_ref, hq_ref)
    k_all = conv_silu(k_ref, wk_ref, hk_ref)
    v_all = conv_silu(v_ref, wv_ref, hv_ref)

    ri = lax.broadcasted_iota(jnp.int32, (chunk, chunk), 0)
    ci = lax.broadcasted_iota(jnp.int32, (chunk, chunk), 1)
    incl = ri >= ci
    strict = ri > ci
    eye = (ri == ci).astype(F32)
    off_masks = []
    b = 1
    while b < chunk:
        off_masks.append((ri // (2 * b) == ci // (2 * b)) & (ri // b != ci // b))
        b *= 2

    chains = [(g, c) for g in range(nheads) for c in range(nchunks)]
    a_mat, attn, rhs, lhs_top, qd, cdec = {}, {}, {}, {}, {}, {}
    for g in range(nheads):
        hs = slice(g * dk, (g + 1) * dk)
        q = q_all[:, hs]
        k = k_all[:, hs]
        v = v_all[:, hs]
        q = q * (lax.rsqrt(jnp.sum(q * q, axis=-1, keepdims=True) + L2_EPS) * (dk ** -0.5))
        k = k * lax.rsqrt(jnp.sum(k * k, axis=-1, keepdims=True) + L2_EPS)
        gc_row = jnp.broadcast_to(gc_ref[g], (LANES, tb))
        gc_col = gc_row.T
        beta_col = jnp.broadcast_to(beta_ref[g], (LANES, tb)).T
        eg_col = jnp.exp(gc_col)
        for c in range(nchunks):
            sl = slice(c * chunk, (c + 1) * chunk)
            qc, kc, vc = q[sl], k[sl], v[sl]
            bc = beta_col[sl]
            gcc = gc_col[sl]
            kb = kc * bc
            diff = gcc[:, :chunk] - gc_row[:chunk, sl]
            decay = jnp.where(incl, jnp.exp(jnp.where(incl, diff, 0.0)), 0.0)
            qk = _dot_nt(jnp.concatenate([qc, kb], axis=0), kc)
            attn[g, c] = qk[:chunk] * decay
            a_mat[g, c] = jnp.where(strict, qk[chunk:] * decay, 0.0)
            rhs[g, c] = jnp.concatenate([kb * eg_col[sl], vc * bc], axis=1)
            g_last = gcc[chunk - 1:chunk, :]
            lhs_top[g, c] = (kc * jnp.exp(g_last - gcc)).T
            qd[g, c] = qc * eg_col[sl]
            cdec[g, c] = jnp.exp(g_last)

    x = {ch: eye - jnp.where(off_masks[0], a_mat[ch], 0.0) for ch in chains}
    for off_mask in off_masks[1:]:
        y = {ch: _dot(jnp.where(off_mask, a_mat[ch], 0.0), x[ch]) for ch in chains}
        x = {ch: x[ch] - _dot(x[ch], y[ch]) for ch in chains}
    wu = {ch: _dot(x[ch], rhs[ch]) for ch in chains}
    st = {ch: _dot(jnp.concatenate([lhs_top[ch], attn[ch]], axis=0), wu[ch]) for ch in chains}

    s = [s_ref[g] for g in range(nheads)]
    outs = {}
    for c in range(nchunks):
        for g in range(nheads):
            t = st[g, c]
            lhs = jnp.concatenate([-t[:dk, :dk], qd[g, c] - t[dk:, :dk]], axis=0)
            r = _dot(lhs, s[g])
            outs[g, c] = r[dk:] + t[dk:, dk:]
            s[g] = s[g] * cdec[g, c] + r[:dk] + t[:dk, dk:]
    for g in range(nheads):
        s_ref[g] = s[g]

    nw = nw_ref[...]
    cols = []
    for g in range(nheads):
        og = jnp.concatenate([outs[g, c] for c in range(nchunks)], axis=0)
        cols.append(og * lax.rsqrt(jnp.mean(og * og, axis=-1, keepdims=True) + RMS_EPS_A) * nw)
    o = jnp.concatenate(cols, axis=1)
    o_ref[...] = (o * _silu(z_ref[...].astype(F32))).astype(o_ref.dtype)


def _gated_delta_net(h, conv_w_t, gc, beta, norm_w, batch, heads):
    M = h.shape[0]
    T = M // batch
    tb = min(GDN_TB, T)
    nt = T // tb
    dk = HEAD_DIM
    chunk = min(GDN_CHUNK, tb)
    hps = min(GDN_HEADS_PER_STEP, heads)
    ng = heads // hps
    width = hps * dk

    def hspec(off):
        return pl.BlockSpec((tb, width), lambda b, hh, t: (b * nt + t, hh + off))

    def wspec(off):
        return pl.BlockSpec((CONV_TAPS, width), lambda b, hh, t: (0, hh + off))

    gspec = pl.BlockSpec((hps, 1, tb), lambda b, hh, t: (hh, 0, b * nt + t))
    return pl.pallas_call(
        functools.partial(_gdn_kernel, chunk),
        grid=(batch, ng, nt),
        in_specs=[hspec(0), hspec(ng), hspec(2 * ng), hspec(3 * ng),
                  wspec(0), wspec(ng), wspec(2 * ng), gspec, gspec,
                  pl.BlockSpec((1, dk), lambda b, hh, t: (0, 0))],
        out_specs=pl.BlockSpec((tb, width), lambda b, hh, t: (b * nt + t, hh)),
        out_shape=jax.ShapeDtypeStruct((M, heads * dk), BF16),
        scratch_shapes=[pltpu.VMEM((hps, dk, dk), F32),
                        pltpu.VMEM((SUBLANES, width), F32),
                        pltpu.VMEM((SUBLANES, width), F32),
                        pltpu.VMEM((SUBLANES, width), F32)],
        compiler_params=_cparams(("parallel", "parallel", "arbitrary")),
        name="gated_delta_rule",
    )(h, h, h, h, conv_w_t, conv_w_t, conv_w_t,
      gc.reshape(heads, 1, M), beta.reshape(heads, 1, M), norm_w.reshape(1, dk))


def _t5_bucket_table(n):
    rel = np.arange(n)
    max_exact = REL_BUCKETS // 2
    nf = np.maximum(rel, 1).astype(np.float32)
    large = max_exact + (np.log(nf / np.float32(max_exact)) / np.float32(math.log(REL_MAX_DIST / max_exact))
                         * np.float32(REL_BUCKETS - max_exact)).astype(np.int32)
    large = np.minimum(large, REL_BUCKETS - 1)
    return np.where(rel < max_exact, rel, large)


def _attn_kernel(lam_init, q1_ref, q2_ref, k1_ref, k2_ref, v_ref, z_ref, brow_ref, lam_ref, sw_ref,
                 o_ref, m1_ref, l1_ref, a1_ref, m2_ref, l2_ref, a2_ref, bd_ref, bs_ref, s_ref):
    tq = bd_ref.shape[0]
    dv = v_ref.shape[0] * LANES
    step = pl.program_id(2)

    def slabs(ref, rows):
        return jnp.concatenate([ref[c, rows, :] for c in range(ref.shape[0])], axis=1)

    @pl.when(step == 0)
    def _():
        r = pltpu.roll(jnp.broadcast_to(brow_ref[...], (tq, tq)), 0, axis=1, stride=1, stride_axis=0)
        r = r * LOG2_E
        ri = lax.broadcasted_iota(jnp.int32, (tq, tq), 0)
        ci = lax.broadcasted_iota(jnp.int32, (tq, tq), 1)
        bd_ref[...] = jnp.where(ri >= ci, r, -jnp.inf)
        bs_ref[...] = jnp.where(ri < ci, r, 0.0)[:LANES, tq - LANES:]

    def init_stats():
        m1_ref[...] = jnp.full_like(m1_ref, -jnp.inf)
        m2_ref[...] = jnp.full_like(m2_ref, -jnp.inf)
        l1_ref[...] = jnp.zeros_like(l1_ref)
        l2_ref[...] = jnp.zeros_like(l2_ref)
        a1_ref[...] = jnp.zeros_like(a1_ref)
        a2_ref[...] = jnp.zeros_like(a2_ref)

    def lanes(x, width):
        return jnp.concatenate([x] * (width // LANES), axis=1)

    def softmax_update(s, m_ref, l_ref):
        m_old = m_ref[...]
        m_new = jnp.maximum(m_old, jnp.max(s, axis=-1, keepdims=True))
        p = jnp.exp2(s - lanes(m_new, tq))
        alpha = jnp.exp2(m_old - m_new)
        l_ref[...] = alpha * l_ref[...] + jnp.sum(p, axis=-1, keepdims=True)
        m_ref[...] = m_new
        return p.astype(BF16), alpha

    def scores(q, kj, slot):
        start = pl.multiple_of(kj * tq, tq)
        s_ref[slot, 0] = lax.dot_general(q[0], k1_ref[pl.ds(start, tq), :], (((1,), (1,)), ((), ())),
                                         preferred_element_type=F32)
        s_ref[slot, 1] = lax.dot_general(q[1], k2_ref[pl.ds(start, tq), :], (((1,), (1,)), ((), ())),
                                         preferred_element_type=F32)

    def consume(kj, slot, bias_ref):
        start = pl.multiple_of(kj * tq, tq)
        vblk = slabs(v_ref, pl.ds(start, tq))
        for stream, (m_ref, l_ref, a_ref) in enumerate(((m1_ref, l1_ref, a1_ref), (m2_ref, l2_ref, a2_ref))):
            s = s_ref[slot, stream]
            if bias_ref is not None:
                s = s + bias_ref[...]
            p, alpha = softmax_update(s, m_ref, l_ref)
            a_ref[...] = lanes(alpha, dv) * a_ref[...] + jnp.dot(p, vblk, preferred_element_type=F32)

    def add_previous_block_bias(kj, slot, qi):
        flag = jnp.where(kj == qi - 1, 1.0, 0.0)
        corner = flag * bs_ref[...]
        for stream in range(2):
            s_ref[slot, stream, :LANES, tq - LANES:] = s_ref[slot, stream, :LANES, tq - LANES:] + corner

    def pair_loop(q, qi, first):
        def pair_body(t, carry):
            kj = first + 2 * t
            scores(q, kj + 1, 1)
            consume(kj, 0, None)
            add_previous_block_bias(kj + 1, 1, qi)
            scores(q, kj + 2, 0)
            consume(kj + 1, 1, None)
            return carry

        lax.fori_loop(0, qi // 2, pair_body, 0)

    lv = lam_ref[...]
    lam = (jnp.exp(jnp.sum(lv[0:1] * lv[1:2], axis=-1, keepdims=True))
           - jnp.exp(jnp.sum(lv[2:3] * lv[3:4], axis=-1, keepdims=True)) + lam_init)

    def finalize(rows):
        o = a1_ref[...] / lanes(l1_ref[...], dv) - lam * (a2_ref[...] / lanes(l2_ref[...], dv))
        o = o * lax.rsqrt(jnp.mean(o * o, axis=-1, keepdims=True) + RMS_EPS_B) * sw_ref[...]
        o = o * (1.0 - lam_init) * _silu(slabs(z_ref, rows).astype(F32))
        o_ref[rows, :] = o.astype(o_ref.dtype)

    for pair in range(q1_ref.shape[0] // (2 * tq)):
        base = 2 * pair * tq
        rows_even, rows_odd = slice(base, base + tq), slice(base + tq, base + 2 * tq)
        q_even = (q1_ref[rows_even, :], q2_ref[rows_even, :])
        q_odd = (q1_ref[rows_odd, :], q2_ref[rows_odd, :])
        qi_even = 2 * (step * (q1_ref.shape[0] // (2 * tq)) + pair)
        qi_odd = qi_even + 1

        init_stats()
        scores(q_even, 0, 0)
        pair_loop(q_even, qi_even, 0)
        scores(q_odd, 0, 1)
        consume(qi_even, 0, bd_ref)
        finalize(rows_even)

        init_stats()
        add_previous_block_bias(0, 1, qi_odd)
        scores(q_odd, 1, 0)
        consume(0, 1, None)
        pair_loop(q_odd, qi_odd, 1)
        consume(qi_odd, 0, bd_ref)
        finalize(rows_odd)


def _diff_attention(h, brow, lam_vecs, subln_w, batch, heads, lam_init):
    M = h.shape[1]
    T = M // batch
    tq = min(ATT_TQ, T // 2)
    rows = 2 * tq * min(ATT_PAIRS_PER_STEP, T // (2 * tq))
    ns = T // rows
    assert ns * rows == T
    dh = HEAD_DIM
    dv = 2 * dh
    return pl.pallas_call(
        functools.partial(_attn_kernel, lam_init),
        grid=(batch, heads, ns),
        in_specs=[pl.BlockSpec((None, rows, dh), lambda b, hh, i: (2 * hh, b * ns + i, 0)),
                  pl.BlockSpec((None, rows, dh), lambda b, hh, i: (2 * hh + 1, b * ns + i, 0)),
                  pl.BlockSpec((None, T, dh), lambda b, hh, i: (2 * heads + 2 * hh, b, 0)),
                  pl.BlockSpec((None, T, dh), lambda b, hh, i: (2 * heads + 2 * hh + 1, b, 0)),
                  pl.BlockSpec((2, T, dh), lambda b, hh, i: (2 * heads + hh, b, 0)),
                  pl.BlockSpec((2, rows, dh), lambda b, hh, i: (3 * heads + hh, b * ns + i, 0)),
                  pl.BlockSpec((None, 1, tq), lambda b, hh, i: (hh, 0, 0)),
                  pl.BlockSpec((4, dh), lambda b, hh, i: (0, 0)),
                  pl.BlockSpec((1, dv), lambda b, hh, i: (0, 0))],
        out_specs=pl.BlockSpec((rows, dv), lambda b, hh, i: (b * ns + i, hh)),
        out_shape=jax.ShapeDtypeStruct((M, heads * dv), BF16),
        scratch_shapes=[pltpu.VMEM((tq, LANES), F32), pltpu.VMEM((tq, LANES), F32), pltpu.VMEM((tq, dv), F32),
                        pltpu.VMEM((tq, LANES), F32), pltpu.VMEM((tq, LANES), F32), pltpu.VMEM((tq, dv), F32),
                        pltpu.VMEM((tq, tq), F32), pltpu.VMEM((LANES, LANES), F32),
                        pltpu.VMEM((2, 2, tq, tq), F32)],
        compiler_params=_cparams(("parallel", "parallel", "arbitrary")),
        name="diff_attention",
    )(h, h, h, h, h, h, brow, lam_vecs, subln_w.reshape(1, dv))


def kernel(x, ln_g, ln_b, rel_bias, a_w_in, a_conv_w, a_a_log, a_dt_bias, a_norm_w, a_w_out,
           b_w_in, b_lam_q1, b_lam_k1, b_lam_q2, b_lam_k2, b_subln_w, b_w_out):
    batch, T, D = x.shape
    M = batch * T
    depth = ln_g.shape[0]
    a_heads = a_a_log.shape[1]
    b_heads = rel_bias.shape[1]
    alpha = (2.0 * depth) ** 0.25
    att_tq = min(ATT_TQ, T // 2)
    assert T % min(GDN_TB, T) == 0 and T % (2 * att_tq) == 0 and att_tq >= LANES

    q_w = 2 * b_heads * HEAD_DIM
    n_in_b = b_w_in.shape[2]
    b_col_scale = jnp.where(jnp.arange(n_in_b) < q_w, HEAD_DIM ** -0.5 * LOG2_E, 1.0)
    ones_d = jnp.ones((D,), F32)
    cast_jobs = {}
    for i in range(depth):
        if i % 2 == 1:
            cast_jobs["b_in", i // 2] = (b_w_in, i // 2, b_col_scale)
        cast_jobs["out", i] = (a_w_out if i % 2 == 0 else b_w_out, i // 2, ones_d)
    bf16_weights = {}

    xf = x.reshape(M, D)
    xb = None
    for i in range(depth):
        j = i // 2
        if i % 2 == 0:
            qkvz_w = 4 * a_heads * HEAD_DIM
            chunk = min(GDN_CHUNK, GDN_TB, T)
            a_w_t = jnp.swapaxes(a_w_in, 1, 2)
            gc, beta, xb = _gates(xf, a_w_t, j, qkvz_w, a_a_log[j], a_dt_bias[j], chunk)
            jobs = list(cast_jobs.items()) if i == 0 else []
            h, cast = _matmul(xb, _cast_weight_transposed(a_w_t, j, qkvz_w), BF16,
                              [job for _, job in jobs])
            bf16_weights.update({name: wb for (name, _), wb in zip(jobs, cast)})
            o = _gated_delta_net(h, a_conv_w[j].T, gc, beta, a_norm_w[j], batch, a_heads)
        else:
            tq = att_tq
            h, _ = _matmul(xb, bf16_weights["b_in", j], BF16, slab_major=True)
            table = rel_bias[_t5_bucket_table(tq)] - rel_bias[REL_BUCKETS - 1][None, :]
            brow = jnp.roll(table[::-1], 1, axis=0).T.reshape(b_heads, 1, tq)
            lam_vecs = jnp.stack([b_lam_q1[j], b_lam_k1[j], b_lam_q2[j], b_lam_k2[j]])
            lam_init = 0.8 - 0.6 * math.exp(-0.3 * i)
            o = _diff_attention(h, brow, lam_vecs, b_subln_w[j], batch, b_heads, lam_init)
        r = _matmul_residual(o, bf16_weights["out", i], xf, alpha)
        xf, xb = _layer_norm(r, ln_g[i], ln_b[i], with_bf16=i + 1 < depth)
    return xf.reshape(batch, T, D)
```
